```python
import math, functools
import jax, jax.numpy as jnp
from jax import lax
import numpy as np

D_MODEL = 1024
BATCH = 2
SEQ = 8192
DEPTH = 1
DEC_BATCH = 32
DEC_SEQ = 8
PAST_LEN = 16384
PAGE_SIZE = 128

HEAD_DIM = 64
HEADS_PER_GROUP = D_MODEL // 128
DILATED_GROUPS = ((128, 1), (512, 4), (2048, 16))
N_DIL = len(DILATED_GROUPS)
N_HEADS_A = N_DIL * HEADS_PER_GROUP
D_A = HEADS_PER_GROUP * HEAD_DIM
D_QKV = N_HEADS_A * HEAD_DIM
BLOCK = 128
SCALE = HEAD_DIM ** -0.5
CHUNK = 128
D_B = D_MODEL // 2
SGU_GROUPS = 4
SGU_GROUP_DIM = D_B // SGU_GROUPS
N_BUCKETS = 32
MAX_DISTANCE = 2048
MOE_GROUPS = 4
EXPERTS_PER_GROUP = 8
N_EXPERTS = MOE_GROUPS * EXPERTS_PER_GROUP
TOP_K = 2
D_EXPERT = D_MODEL // 2
N_ADA = 6
EPS = 1e-6
NEG_INF = -1e30
IN_SPLITS = (D_QKV, 2 * D_QKV, 3 * D_QKV, 3 * D_QKV + D_B, 3 * D_QKV + 2 * D_B, 3 * D_QKV + 2 * D_B + D_MODEL)
IN_COLS = 3 * D_QKV + 2 * D_B + 2 * D_MODEL

kernel_name = 'hybrid_dilated_sgu_hmoe_step'


def _rmsnorm(x, g):
    xf = x.astype(jnp.float32)
    y = xf * lax.rsqrt(jnp.mean(xf * xf, axis=-1, keepdims=True) + EPS)
    return (y * g.astype(jnp.float32)).astype(x.dtype)


def _layernorm(x, g, b):
    xf = x.astype(jnp.float32)
    xc = xf - jnp.mean(xf, axis=-1, keepdims=True)
    y = xc * lax.rsqrt(jnp.mean(xc * xc, axis=-1, keepdims=True) + EPS)
    return (y * g.astype(jnp.float32) + b.astype(jnp.float32)).astype(x.dtype)


def _t5_bucket(dist):
    max_exact = N_BUCKETS // 2
    d = jnp.maximum(dist, 1).astype(jnp.float32)
    large = max_exact + (jnp.log(d / max_exact) / math.log(MAX_DISTANCE / max_exact)
                         * (N_BUCKETS - max_exact)).astype(jnp.int32)
    return jnp.where(dist < max_exact, dist, jnp.minimum(large, N_BUCKETS - 1))


def _probs_lse(logits):
    m = jnp.max(logits, axis=-1, keepdims=True)
    e = jnp.exp(logits - m)
    s = jnp.sum(e, axis=-1, keepdims=True)
    return e / s, (m + jnp.log(s))[..., 0]


def _dilated_prompt(q, k, v, bias_tab, dil, steps):
    B, S, H, E = q.shape
    unit = dil * BLOCK
    s_pad = -(-S // unit) * unit
    m_len = s_pad // dil
    nb = m_len // BLOCK

    def to_sub(a):
        a = jnp.pad(a.astype(jnp.float32), ((0, 0), (0, s_pad - S), (0, 0), (0, 0)))
        a = jnp.moveaxis(a.reshape(B, m_len, dil, H, E), 2, 1)
        return a.reshape(B, dil, nb, BLOCK, H, E)

    def with_prev(a):
        prev = jnp.pad(a, ((0, 0), (0, 0), (1, 0), (0, 0), (0, 0), (0, 0)))[:, :, :-1]
        return jnp.concatenate([prev, a], axis=3)

    qs = to_sub(q)
    kb = with_prev(to_sub(k))
    vb = with_prev(to_sub(v))
    delta = jnp.arange(BLOCK)[:, None] + BLOCK - jnp.arange(2 * BLOCK)[None, :]
    band = (delta >= 0) & (delta <= steps)
    has_prev = (jnp.arange(nb) > 0)[:, None, None] | (jnp.arange(2 * BLOCK) >= BLOCK)[None, None, :]
    valid = band[None] & has_prev
    bias = jnp.transpose(bias_tab.astype(jnp.float32)[_t5_bucket(jnp.maximum(delta, 0) * dil)], (2, 0, 1))
    logits = jnp.einsum('brnqhe,brnkhe->brnhqk', qs, kb) * SCALE + bias
    logits = jnp.where(valid[:, None], logits, NEG_INF)
    probs, lse = _probs_lse(logits)
    o = jnp.einsum('brnhqk,brnkhe->brnqhe', probs, vb)
    lse = jnp.swapaxes(lse, 3, 4)

    def from_sub(a):
        a = a.reshape((B, dil, m_len) + a.shape[4:])
        return jnp.moveaxis(a, 1, 2).reshape((B, s_pad) + a.shape[3:])[:, :S]

    return from_sub(o), from_sub(lse)


def _dilated_sample(q, k_all, v_all, bias_tab, dil, steps):
    B, T, H, E = q.shape
    L = k_all.shape[1] - T
    kk = jnp.arange(steps + 1)
    idx = L + jnp.arange(T)[:, None] - kk[None, :] * dil
    valid = idx >= 0
    idx = jnp.maximum(idx, 0)
    kg = k_all[:, idx].astype(jnp.float32)
    vg = v_all[:, idx].astype(jnp.float32)
    bias = bias_tab.astype(jnp.float32)[_t5_bucket(kk * dil)].T
    logits = jnp.einsum('bthe,btkhe->bhtk', q.astype(jnp.float32), kg) * SCALE + bias[:, None, :]
    logits = jnp.where(valid, logits, NEG_INF)
    probs, lse = _probs_lse(logits)
    o = jnp.einsum('bhtk,btkhe->bthe', probs, vg)
    return o, jnp.swapaxes(lse, 1, 2)


def _merge_dilations(outs, lses):
    w = jax.nn.softmax(jnp.stack(lses), axis=0)
    return jnp.sum(jnp.stack(outs) * w[..., None], axis=0)


def _mixer_a_prompt(q, k, v, rel_bias):
    S = q.shape[1]
    outs, lses, states = [], [], []
    for g, (win, dil) in enumerate(DILATED_GROUPS):
        tab = rel_bias[:, g * HEADS_PER_GROUP:(g + 1) * HEADS_PER_GROUP]
        o, lse = _dilated_prompt(q[:, :, g], k[:, :, g], v[:, :, g], tab, dil, win // dil)
        outs.append(o)
        lses.append(lse)
        keep = min(win, S)
        states.append(jnp.stack([k[:, S - keep:, g], v[:, S - keep:, g]], axis=2))
    return _merge_dilations(outs, lses), states


def _mixer_a_sample(q, k, v, caches, rel_bias):
    outs, lses, states = [], [], []
    for g, (win, dil) in enumerate(DILATED_GROUPS):
        cache = caches[g]
        new_rows = jnp.stack([k[:, :, g], v[:, :, g]], axis=2).astype(cache.dtype)
        buf = jnp.concatenate([cache, new_rows], axis=1)
        tab = rel_bias[:, g * HEADS_PER_GROUP:(g + 1) * HEADS_PER_GROUP]
        o, lse = _dilated_sample(q[:, :, g], buf[:, :, 0], buf[:, :, 1], tab, dil, win // dil)
        outs.append(o)
        lses.append(lse)
        keep = min(win, buf.shape[1])
        states.append(buf[:, buf.shape[1] - keep:])
    return _merge_dilations(outs, lses), states


def _chunk_sgu(u, v, w_spatial, b_spatial):
    B, T, _ = v.shape
    t_pad = -(-T // CHUNK) * CHUNK
    vc = jnp.pad(v, ((0, 0), (0, t_pad - T), (0, 0))).reshape(B, t_pad // CHUNK, CHUNK, SGU_GROUPS, SGU_GROUP_DIM)
    mixed = jnp.einsum('gts,bnsgc->bntgc', jnp.tril(w_spatial), vc) + jnp.transpose(b_spatial)[:, :, None]
    return u * mixed.reshape(B, t_pad, D_B)[:, :T]


def _hier_moe(h, w_rg, b_rg, w_re, b_re, w_eg, w_eu, w_ed):
    N = h.shape[0]
    rows = jnp.arange(N)
    g_logits = jnp.einsum('nd,dg->ng', h, w_rg, preferred_element_type=jnp.float32) + b_rg.astype(jnp.float32)
    g_prob = jax.nn.softmax(g_logits, axis=-1)
    _, g_idx = lax.top_k(g_logits, 1)
    g_idx = g_idx[:, 0]
    g_w = g_prob[rows, g_idx]
    e_logits = jnp.einsum('nd,dge->nge', h, w_re, preferred_element_type=jnp.float32) + b_re.astype(jnp.float32)
    e_sel = e_logits[rows, g_idx]
    e_top, e_idx = lax.top_k(e_sel, TOP_K)
    e_w = jax.nn.softmax(e_top, axis=-1)
    expert_ids = g_idx[:, None] * EXPERTS_PER_GROUP + e_idx
    combine = jnp.sum(jax.nn.one_hot(expert_ids, N_EXPERTS, dtype=jnp.float32)
                      * (g_w[:, None] * e_w)[..., None], axis=1).astype(h.dtype)
    out = jnp.zeros_like(h)
    for e in range(N_EXPERTS):
        a = jax.nn.silu(h @ w_eg[e]) * (h @ w_eu[e])
        out = out + combine[:, e:e + 1] * (a @ w_ed[e])
    return out


def _layer(x, c, attend, w_ada, b_ada, g_norm1, w_in, ln_v_g, ln_v_b, w_spatial, b_spatial,
           w_up_a, w_up_b, w_out, g_norm2, w_route_group, b_route_group, w_route_expert,
           b_route_expert, w_e_gate, w_e_up, w_e_down):
    B, T, D = x.shape
    mod = jax.nn.silu(c) @ w_ada + b_ada
    sh1, sc1, gt1, sh2, sc2, gt2 = jnp.split(mod[:, None, :], N_ADA, axis=-1)
    h = _rmsnorm(x, g_norm1) * (1 + sc1) + sh1
    proj = h @ w_in
    q, k, v, u_b, v_b, gate_a, gate_b = jnp.split(proj, list(IN_SPLITS), axis=-1)
    head_shape = (B, T, N_DIL, HEADS_PER_GROUP, HEAD_DIM)
    o_a, kv_states = attend(q.reshape(head_shape), k.reshape(head_shape), v.reshape(head_shape))
    o_a = o_a.reshape(B, T, D_A).astype(x.dtype)
    u_b = jax.nn.gelu(u_b, approximate=False)
    v_b = _layernorm(jax.nn.gelu(v_b, approximate=False), ln_v_g, ln_v_b)
    o_b = _chunk_sgu(u_b, v_b, w_spatial, b_spatial)
    y = (jax.nn.sigmoid(gate_a) * (o_a @ w_up_a) + jax.nn.sigmoid(gate_b) * (o_b @ w_up_b)) @ w_out
    x = x + gt1 * y
    h2 = _rmsnorm(x, g_norm2) * (1 + sc2) + sh2
    moe = _hier_moe(h2.reshape(B * T, D), w_route_group, b_route_group, w_route_expert, b_route_expert,
                    w_e_gate, w_e_up, w_e_down).reshape(B, T, D)
    x = x + gt2 * moe
    return x, kv_states, v_b


def setup_inputs(seed: int = 0) -> dict:
    key = jax.random.key(seed)
    ks = jax.random.split(key, 28)

    def nrm(k, shape, scale=1.0):
        return jax.random.normal(k, shape, jnp.float32) * scale

    def cache_shape(win):
        return (DEPTH, DEC_BATCH, min(win, PAST_LEN), 2, HEADS_PER_GROUP, HEAD_DIM)

    return {
        'x_prompt': nrm(ks[0], (BATCH, SEQ, D_MODEL)),
        'x_sample': nrm(ks[1], (DEC_BATCH, DEC_SEQ, D_MODEL)),
        'c_prompt': nrm(ks[2], (BATCH, D_MODEL)),
        'c_sample': nrm(ks[3], (DEC_BATCH, D_MODEL)),
        'cache_kv_w128': nrm(ks[4], cache_shape(DILATED_GROUPS[0][0])),
        'cache_kv_w512': nrm(ks[5], cache_shape(DILATED_GROUPS[1][0])),
        'cache_kv_w2048': nrm(ks[6], cache_shape(DILATED_GROUPS[2][0])),
        'rel_bias': nrm(ks[7], (N_BUCKETS, N_HEADS_A), 0.5),
        'w_ada': nrm(ks[8], (DEPTH, D_MODEL, N_ADA * D_MODEL), D_MODEL ** -0.5),
        'b_ada': nrm(ks[9], (DEPTH, N_ADA * D_MODEL), 0.02),
        'g_norm1': 1.0 + nrm(ks[10], (DEPTH, D_MODEL), 0.1),
        'w_in': nrm(ks[11], (DEPTH, D_MODEL, IN_COLS), D_MODEL ** -0.5),
        'ln_v_g': 1.0 + nrm(ks[12], (DEPTH, D_B), 0.1),
        'ln_v_b': nrm(ks[13], (DEPTH, D_B), 0.1),
        'w_spatial': nrm(ks[14], (DEPTH, SGU_GROUPS, CHUNK, CHUNK), CHUNK ** -0.5),
        'b_spatial': 1.0 + nrm(ks[15], (DEPTH, SGU_GROUPS, CHUNK), 0.1),
        'w_up_a': nrm(ks[16], (DEPTH, D_A, D_MODEL), D_A ** -0.5),
        'w_up_b': nrm(ks[17], (DEPTH, D_B, D_MODEL), D_B ** -0.5),
        'w_out': nrm(ks[18], (DEPTH, D_MODEL, D_MODEL), D_MODEL ** -0.5),
        'g_norm2': 1.0 + nrm(ks[19], (DEPTH, D_MODEL), 0.1),
        'w_route_group': nrm(ks[20], (DEPTH, D_MODEL, MOE_GROUPS), D_MODEL ** -0.5),
        'b_route_group': nrm(ks[21], (DEPTH, MOE_GROUPS), 0.01),
        'w_route_expert': nrm(ks[22], (DEPTH, D_MODEL, MOE_GROUPS, EXPERTS_PER_GROUP), D_MODEL ** -0.5),
        'b_route_expert': nrm(ks[23], (DEPTH, MOE_GROUPS, EXPERTS_PER_GROUP), 0.01),
        'w_e_gate': nrm(ks[24], (DEPTH, N_EXPERTS, D_MODEL, D_EXPERT), D_MODEL ** -0.5),
        'w_e_up': nrm(ks[25], (DEPTH, N_EXPERTS, D_MODEL, D_EXPERT), D_MODEL ** -0.5),
        'w_e_down': nrm(ks[26], (DEPTH, N_EXPERTS, D_EXPERT, D_MODEL), D_EXPERT ** -0.5),
        'g_final': 1.0 + nrm(ks[27], (D_MODEL,), 0.1),
    }


def reference(x_prompt, x_sample, c_prompt, c_sample, cache_kv_w128, cache_kv_w512, cache_kv_w2048,
              rel_bias, w_ada, b_ada, g_norm1, w_in, ln_v_g, ln_v_b, w_spatial, b_spatial, w_up_a,
              w_up_b, w_out, g_norm2, w_route_group, b_route_group, w_route_expert, b_route_expert,
              w_e_gate, w_e_up, w_e_down, g_final):
    layer_weights = (w_ada, b_ada, g_norm1, w_in, ln_v_g, ln_v_b, w_spatial, b_spatial, w_up_a, w_up_b,
                     w_out, g_norm2, w_route_group, b_route_group, w_route_expert, b_route_expert,
                     w_e_gate, w_e_up, w_e_down)
    caches = (cache_kv_w128, cache_kv_w512, cache_kv_w2048)
    y_p, y_s = x_prompt, x_sample
    kv_p = [[] for _ in range(N_DIL)]
    kv_s = [[] for _ in range(N_DIL)]
    sgu_v = []
    for l in range(DEPTH):
        lw = [w[l] for w in layer_weights]
        y_p, st_p, _ = _layer(y_p, c_prompt, functools.partial(_mixer_a_prompt, rel_bias=rel_bias), *lw)
        layer_caches = (cache_kv_w128[l], cache_kv_w512[l], cache_kv_w2048[l])
        y_s, st_s, v_s = _layer(y_s, c_sample,
                                functools.partial(_mixer_a_sample, caches=layer_caches, rel_bias=rel_bias), *lw)
        for g in range(N_DIL):
            kv_p[g].append(st_p[g])
            kv_s[g].append(st_s[g])
        sgu_v.append(v_s)
    y_p = _rmsnorm(y_p, g_final)
    y_s = _rmsnorm(y_s, g_final)
    return (y_p, y_s,
            jnp.stack(kv_p[0]), jnp.stack(kv_p[1]), jnp.stack(kv_p[2]),
            jnp.stack(kv_s[0]), jnp.stack(kv_s[1]), jnp.stack(kv_s[2]),
            jnp.stack(sgu_v))
```

```python
import functools
import math

import numpy as np
import jax
import jax.numpy as jnp
from jax import lax
from jax.experimental import pallas as pl
from jax.experimental.pallas import tpu as pltpu

F32 = jnp.float32
BF16 = jnp.bfloat16
HIGHEST = lax.Precision.HIGHEST

D_MODEL = 1024
HEAD_DIM = 64
HEADS = 8
DILATED_GROUPS = ((128, 1), (512, 4), (2048, 16))
N_DIL = 3
D_A = HEADS * HEAD_DIM
D_QKV = N_DIL * D_A
BLOCK = 128
STEPS = 128
SCALE = HEAD_DIM ** -0.5
CHUNK = 128
D_B = 512
SGU_GROUPS = 4
SGU_DIM = D_B // SGU_GROUPS
N_BUCKETS = 32
MAX_DISTANCE = 2048
MOE_GROUPS = 4
EXPERTS_PER_GROUP = 8
N_EXPERTS = 32
D_EXPERT = 512
EPS = 1e-6
NEG_INF = -1e30
LANES = 128
C_U = 3 * D_QKV
C_V = C_U + D_B
C_GA = C_V + D_B
C_GB = C_GA + D_MODEL
IN_COLS = C_GB + D_MODEL

TM_STAGE1 = 256
TM_STAGE2 = 512
TM_MOE = 256
TM_FINAL = 512
VMEM_LIMIT = 56 * 1024 * 1024


def _dot(a, b):
    return jnp.dot(a, b, preferred_element_type=F32)


def _dot_nt(a, b):
    return lax.dot_general(a, b, (((1,), (1,)), ((), ())), preferred_element_type=F32)


def _sigmoid(x):
    return 1.0 / (1.0 + jnp.exp(-x))


def _gelu(x):
    return 0.5 * x * (1.0 + lax.erf(x * (2.0 ** -0.5)))


def _params(n_grid):
    return pltpu.CompilerParams(dimension_semantics=("arbitrary",) * n_grid, vmem_limit_bytes=VMEM_LIMIT)


def _ada_body(c_ref, w_ref, b_ref, o_ref):
    c = c_ref[...]
    s = c * _sigmoid(c)
    o_ref[...] = jnp.dot(s, w_ref[...], precision=HIGHEST, preferred_element_type=F32) + b_ref[...]


def _ada(c_all, w_ada, b_ada):
    n, d = c_all.shape
    cols = w_ada.shape[1]
    tn = 1024
    return pl.pallas_call(
        _ada_body,
        out_shape=jax.ShapeDtypeStruct((n, cols), F32),
        grid=(cols // tn,),
        in_specs=[pl.BlockSpec((n, d), lambda j: (0, 0)),
                  pl.BlockSpec((d, tn), lambda j: (0, j)),
                  pl.BlockSpec((1, tn), lambda j: (0, j))],
        out_specs=pl.BlockSpec((n, tn), lambda j: (0, j)),
        compiler_params=_params(1),
        name="ada",
    )(c_all, w_ada, b_ada.reshape(1, cols))


def _stage1_body(x_ref, sh_ref, sc_ref, g1_ref, w_ref, lng_ref, lnb_ref, ws_ref, bs_ref, *outs,
                 period, emit_kv_bf16, emit_vn):
    outs = list(outs)
    q_ref = outs.pop(0)
    k_ref = outs.pop(0) if emit_kv_bf16 else None
    v_ref = outs.pop(0) if emit_kv_bf16 else None
    kv_refs = [outs.pop(0) for _ in range(N_DIL)]
    ob_ref, ga_ref, gb_ref = outs.pop(0), outs.pop(0), outs.pop(0)
    vn_ref = outs.pop(0) if emit_vn else None

    x = x_ref[...]
    tm = x.shape[0]
    h = x * lax.rsqrt(jnp.mean(x * x, axis=-1, keepdims=True) + EPS) * g1_ref[...]
    h = h * (1.0 + sc_ref[0]) + sh_ref[0]
    hb = h.astype(BF16)

    q_ref[...] = (_dot(hb, w_ref[:, 0:D_QKV]) * SCALE).astype(q_ref.dtype)
    k = _dot(hb, w_ref[:, D_QKV:2 * D_QKV])
    v = _dot(hb, w_ref[:, 2 * D_QKV:3 * D_QKV])
    if emit_kv_bf16:
        k_ref[...] = k.astype(BF16)
        v_ref[...] = v.astype(BF16)
    for g in range(N_DIL):
        rows = kv_refs[g].shape[1]
        kv_refs[g][0, :, 0:D_A] = k[tm - rows:, g * D_A:(g + 1) * D_A]
        kv_refs[g][0, :, D_A:2 * D_A] = v[tm - rows:, g * D_A:(g + 1) * D_A]

    ga_ref[...] = _sigmoid(_dot(hb, w_ref[:, C_GA:C_GB])).astype(BF16)
    gb_ref[...] = _sigmoid(_dot(hb, w_ref[:, C_GB:IN_COLS])).astype(BF16)

    u = _gelu(_dot(hb, w_ref[:, C_U:C_V]))
    vb = _gelu(_dot(hb, w_ref[:, C_V:C_GA]))
    xc = vb - jnp.mean(vb, axis=-1, keepdims=True)
    vn = xc * lax.rsqrt(jnp.mean(xc * xc, axis=-1, keepdims=True) + EPS) * lng_ref[...] + lnb_ref[...]
    if emit_vn:
        vn_ref[...] = vn
    vnb = vn.astype(BF16)

    r = ws_ref.shape[1]
    ri = lax.broadcasted_iota(jnp.int32, (r, r), 0)
    ci = lax.broadcasted_iota(jnp.int32, (r, r), 1)
    keep = ci <= ri
    if period != r:
        sh = int(math.log2(period))
        keep = keep & (jnp.right_shift(ri, sh) == jnp.right_shift(ci, sh))
    for g in range(SGU_GROUPS):
        wg = jnp.where(keep, ws_ref[g], 0.0).astype(BF16)
        cs = slice(g * SGU_DIM, (g + 1) * SGU_DIM)
        for c in range(tm // r):
            rs = slice(c * r, (c + 1) * r)
            mixed = _dot(wg, vnb[rs, cs]) + bs_ref[:, cs]
            ob_ref[rs, cs] = (u[rs, cs] * mixed).astype(BF16)


def _stage1(x2d, sh, sc, g1, w_in_bf, lng, lnb, ws, bs, *, tm, mod_index, tail_shapes, tail_index,
            period, q_dtype, emit_kv_bf16, emit_vn, name):
    n = x2d.shape[0]
    grid = (n // tm,)
    mod_block = (1,) + sh.shape[1:]
    r = ws.shape[1]
    const2 = lambda i: (0, 0)
    row = lambda i: (i, 0)
    in_specs = [
        pl.BlockSpec((tm, D_MODEL), row),
        pl.BlockSpec(mod_block, mod_index),
        pl.BlockSpec(mod_block, mod_index),
        pl.BlockSpec((1, D_MODEL), const2),
        pl.BlockSpec((D_MODEL, IN_COLS), const2, pipeline_mode=pl.Buffered(1)),
        pl.BlockSpec((1, D_B), const2),
        pl.BlockSpec((1, D_B), const2),
        pl.BlockSpec((SGU_GROUPS, r, r), lambda i: (0, 0, 0)),
        pl.BlockSpec((r, D_B), const2),
    ]
    out_shape = [jax.ShapeDtypeStruct((n, D_QKV), q_dtype)]
    out_specs = [pl.BlockSpec((tm, D_QKV), row)]
    if emit_kv_bf16:
        out_shape += [jax.ShapeDtypeStruct((n, D_QKV), BF16)] * 2
        out_specs += [pl.BlockSpec((tm, D_QKV), row)] * 2
    for g in range(N_DIL):
        nb, keep = tail_shapes[g]
        rows = min(tm, keep)
        out_shape.append(jax.ShapeDtypeStruct((nb, keep, 2 * D_A), F32))
        out_specs.append(pl.BlockSpec((1, rows, 2 * D_A), functools.partial(tail_index, g=g, rows=rows)))
    out_shape += [jax.ShapeDtypeStruct((n, D_B), BF16),
                  jax.ShapeDtypeStruct((n, D_MODEL), BF16),
                  jax.ShapeDtypeStruct((n, D_MODEL), BF16)]
    out_specs += [pl.BlockSpec((tm, D_B), row), pl.BlockSpec((tm, D_MODEL), row), pl.BlockSpec((tm, D_MODEL), row)]
    if emit_vn:
        out_shape.append(jax.ShapeDtypeStruct((n, D_B), F32))
        out_specs.append(pl.BlockSpec((tm, D_B), row))
    body = functools.partial(_stage1_body, period=period, emit_kv_bf16=emit_kv_bf16, emit_vn=emit_vn)
    return pl.pallas_call(
        body, out_shape=out_shape, grid=grid, in_specs=in_specs, out_specs=out_specs,
        compiler_params=_params(1), name=name,
    )(x2d, sh, sc, g1, w_in_bf, lng, lnb, ws, bs)


def _attn_prompt_body(q_ref, kp_ref, kc_ref, vp_ref, vc_ref, bias_ref, o_ref, lse_ref):
    n = pl.program_id(2)
    q = q_ref[0]
    kcat = jnp.concatenate([kp_ref[0], kc_ref[0]], axis=0)
    vcat = jnp.concatenate([vp_ref[0], vc_ref[0]], axis=0)
    ri = lax.broadcasted_iota(jnp.int32, (BLOCK, 2 * BLOCK), 0)
    ci = lax.broadcasted_iota(jnp.int32, (BLOCK, 2 * BLOCK), 1)
    delta = ri + BLOCK - ci
    first_key = jnp.where(n > 0, 0, BLOCK)
    valid = (delta >= 0) & (delta <= STEPS) & (ci >= first_key)
    lane = lax.broadcasted_iota(jnp.int32, (BLOCK, LANES), 1)
    lse_all = jnp.zeros((BLOCK, LANES), F32)
    outs = []
    for h in range(HEADS):
        sl = slice(h * HEAD_DIM, (h + 1) * HEAD_DIM)
        s = _dot_nt(q[:, sl], kcat[:, sl]) + bias_ref[h]
        s = jnp.where(valid, s, NEG_INF)
        m = jnp.max(s, axis=-1, keepdims=True)
        p = jnp.exp(s - m)
        l = jnp.sum(p, axis=-1, keepdims=True)
        outs.append(_dot(p.astype(BF16), vcat[:, sl]) / l)
        lse_all = jnp.where(lane == h, m + jnp.log(l), lse_all)
    o_ref[0] = jnp.concatenate(outs, axis=1).astype(BF16)
    lse_ref[0] = lse_all


def _attn_prompt_group(q3, k3, v3, bias, g, dil):
    b_sz, s_len, _ = q3.shape
    m_len = s_len // dil
    nb = m_len // BLOCK
    view = lambda a: a.reshape(b_sz, m_len, dil * D_QKV)
    cur = pl.BlockSpec((1, BLOCK, D_A), lambda b, r, n: (b, n, r * N_DIL + g))
    prev = pl.BlockSpec((1, BLOCK, D_A), lambda b, r, n: (b, jnp.maximum(n - 1, 0), r * N_DIL + g))
    o, lse = pl.pallas_call(
        _attn_prompt_body,
        out_shape=[jax.ShapeDtypeStruct((b_sz, m_len, dil * D_A), BF16),
                   jax.ShapeDtypeStruct((b_sz, m_len, dil * LANES), F32)],
        grid=(b_sz, dil, nb),
        in_specs=[cur, prev, cur, prev, cur,
                  pl.BlockSpec((HEADS, BLOCK, 2 * BLOCK), lambda b, r, n: (0, 0, 0))],
        out_specs=[pl.BlockSpec((1, BLOCK, D_A), lambda b, r, n: (b, n, r)),
                   pl.BlockSpec((1, BLOCK, LANES), lambda b, r, n: (b, n, r))],
        compiler_params=_params(3),
        name=f"attn_prompt_d{dil}",
    )(view(q3), view(k3), view(k3), view(v3), view(v3), bias)
    return o.reshape(b_sz * s_len, D_A), lse.reshape(b_sz * s_len, LANES)


def _attn_sample_body(q_ref, n0_ref, n1_ref, n2_ref, c0_ref, c1_ref, c2_ref,
                      b0_ref, b1_ref, b2_ref, m0_ref, m1_ref, m2_ref, o_ref, *, t_new):
    rows = HEADS * t_new
    q = q_ref[...]
    ri = lax.broadcasted_iota(jnp.int32, (rows, D_A), 0)
    ci = lax.broadcasted_iota(jnp.int32, (rows, D_A), 1)
    own_head = jnp.right_shift(ri, int(math.log2(t_new))) == jnp.right_shift(ci, int(math.log2(HEAD_DIM)))
    pad = jnp.zeros((BLOCK - t_new, D_A), F32)
    caches = []
    c0 = c0_ref[0]
    caches.append((c0[:, 0:D_A], c0[:, D_A:2 * D_A]))
    c1 = c1_ref[0]
    caches.append((c1[:, 0:D_A], c1[:, D_A:2 * D_A]))
    c2 = c2_ref[0]
    row = 2 * D_A
    caches.append((jnp.concatenate([c2[:, r * row:r * row + D_A] for r in range(t_new)], axis=0),
                   jnp.concatenate([c2[:, r * row + D_A:(r + 1) * row] for r in range(t_new)], axis=0)))
    news = (n0_ref, n1_ref, n2_ref)
    biases = (b0_ref, b1_ref, b2_ref)
    masks = (m0_ref, m1_ref, m2_ref)
    parts = []
    for g in range(N_DIL):
        qg = q[:, g * D_A:(g + 1) * D_A]
        qblk = jnp.where(own_head, jnp.concatenate([qg] * HEADS, axis=0), 0.0).astype(BF16)
        new = news[g][0]
        kmat = jnp.concatenate([caches[g][0], new[:, 0:D_A], pad], axis=0).astype(BF16)
        vmat = jnp.concatenate([caches[g][1], new[:, D_A:2 * D_A], pad], axis=0).astype(BF16)
        s = _dot_nt(qblk, kmat) + biases[g][...]
        s = jnp.where(masks[g][...] > 0.5, s, NEG_INF)
        m = jnp.max(s, axis=-1, keepdims=True)
        p = jnp.exp(s - m)
        l = jnp.sum(p, axis=-1, keepdims=True)
        parts.append((_dot(p.astype(BF16), vmat) / l, m + jnp.log(l)))
    top = jnp.maximum(jnp.maximum(parts[0][1], parts[1][1]), parts[2][1])
    es = [jnp.exp(lse - top) for _, lse in parts]
    den = es[0] + es[1] + es[2]
    merged = (es[0] / den) * parts[0][0] + (es[1] / den) * parts[1][0] + (es[2] / den) * parts[2][0]
    merged = jnp.where(own_head, merged, 0.0)
    acc = merged[0:t_new]
    for h in range(1, HEADS):
        acc = acc + merged[h * t_new:(h + 1) * t_new]
    o_ref[...] = acc


def _attn_sample(q_s, kv_new, caches, biases, masks, t_new):
    n = q_s.shape[0]
    nb = n // t_new
    win0, win1 = DILATED_GROUPS[0][0], DILATED_GROUPS[1][0]
    win2, dil2 = DILATED_GROUPS[2]
    c0 = caches[0].reshape(nb, win0, 2 * D_A)
    c1 = caches[1].reshape(nb, win1, 2 * D_A)
    c2 = caches[2].reshape(nb, win2 // dil2, dil2 * 2 * D_A)
    per_b = lambda b: (b, 0, 0)
    const2 = lambda b: (0, 0)
    in_specs = [pl.BlockSpec((t_new, D_QKV), lambda b: (b, 0))]
    in_specs += [pl.BlockSpec((1, t_new, 2 * D_A), per_b)] * 3
    in_specs += [pl.BlockSpec((1, win0, 2 * D_A), per_b),
                 pl.BlockSpec((1, win1, 2 * D_A), per_b),
                 pl.BlockSpec((1, win2 // dil2, t_new * 2 * D_A), per_b)]
    in_specs += [pl.BlockSpec(a.shape, const2) for a in biases]
    in_specs += [pl.BlockSpec(a.shape, const2) for a in masks]
    return pl.pallas_call(
        functools.partial(_attn_sample_body, t_new=t_new),
        out_shape=jax.ShapeDtypeStruct((n, D_A), F32),
        grid=(nb,),
        in_specs=in_specs,
        out_specs=pl.BlockSpec((t_new, D_A), lambda b: (b, 0)),
        compiler_params=_params(1),
        name="attn_sample",
    )(q_s, *[a.reshape(nb, t_new, 2 * D_A) for a in kv_new], c0, c1, c2, *biases, *masks)


def _stage2_body(*refs, merged):
    refs = list(refs)
    if merged:
        oa_ref = refs.pop(0)
    else:
        o_refs = [refs.pop(0) for _ in range(N_DIL)]
        l_refs = [refs.pop(0) for _ in range(N_DIL)]
        ex_ref = refs.pop(0)
    (ob_ref, ga_ref, gb_ref, x_ref, gt1_ref, sh2_ref, sc2_ref, wua_ref, wub_ref, wo_ref, g2_ref,
     wr_ref, br_ref, x1_ref, h2_ref, route_ref, cnt_ref, run_ref) = refs
    step = pl.program_id(0)

    @pl.when(step == 0)
    def _():
        run_ref[...] = jnp.zeros_like(run_ref)

    if merged:
        oa = oa_ref[...].astype(BF16)
    else:
        ls = [r[...] for r in l_refs]
        top = jnp.maximum(jnp.maximum(ls[0], ls[1]), ls[2])
        es = [jnp.exp(l - top) for l in ls]
        den = es[0] + es[1] + es[2]
        oa = None
        for g in range(N_DIL):
            w = es[g] / den
            w_hi = w.astype(BF16)
            w_lo = (w - w_hi.astype(F32)).astype(BF16)
            wx = _dot(w_hi, ex_ref[...]) + _dot(w_lo, ex_ref[...])
            term = wx * o_refs[g][...].astype(F32)
            oa = term if oa is None else oa + term
        oa = oa.astype(BF16)

    ya = _dot(oa, wua_ref[...])
    yb = _dot(ob_ref[...], wub_ref[...])
    z = (ga_ref[...].astype(F32) * ya + gb_ref[...].astype(F32) * yb).astype(BF16)
    x1 = x_ref[...] + gt1_ref[0] * _dot(z, wo_ref[...])
    x1_ref[...] = x1
    h2 = x1 * lax.rsqrt(jnp.mean(x1 * x1, axis=-1, keepdims=True) + EPS) * g2_ref[...]
    h2 = h2 * (1.0 + sc2_ref[0]) + sh2_ref[0]
    h2_ref[...] = h2.astype(BF16)

    logit = jnp.dot(h2, wr_ref[...], precision=HIGHEST, preferred_element_type=F32) + br_ref[...]
    tm = logit.shape[0]
    lane_i = lax.broadcasted_iota(jnp.int32, (tm, LANES), 1)
    lane = lane_i.astype(F32)
    big = float(LANES)
    is_g = (lane_i >= N_EXPERTS) & (lane_i < N_EXPERTS + MOE_GROUPS)
    gl = jnp.where(is_g, logit, -jnp.inf)
    gmax = jnp.max(gl, axis=-1, keepdims=True)
    g_idx = jnp.min(jnp.where(gl == gmax, lane, big), axis=-1, keepdims=True) - N_EXPERTS
    g_w = 1.0 / jnp.sum(jnp.where(is_g, jnp.exp(logit - gmax), 0.0), axis=-1, keepdims=True)
    lo = g_idx * EXPERTS_PER_GROUP
    el = jnp.where((lane >= lo) & (lane < lo + EXPERTS_PER_GROUP), logit, -jnp.inf)
    t1 = jnp.max(el, axis=-1, keepdims=True)
    i1 = jnp.min(jnp.where(el == t1, lane, big), axis=-1, keepdims=True)
    el2 = jnp.where(lane == i1, -jnp.inf, el)
    t2 = jnp.max(el2, axis=-1, keepdims=True)
    i2 = jnp.min(jnp.where(el2 == t2, lane, big), axis=-1, keepdims=True)
    d = jnp.exp(t2 - t1)
    cw1 = g_w * (1.0 / (1.0 + d))
    cw2 = g_w * (d / (1.0 + d))

    oh1 = lane == i1
    oh2 = lane == i2
    oh = jnp.where(oh1 | oh2, 1.0, 0.0)
    ri = lax.broadcasted_iota(jnp.int32, (tm, tm), 0)
    ci = lax.broadcasted_iota(jnp.int32, (tm, tm), 1)
    before = jnp.where(ci < ri, 1.0, 0.0).astype(BF16)
    run = run_ref[0:1, :]
    seen = _dot(before, oh.astype(BF16)) + run
    rank1 = jnp.sum(jnp.where(oh1, seen, 0.0), axis=-1, keepdims=True)
    rank2 = jnp.sum(jnp.where(oh2, seen, 0.0), axis=-1, keepdims=True)
    run_new = jnp.broadcast_to(run + jnp.sum(oh, axis=0, keepdims=True), run_ref.shape)
    run_ref[...] = run_new
    cnt_ref[...] = run_new

    cols = (cw1, cw2, i1, i2, rank1, rank2)
    route = jnp.zeros((tm, LANES), F32)
    for j, col in enumerate(cols):
        route = jnp.where(lane_i == j, col, route)
    route_ref[...] = route


def _stage2(attn_in, ob, ga, gb, x2d, gt1, sh2, sc2, wua, wub, wo, g2, wr, br, *, tm, mod_index, merged, name):
    n = x2d.shape[0]
    row = lambda i: (i, 0)
    const2 = lambda i: (0, 0)
    mod_block = (1,) + gt1.shape[1:]
    if merged:
        attn_specs = [pl.BlockSpec((tm, D_A), row)]
    else:
        attn_specs = [pl.BlockSpec((tm, D_A), row)] * N_DIL + [pl.BlockSpec((tm, LANES), row)] * N_DIL
        attn_specs.append(pl.BlockSpec((LANES, D_A), const2))
    in_specs = attn_specs + [
        pl.BlockSpec((tm, D_B), row), pl.BlockSpec((tm, D_MODEL), row), pl.BlockSpec((tm, D_MODEL), row),
        pl.BlockSpec((tm, D_MODEL), row),
        pl.BlockSpec(mod_block, mod_index), pl.BlockSpec(mod_block, mod_index), pl.BlockSpec(mod_block, mod_index),
        pl.BlockSpec((D_A, D_MODEL), const2), pl.BlockSpec((D_B, D_MODEL), const2),
        pl.BlockSpec((D_MODEL, D_MODEL), const2), pl.BlockSpec((1, D_MODEL), const2),
        pl.BlockSpec((D_MODEL, LANES), const2), pl.BlockSpec((1, LANES), const2),
    ]
    out_shape = [jax.ShapeDtypeStruct((n, D_MODEL), F32), jax.ShapeDtypeStruct((n, D_MODEL), BF16),
                 jax.ShapeDtypeStruct((n, LANES), F32), jax.ShapeDtypeStruct((8, LANES), F32)]
    out_specs = [pl.BlockSpec((tm, D_MODEL), row), pl.BlockSpec((tm, D_MODEL), row),
                 pl.BlockSpec((tm, LANES), row), pl.BlockSpec((8, LANES), const2)]
    return pl.pallas_call(
        functools.partial(_stage2_body, merged=merged),
        out_shape=out_shape, grid=(n // tm,), in_specs=in_specs, out_specs=out_specs,
        scratch_shapes=[pltpu.VMEM((8, LANES), F32)],
        compiler_params=_params(1), name=name,
    )(*attn_in, ob, ga, gb, x2d, gt1, sh2, sc2, wua, wub, wo, g2, wr, br)


def _moe_body(te_ref, nv_ref, x_ref, wg_ref, wu_ref, wd_ref, y_ref, wg_s, wu_s, wd_s):
    i = pl.program_id(0)
    active = i < nv_ref[0]
    fresh = (i == 0) | (te_ref[i] != te_ref[jnp.maximum(i - 1, 0)])

    @pl.when(active & fresh)
    def _():
        wg_s[...] = wg_ref[0].astype(BF16)
        wu_s[...] = wu_ref[0].astype(BF16)
        wd_s[...] = wd_ref[0].astype(BF16)

    @pl.when(active)
    def _():
        x = x_ref[...]
        a = _dot(x, wg_s[...])
        b = _dot(x, wu_s[...])
        mid = (a * _sigmoid(a) * b).astype(BF16)
        y_ref[...] = _dot(mid, wd_s[...]).astype(y_ref.dtype)


def _moe(tile_expert, n_valid, x_sorted, w_eg, w_eu, w_ed, tm):
    cap = x_sorted.shape[0]
    last = lambda i, nv: jnp.minimum(i, nv[0] - 1)
    xmap = lambda i, te, nv: (last(i, nv), 0)
    wmap = lambda i, te, nv: (te[last(i, nv)], 0, 0)
    grid_spec = pltpu.PrefetchScalarGridSpec(
        num_scalar_prefetch=2,
        grid=(cap // tm,),
        in_specs=[pl.BlockSpec((tm, D_MODEL), xmap),
                  pl.BlockSpec((1, D_MODEL, D_EXPERT), wmap),
                  pl.BlockSpec((1, D_MODEL, D_EXPERT), wmap),
                  pl.BlockSpec((1, D_EXPERT, D_MODEL), wmap)],
        out_specs=pl.BlockSpec((tm, D_MODEL), xmap),
        scratch_shapes=[pltpu.VMEM((D_MODEL, D_EXPERT), BF16), pltpu.VMEM((D_MODEL, D_EXPERT), BF16),
                        pltpu.VMEM((D_EXPERT, D_MODEL), BF16)],
    )
    return pl.pallas_call(
        _moe_body, out_shape=jax.ShapeDtypeStruct((cap, D_MODEL), BF16), grid_spec=grid_spec,
        compiler_params=_params(1), name="moe",
    )(tile_expert, n_valid, x_sorted, w_eg, w_eu, w_ed)


def _final_body(x1_ref, y1_ref, y2_ref, route_ref, gt2_ref, gf_ref, o_ref):
    route = route_ref[...]
    moe = route[:, 0:1] * y1_ref[...].astype(F32) + route[:, 1:2] * y2_ref[...].astype(F32)
    x = x1_ref[...] + gt2_ref[0] * moe
    o_ref[...] = x * lax.rsqrt(jnp.mean(x * x, axis=-1, keepdims=True) + EPS) * gf_ref[...]


def _final(x1, y1, y2, route, gt2, gf, *, tm, mod_index, name):
    n = x1.shape[0]
    row = lambda i: (i, 0)
    mod_block = (1,) + gt2.shape[1:]
    return pl.pallas_call(
        _final_body,
        out_shape=jax.ShapeDtypeStruct((n, D_MODEL), F32),
        grid=(n // tm,),
        in_specs=[pl.BlockSpec((tm, D_MODEL), row), pl.BlockSpec((tm, D_MODEL), row),
                  pl.BlockSpec((tm, D_MODEL), row), pl.BlockSpec((tm, LANES), row),
                  pl.BlockSpec(mod_block, mod_index), pl.BlockSpec((1, D_MODEL), lambda i: (0, 0))],
        out_specs=pl.BlockSpec((tm, D_MODEL), row),
        compiler_params=_params(1), name=name,
    )(x1, y1, y2, route, gt2, gf)


def _t5_bucket(dist):
    max_exact = N_BUCKETS // 2
    d = jnp.maximum(dist, 1).astype(F32)
    large = max_exact + (jnp.log(d / max_exact) / math.log(MAX_DISTANCE / max_exact)
                         * (N_BUCKETS - max_exact)).astype(jnp.int32)
    return jnp.where(dist < max_exact, dist, jnp.minimum(large, N_BUCKETS - 1))


def _prompt_bias(tab, dil):
    delta = np.arange(BLOCK)[:, None] + BLOCK - np.arange(2 * BLOCK)[None, :]
    return jnp.transpose(tab[_t5_bucket(jnp.asarray(np.maximum(delta, 0) * dil, jnp.int32))], (2, 0, 1))


def _sample_tables(tab, win, dil, t_new, cache_rows):
    if cache_rows == win:
        buf = np.arange(win)
    else:
        per = win // dil
        buf = (np.arange(per)[None, :] * dil + np.arange(t_new)[:, None]).reshape(-1)
    buf = np.concatenate([buf, win + np.arange(t_new), np.full(BLOCK - t_new, -1)])
    back = win + np.arange(t_new)[:, None] - buf[None, :]
    valid = (buf[None, :] >= 0) & (back >= 0) & (back % dil == 0) & (back // dil <= STEPS)
    bias = tab[_t5_bucket(jnp.asarray(np.maximum(back, 0), jnp.int32))]
    bias = jnp.transpose(bias, (2, 0, 1)).reshape(HEADS * t_new, buf.shape[0])
    mask = np.broadcast_to(valid[None], (HEADS, t_new, buf.shape[0])).reshape(HEADS * t_new, -1)
    return bias, jnp.asarray(mask.astype(np.float32))


def kernel(x_prompt, x_sample, c_prompt, c_sample, cache_kv_w128, cache_kv_w512, cache_kv_w2048, rel_bias, w_ada,
           b_ada, g_norm1, w_in, ln_v_g, ln_v_b, w_spatial, b_spatial, w_up_a, w_up_b, w_out, g_norm2,
           w_route_group, b_route_group, w_route_expert, b_route_expert, w_e_gate, w_e_up, w_e_down, g_final):
    assert w_ada.shape[0] == 1, "single layer"
    bp, s_len, _ = x_prompt.shape
    bs, t_new, _ = x_sample.shape
    n_p, n_s = bp * s_len, bs * t_new
    caches = (cache_kv_w128[0], cache_kv_w512[0], cache_kv_w2048[0])

    mod = _ada(jnp.concatenate([c_prompt, c_sample], axis=0), w_ada[0], b_ada[0])
    mod_p = [mod[:bp, j * D_MODEL:(j + 1) * D_MODEL].reshape(bp, 1, D_MODEL) for j in range(6)]
    mod_s = [jnp.repeat(mod[bp:, j * D_MODEL:(j + 1) * D_MODEL], t_new, axis=0).reshape(1, n_s, D_MODEL)
             for j in range(6)]

    w_in_bf = w_in[0].astype(BF16)
    g1 = g_norm1[0].reshape(1, D_MODEL)
    lng, lnb = ln_v_g[0].reshape(1, D_B), ln_v_b[0].reshape(1, D_B)
    ws_p = w_spatial[0]
    bs_p = jnp.repeat(jnp.transpose(b_spatial[0]), SGU_DIM, axis=1)
    reps = n_s // t_new
    ws_s = jnp.tile(w_spatial[0][:, :t_new, :t_new], (1, reps, reps))
    bs_s = jnp.tile(bs_p[:t_new], (reps, 1))

    tpb = s_len // TM_STAGE1
    tail_p = [(bp, min(win, s_len)) for win, _ in DILATED_GROUPS]

    def tail_index_p(i, g, rows):
        first = tpb - tail_p[g][1] // rows
        return (i // tpb, jnp.maximum(i % tpb - first, 0), 0)

    q_p, k_p, v_p, kv0_p, kv1_p, kv2_p, ob_p, ga_p, gb_p = _stage1(
        x_prompt.reshape(n_p, D_MODEL), mod_p[0], mod_p[1], g1, w_in_bf, lng, lnb, ws_p, bs_p,
        tm=TM_STAGE1, mod_index=lambda i: (i // tpb, 0, 0), tail_shapes=tail_p, tail_index=tail_index_p,
        period=CHUNK, q_dtype=BF16, emit_kv_bf16=True, emit_vn=False, name="stage1_prompt")
    q_s, kv0_s, kv1_s, kv2_s, ob_s, ga_s, gb_s, vn_s = _stage1(
        x_sample.reshape(n_s, D_MODEL), mod_s[0], mod_s[1], g1, w_in_bf, lng, lnb, ws_s, bs_s,
        tm=n_s, mod_index=lambda i: (0, 0, 0), tail_shapes=[(1, n_s)] * N_DIL,
        tail_index=lambda i, g, rows: (0, 0, 0),
        period=t_new, q_dtype=F32, emit_kv_bf16=False, emit_vn=True, name="stage1_sample")

    q3, k3, v3 = (a.reshape(bp, s_len, D_QKV) for a in (q_p, k_p, v_p))
    o_groups, lse_groups = [], []
    sb, sm = [], []
    for g, (win, dil) in enumerate(DILATED_GROUPS):
        tab = rel_bias[:, g * HEADS:(g + 1) * HEADS].astype(F32)
        o, lse = _attn_prompt_group(q3, k3, v3, _prompt_bias(tab, dil), g, dil)
        o_groups.append(o)
        lse_groups.append(lse)
        cache_rows = win if dil <= t_new else (win // dil) * t_new
        bias, mask = _sample_tables(tab, win, dil, t_new, cache_rows)
        sb.append(bias)
        sm.append(mask)
    oa_s = _attn_sample(q_s, (kv0_s, kv1_s, kv2_s), caches, sb, sm, t_new)

    wua, wub, wo = w_up_a[0].astype(BF16), w_up_b[0].astype(BF16), w_out[0].astype(BF16)
    g2 = g_norm2[0].reshape(1, D_MODEL)
    wr = jnp.concatenate([w_route_expert[0].reshape(D_MODEL, N_EXPERTS), w_route_group[0]], axis=1)
    wr = jnp.pad(wr, ((0, 0), (0, LANES - wr.shape[1])))
    br = jnp.concatenate([b_route_expert[0].reshape(N_EXPERTS), b_route_group[0]])
    br = jnp.pad(br, (0, LANES - br.shape[0])).reshape(1, LANES)
    expand = jnp.asarray((np.arange(LANES)[:, None] == np.arange(D_A)[None, :] // HEAD_DIM), BF16)

    tpb2 = s_len // TM_STAGE2
    x1_p, h2_p, route_p, cnt_p = _stage2(
        o_groups + lse_groups + [expand], ob_p, ga_p, gb_p, x_prompt.reshape(n_p, D_MODEL),
        mod_p[2], mod_p[3], mod_p[4], wua, wub, wo, g2, wr, br,
        tm=TM_STAGE2, mod_index=lambda i: (i // tpb2, 0, 0), merged=False, name="stage2_prompt")
    x1_s, h2_s, route_s, cnt_s = _stage2(
        [oa_s], ob_s, ga_s, gb_s, x_sample.reshape(n_s, D_MODEL),
        mod_s[2], mod_s[3], mod_s[4], wua, wub, wo, g2, wr, br,
        tm=n_s, mod_index=lambda i: (0, 0, 0), merged=True, name="stage2_sample")

    n_all = n_p + n_s
    cap = 2 * n_all + N_EXPERTS * TM_MOE
    n_tiles = cap // TM_MOE
    route = jnp.concatenate([route_p, route_s], axis=0)
    expert = route[:, 2:4].astype(jnp.int32)
    rank = route[:, 4:6].astype(jnp.int32)
    count_p = cnt_p[0, :N_EXPERTS].astype(jnp.int32)
    count_s = cnt_s[0, :N_EXPERTS].astype(jnp.int32)
    rank = rank + jnp.where(jnp.arange(n_all)[:, None] >= n_p, count_p[expert], 0)
    padded = ((count_p + count_s + TM_MOE - 1) // TM_MOE) * TM_MOE
    ends = jnp.cumsum(padded)
    pos = (ends - padded)[expert] + rank
    n_valid = (ends[-1] // TM_MOE).astype(jnp.int32).reshape(1)
    tile_start = jnp.arange(n_tiles, dtype=jnp.int32) * TM_MOE
    tile_expert = jnp.minimum(jnp.sum(ends[None, :] <= tile_start[:, None], axis=1), N_EXPERTS - 1).astype(jnp.int32)
    token_of = jnp.zeros((cap,), jnp.int32).at[pos.reshape(-1)].set(jnp.repeat(jnp.arange(n_all, dtype=jnp.int32), 2))
    h2_all = jnp.concatenate([h2_p, h2_s], axis=0)
    x_sorted = jnp.take(h2_all, token_of, axis=0)

    y_sorted = _moe(tile_expert, n_valid, x_sorted, w_e_gate[0], w_e_up[0], w_e_down[0], TM_MOE)
    y1 = jnp.take(y_sorted, pos[:, 0], axis=0)
    y2 = jnp.take(y_sorted, pos[:, 1], axis=0)

    gf = g_final.reshape(1, D_MODEL)
    tpb3 = s_len // TM_FINAL
    y_p = _final(x1_p, y1[:n_p], y2[:n_p], route_p, mod_p[5], gf,
                 tm=TM_FINAL, mod_index=lambda i: (i // tpb3, 0, 0), name="final_prompt")
    y_s = _final(x1_s, y1[n_p:], y2[n_p:], route_s, mod_s[5], gf,
                 tm=n_s, mod_index=lambda i: (0, 0, 0), name="final_sample")

    kv_p = [a.reshape(1, bp, a.shape[1], 2, HEADS, HEAD_DIM) for a in (kv0_p, kv1_p, kv2_p)]
    kv_s = []
    for cache, new in zip(caches, (kv0_s, kv1_s, kv2_s)):
        new = new.reshape(bs, t_new, 2, HEADS, HEAD_DIM).astype(cache.dtype)
        kv_s.append(jnp.concatenate([cache[:, t_new:], new], axis=1)[None])
    return (y_p.reshape(bp, s_len, D_MODEL), y_s.reshape(bs, t_new, D_MODEL),
            kv_p[0], kv_p[1], kv_p[2], kv_s[0], kv_s[1], kv_s[2],
            vn_s.reshape(1, bs, t_new, D_B))
```

```python
import functools
import math

import numpy as np
import jax
import jax.numpy as jnp
from jax import lax
from jax.experimental import pallas as pl
from jax.experimental.pallas import tpu as pltpu

F32 = jnp.float32
BF16 = jnp.bfloat16
HIGHEST = lax.Precision.HIGHEST

D_MODEL = 1024
HEAD_DIM = 64
HEADS = 8
DILATED_GROUPS = ((128, 1), (512, 4), (2048, 16))
N_DIL = 3
D_A = HEADS * HEAD_DIM
D_QKV = N_DIL * D_A
BLOCK = 128
STEPS = 128
SCALE = HEAD_DIM ** -0.5
CHUNK = 128
D_B = 512
SGU_GROUPS = 4
SGU_DIM = D_B // SGU_GROUPS
N_BUCKETS = 32
MAX_DISTANCE = 2048
MOE_GROUPS = 4
EXPERTS_PER_GROUP = 8
N_EXPERTS = 32
D_EXPERT = 512
EPS = 1e-6
NEG_INF = -1e30
LANES = 128
C_U = 3 * D_QKV
C_V = C_U + D_B
C_GA = C_V + D_B
C_GB = C_GA + D_MODEL
IN_COLS = C_GB + D_MODEL

TM_STAGE1 = 256
TM_STAGE2 = 512
TM_MOE = 256
TM_FINAL = 512
VMEM_LIMIT = 56 * 1024 * 1024


def _dot(a, b):
    return jnp.dot(a, b, preferred_element_type=F32)


def _dot_nt(a, b):
    return lax.dot_general(a, b, (((1,), (1,)), ((), ())), preferred_element_type=F32)


def _sigmoid(x):
    return 1.0 / (1.0 + jnp.exp(-x))


def _gelu(x):
    return 0.5 * x * (1.0 + lax.erf(x * (2.0 ** -0.5)))


def _params(n_grid):
    return pltpu.CompilerParams(dimension_semantics=("arbitrary",) * n_grid, vmem_limit_bytes=VMEM_LIMIT)


def _row_copy(src, src_row, dst, dst_row, sem):
    return pltpu.make_async_copy(src.at[pl.ds(src_row, 1)], dst.at[pl.ds(dst_row, 1)], sem)


def _ada_body(c_ref, w_ref, b_ref, o_ref):
    c = c_ref[...]
    s = c * _sigmoid(c)
    o_ref[...] = jnp.dot(s, w_ref[...], precision=HIGHEST, preferred_element_type=F32) + b_ref[...]


def _ada(c_all, w_ada, b_ada):
    n, d = c_all.shape
    cols = w_ada.shape[1]
    tn = 1024
    return pl.pallas_call(
        _ada_body,
        out_shape=jax.ShapeDtypeStruct((n, cols), F32),
        grid=(cols // tn,),
        in_specs=[pl.BlockSpec((n, d), lambda j: (0, 0)),
                  pl.BlockSpec((d, tn), lambda j: (0, j)),
                  pl.BlockSpec((1, tn), lambda j: (0, j))],
        out_specs=pl.BlockSpec((n, tn), lambda j: (0, j)),
        compiler_params=_params(1),
        name="ada",
    )(c_all, w_ada, b_ada.reshape(1, cols))


def _stage1_body(x_ref, sh_ref, sc_ref, g1_ref, w_ref, lng_ref, lnb_ref, ws_ref, bs_ref, *rest,
                 period, by_stride, emit_vn):
    rest = list(rest)
    if by_stride:
        qkv_refs = [[rest.pop(0) for _ in range(N_DIL)] for _ in range(3)]
    else:
        q_ref = rest.pop(0)
    kv_refs = [rest.pop(0) for _ in range(N_DIL)]
    ob_ref, ga_ref, gb_ref = rest.pop(0), rest.pop(0), rest.pop(0)
    vn_ref = rest.pop(0) if emit_vn else None
    stage_ref = rest.pop(0) if by_stride else None

    x = x_ref[...]
    tm = x.shape[0]
    h = x * lax.rsqrt(jnp.mean(x * x, axis=-1, keepdims=True) + EPS) * g1_ref[...]
    h = h * (1.0 + sc_ref[0]) + sh_ref[0]
    hb = h.astype(BF16)

    q = _dot(hb, w_ref[:, 0:D_QKV]) * SCALE
    k = _dot(hb, w_ref[:, D_QKV:2 * D_QKV])
    v = _dot(hb, w_ref[:, 2 * D_QKV:3 * D_QKV])
    if by_stride:
        per_group = D_A // LANES
        for val, refs in zip((q, k, v), qkv_refs):
            for c in range(D_QKV // LANES):
                stage_ref[c] = val[:, c * LANES:(c + 1) * LANES]
            for g, (_, dil) in enumerate(DILATED_GROUPS):
                for r in range(dil):
                    for c in range(per_group):
                        piece = stage_ref[g * per_group + c, pl.ds(r, tm // dil, stride=dil), :]
                        refs[g][0, r, :, c * LANES:(c + 1) * LANES] = piece.astype(BF16)
    else:
        q_ref[...] = q
    for g in range(N_DIL):
        rows = kv_refs[g].shape[1]
        kv_refs[g][0, :, 0:D_A] = k[tm - rows:, g * D_A:(g + 1) * D_A]
        kv_refs[g][0, :, D_A:2 * D_A] = v[tm - rows:, g * D_A:(g + 1) * D_A]

    ga_ref[...] = _sigmoid(_dot(hb, w_ref[:, C_GA:C_GB])).astype(BF16)
    gb_ref[...] = _sigmoid(_dot(hb, w_ref[:, C_GB:IN_COLS])).astype(BF16)

    u = _gelu(_dot(hb, w_ref[:, C_U:C_V]))
    vb = _gelu(_dot(hb, w_ref[:, C_V:C_GA]))
    xc = vb - jnp.mean(vb, axis=-1, keepdims=True)
    vn = xc * lax.rsqrt(jnp.mean(xc * xc, axis=-1, keepdims=True) + EPS) * lng_ref[...] + lnb_ref[...]
    if emit_vn:
        vn_ref[...] = vn
    vnb = vn.astype(BF16)

    r = ws_ref.shape[1]
    ri = lax.broadcasted_iota(jnp.int32, (r, r), 0)
    ci = lax.broadcasted_iota(jnp.int32, (r, r), 1)
    keep = ci <= ri
    if period != r:
        sh = int(math.log2(period))
        keep = keep & (jnp.right_shift(ri, sh) == jnp.right_shift(ci, sh))
    for g in range(SGU_GROUPS):
        wg = jnp.where(keep, ws_ref[g], 0.0).astype(BF16)
        cs = slice(g * SGU_DIM, (g + 1) * SGU_DIM)
        for c in range(tm // r):
            rs = slice(c * r, (c + 1) * r)
            mixed = _dot(wg, vnb[rs, cs]) + bs_ref[:, cs]
            ob_ref[rs, cs] = (u[rs, cs] * mixed).astype(BF16)


def _stage1(x2d, sh, sc, g1, w_in_bf, lng, lnb, ws, bs, *, tm, mod_index, stride_index, tail_shapes, tail_index,
            period, by_stride, emit_vn, name):
    n = x2d.shape[0]
    grid = (n // tm,)
    mod_block = (1,) + sh.shape[1:]
    r = ws.shape[1]
    const2 = lambda i: (0, 0)
    row = lambda i: (i, 0)
    in_specs = [
        pl.BlockSpec((tm, D_MODEL), row),
        pl.BlockSpec(mod_block, mod_index),
        pl.BlockSpec(mod_block, mod_index),
        pl.BlockSpec((1, D_MODEL), const2),
        pl.BlockSpec((D_MODEL, IN_COLS), const2, pipeline_mode=pl.Buffered(1)),
        pl.BlockSpec((1, D_B), const2),
        pl.BlockSpec((1, D_B), const2),
        pl.BlockSpec((SGU_GROUPS, r, r), lambda i: (0, 0, 0)),
        pl.BlockSpec((r, D_B), const2),
    ]
    out_shape, out_specs, scratch = [], [], []
    if by_stride:
        n_batch, s_len = stride_index[0], stride_index[1]
        for _ in range(3):
            for _, dil in DILATED_GROUPS:
                out_shape.append(jax.ShapeDtypeStruct((n_batch, dil, s_len // dil, D_A), BF16))
                out_specs.append(pl.BlockSpec((1, dil, tm // dil, D_A), stride_index[2]))
        scratch.append(pltpu.VMEM((D_QKV // LANES, tm, LANES), F32))
    else:
        out_shape.append(jax.ShapeDtypeStruct((n, D_QKV), F32))
        out_specs.append(pl.BlockSpec((tm, D_QKV), row))
    for g in range(N_DIL):
        nb, keep = tail_shapes[g]
        rows = min(tm, keep)
        out_shape.append(jax.ShapeDtypeStruct((nb, keep, 2 * D_A), F32))
        out_specs.append(pl.BlockSpec((1, rows, 2 * D_A), functools.partial(tail_index, g=g, rows=rows)))
    out_shape += [jax.ShapeDtypeStruct((n, D_B), BF16),
                  jax.ShapeDtypeStruct((n, D_MODEL), BF16),
                  jax.ShapeDtypeStruct((n, D_MODEL), BF16)]
    out_specs += [pl.BlockSpec((tm, D_B), row), pl.BlockSpec((tm, D_MODEL), row), pl.BlockSpec((tm, D_MODEL), row)]
    if emit_vn:
        out_shape.append(jax.ShapeDtypeStruct((n, D_B), F32))
        out_specs.append(pl.BlockSpec((tm, D_B), row))
    body = functools.partial(_stage1_body, period=period, by_stride=by_stride, emit_vn=emit_vn)
    return pl.pallas_call(
        body, out_shape=out_shape, grid=grid, in_specs=in_specs, out_specs=out_specs, scratch_shapes=scratch,
        compiler_params=_params(1), name=name,
    )(x2d, sh, sc, g1, w_in_bf, lng, lnb, ws, bs)


def _attn_prompt_body(q_ref, kp_ref, kc_ref, vp_ref, vc_ref, bias_ref, o_ref, lse_ref):
    n = pl.program_id(2)
    q = q_ref[0, 0]
    kcat = jnp.concatenate([kp_ref[0, 0], kc_ref[0, 0]], axis=0)
    vcat = jnp.concatenate([vp_ref[0, 0], vc_ref[0, 0]], axis=0)
    ri = lax.broadcasted_iota(jnp.int32, (BLOCK, 2 * BLOCK), 0)
    ci = lax.broadcasted_iota(jnp.int32, (BLOCK, 2 * BLOCK), 1)
    delta = ri + BLOCK - ci
    first_key = jnp.where(n > 0, 0, BLOCK)
    valid = (delta >= 0) & (delta <= STEPS) & (ci >= first_key)
    lane = lax.broadcasted_iota(jnp.int32, (BLOCK, LANES), 1)
    lse_all = jnp.zeros((BLOCK, LANES), F32)
    outs = []
    for h in range(HEADS):
        sl = slice(h * HEAD_DIM, (h + 1) * HEAD_DIM)
        s = _dot_nt(q[:, sl], kcat[:, sl]) + bias_ref[h]
        s = jnp.where(valid, s, NEG_INF)
        m = jnp.max(s, axis=-1, keepdims=True)
        p = jnp.exp(s - m)
        l = jnp.sum(p, axis=-1, keepdims=True)
        outs.append(_dot(p.astype(BF16), vcat[:, sl]) / l)
        lse_all = jnp.where(lane == h, m + jnp.log(l), lse_all)
    o_ref[0, 0] = jnp.concatenate(outs, axis=1).astype(BF16)
    lse_ref[0, 0] = lse_all


def _attn_prompt_group(q4, k4, v4, bias, dil):
    b_sz, _, m_len, _ = q4.shape
    nb = m_len // BLOCK
    cur = pl.BlockSpec((1, 1, BLOCK, D_A), lambda b, r, n: (b, r, n, 0))
    prev = pl.BlockSpec((1, 1, BLOCK, D_A), lambda b, r, n: (b, r, jnp.maximum(n - 1, 0), 0))
    return pl.pallas_call(
        _attn_prompt_body,
        out_shape=[jax.ShapeDtypeStruct((b_sz, dil, m_len, D_A), BF16),
                   jax.ShapeDtypeStruct((b_sz, dil, m_len, LANES), F32)],
        grid=(b_sz, dil, nb),
        in_specs=[cur, prev, cur, prev, cur,
                  pl.BlockSpec((HEADS, BLOCK, 2 * BLOCK), lambda b, r, n: (0, 0, 0))],
        out_specs=[pl.BlockSpec((1, 1, BLOCK, D_A), lambda b, r, n: (b, r, n, 0)),
                   pl.BlockSpec((1, 1, BLOCK, LANES), lambda b, r, n: (b, r, n, 0))],
        compiler_params=_params(3),
        name=f"attn_prompt_d{dil}",
    )(q4, k4, k4, v4, v4, bias)


def _sample_cache_body(*refs, t_new):
    refs = list(refs)
    q_ref = refs.pop(0)
    new_refs = [refs.pop(0) for _ in range(N_DIL)]
    tail_refs = [refs.pop(0) for _ in range(N_DIL)]
    cache_refs = [refs.pop(0) for _ in range(N_DIL)]
    bias_refs = [refs.pop(0) for _ in range(N_DIL)]
    mask_refs = [refs.pop(0) for _ in range(N_DIL)]
    out_refs = [refs.pop(0) for _ in range(N_DIL)]
    o_ref = refs.pop(0)
    p_refs = [refs.pop(0) for _ in range(N_DIL)]
    scale_ref = refs.pop(0)
    kv = pl.program_id(1)
    rows = HEADS * t_new

    lane = lax.broadcasted_iota(jnp.int32, (D_A, LANES), 1)
    for g in range(N_DIL):
        win = cache_refs[g].shape[-1]
        rolled = pltpu.roll(cache_refs[g][0, 0], win - t_new, axis=1)
        if win > LANES:
            out_refs[g][0, 0, :, 0:win - LANES] = rolled[:, 0:win - LANES]
        out_refs[g][0, 0, :, win - LANES:win] = jnp.where(lane >= LANES - t_new, tail_refs[g][0, 0],
                                                          rolled[:, win - LANES:win])

    ri = lax.broadcasted_iota(jnp.int32, (rows, D_A), 0)
    ci = lax.broadcasted_iota(jnp.int32, (rows, D_A), 1)
    own_head = jnp.right_shift(ri, int(math.log2(t_new))) == jnp.right_shift(ci, int(math.log2(HEAD_DIM)))
    pad = jnp.zeros((LANES - t_new, D_A), F32)
    glane = lax.broadcasted_iota(jnp.int32, (rows, LANES), 1)

    @pl.when(kv == 0)
    def _():
        q = q_ref[...]
        stats = []
        for g in range(N_DIL):
            win = cache_refs[g].shape[-1]
            qg = q[:, g * D_A:(g + 1) * D_A]
            qblk = jnp.where(own_head, jnp.concatenate([qg] * HEADS, axis=0), 0.0).astype(BF16)
            s_c = _dot(qblk, cache_refs[g][0, 0].astype(BF16)) + bias_refs[g][:, 0:win]
            s_c = jnp.where(mask_refs[g][:, 0:win] > 0.5, s_c, NEG_INF)
            k_new = jnp.concatenate([new_refs[g][0][:, 0:D_A], pad], axis=0).astype(BF16)
            s_n = _dot_nt(qblk, k_new) + bias_refs[g][:, win:]
            s_n = jnp.where(mask_refs[g][:, win:] > 0.5, s_n, NEG_INF)
            m = jnp.maximum(jnp.max(s_c, axis=-1, keepdims=True), jnp.max(s_n, axis=-1, keepdims=True))
            p_c = jnp.exp(s_c - m)
            p_n = jnp.exp(s_n - m)
            l = jnp.sum(p_c, axis=-1, keepdims=True) + jnp.sum(p_n, axis=-1, keepdims=True)
            p_refs[g][:, 0:win] = p_c.astype(BF16)
            p_refs[g][:, win:] = p_n.astype(BF16)
            stats.append((l, m + jnp.log(l)))
        top = jnp.maximum(jnp.maximum(stats[0][1], stats[1][1]), stats[2][1])
        es = [jnp.exp(lse - top) for _, lse in stats]
        den = es[0] + es[1] + es[2]
        scale = jnp.zeros((rows, LANES), F32)
        for g in range(N_DIL):
            scale = jnp.where(glane == g, es[g] / den / stats[g][0], scale)
        scale_ref[...] = scale

    @pl.when(kv == 1)
    def _():
        acc = jnp.zeros((rows, D_A), F32)
        for g in range(N_DIL):
            win = cache_refs[g].shape[-1]
            part = _dot_nt(p_refs[g][:, 0:win], cache_refs[g][0, 0].astype(BF16))
            v_new = jnp.concatenate([new_refs[g][0][:, D_A:2 * D_A], pad], axis=0).astype(BF16)
            part = part + _dot(p_refs[g][:, win:], v_new)
            acc = acc + scale_ref[:, g:g + 1] * part
        acc = jnp.where(own_head, acc, 0.0)
        out = acc[0:t_new]
        for h in range(1, HEADS):
            out = out + acc[h * t_new:(h + 1) * t_new]
        o_ref[...] = out


def _sample_cache(q_s, kv_new, tails, caches_t, biases, masks, t_new):
    n = q_s.shape[0]
    nb = n // t_new
    blk4 = lambda b, kv: (b, kv, 0, 0)
    const2 = lambda b, kv: (0, 0)
    in_specs = [pl.BlockSpec((t_new, D_QKV), lambda b, kv: (b, 0))]
    in_specs += [pl.BlockSpec((1, t_new, 2 * D_A), lambda b, kv: (b, 0, 0))] * N_DIL
    in_specs += [pl.BlockSpec((1, 1, D_A, LANES), blk4)] * N_DIL
    in_specs += [pl.BlockSpec((1, 1, D_A, c.shape[-1]), blk4) for c in caches_t]
    in_specs += [pl.BlockSpec(a.shape, const2) for a in biases]
    in_specs += [pl.BlockSpec(a.shape, const2) for a in masks]
    out_shape = [jax.ShapeDtypeStruct(c.shape, c.dtype) for c in caches_t]
    out_specs = [pl.BlockSpec((1, 1, D_A, c.shape[-1]), blk4) for c in caches_t]
    out_shape.append(jax.ShapeDtypeStruct((n, D_A), F32))
    out_specs.append(pl.BlockSpec((t_new, D_A), lambda b, kv: (b, 0)))
    rows = HEADS * t_new
    scratch = [pltpu.VMEM((rows, c.shape[-1] + LANES), BF16) for c in caches_t]
    scratch.append(pltpu.VMEM((rows, LANES), F32))
    return pl.pallas_call(
        functools.partial(_sample_cache_body, t_new=t_new),
        out_shape=out_shape, grid=(nb, 2), in_specs=in_specs, out_specs=out_specs, scratch_shapes=scratch,
        compiler_params=_params(2), name="sample_cache",
    )(q_s, *[a.reshape(nb, t_new, 2 * D_A) for a in kv_new], *tails, *caches_t, *biases, *masks)


def _stage2_body(*refs, merged):
    refs = list(refs)
    if merged:
        oa_ref = refs.pop(0)
    else:
        o_refs = [refs.pop(0) for _ in range(N_DIL)]
        l_refs = [refs.pop(0) for _ in range(N_DIL)]
        ex_ref = refs.pop(0)
    (ob_ref, ga_ref, gb_ref, x_ref, gt1_ref, sh2_ref, sc2_ref, wua_ref, wub_ref, wo_ref, g2_ref,
     wr_ref, br_ref, x1_ref, h2_ref, route_ref, cnt_ref, run_ref) = refs[:18]
    step = pl.program_id(0)
    tm = x_ref.shape[0]

    @pl.when(step == 0)
    def _():
        run_ref[...] = jnp.zeros_like(run_ref)

    if merged:
        oa = oa_ref[...].astype(BF16)
    else:
        o_stage, l_stage = refs[18], refs[19]

        def by_position(ref, stage, dil):
            if dil == 1:
                return ref[0, 0].astype(F32)
            slabs = ref.shape[-1] // LANES
            for r in range(dil):
                rows = ref[0, r].astype(F32)
                for c in range(slabs):
                    stage[c, pl.ds(r, tm // dil, stride=dil), :] = rows[:, c * LANES:(c + 1) * LANES]
            return jnp.concatenate([stage[c] for c in range(slabs)], axis=1)

        ls = [by_position(l_refs[g], l_stage, dil) for g, (_, dil) in enumerate(DILATED_GROUPS)]
        top = jnp.maximum(jnp.maximum(ls[0], ls[1]), ls[2])
        es = [jnp.exp(l - top) for l in ls]
        den = es[0] + es[1] + es[2]
        oa = None
        for g, (_, dil) in enumerate(DILATED_GROUPS):
            w = es[g] / den
            w_hi = w.astype(BF16)
            w_lo = (w - w_hi.astype(F32)).astype(BF16)
            wx = _dot(w_hi, ex_ref[...]) + _dot(w_lo, ex_ref[...])
            term = wx * by_position(o_refs[g], o_stage, dil)
            oa = term if oa is None else oa + term
        oa = oa.astype(BF16)

    ya = _dot(oa, wua_ref[...])
    yb = _dot(ob_ref[...], wub_ref[...])
    z = (ga_ref[...].astype(F32) * ya + gb_ref[...].astype(F32) * yb).astype(BF16)
    x1 = x_ref[...] + gt1_ref[0] * _dot(z, wo_ref[...])
    x1_ref[...] = x1
    h2 = x1 * lax.rsqrt(jnp.mean(x1 * x1, axis=-1, keepdims=True) + EPS) * g2_ref[...]
    h2 = h2 * (1.0 + sc2_ref[0]) + sh2_ref[0]
    h2_ref[...] = h2

    logit = jnp.dot(h2, wr_ref[...], precision=HIGHEST, preferred_element_type=F32) + br_ref[...]
    lane_i = lax.broadcasted_iota(jnp.int32, (tm, LANES), 1)
    lane = lane_i.astype(F32)
    big = float(LANES)
    is_g = (lane_i >= N_EXPERTS) & (lane_i < N_EXPERTS + MOE_GROUPS)
    gl = jnp.where(is_g, logit, -jnp.inf)
    gmax = jnp.max(gl, axis=-1, keepdims=True)
    g_idx = jnp.min(jnp.where(gl == gmax, lane, big), axis=-1, keepdims=True) - N_EXPERTS
    g_w = 1.0 / jnp.sum(jnp.where(is_g, jnp.exp(logit - gmax), 0.0), axis=-1, keepdims=True)
    lo = g_idx * EXPERTS_PER_GROUP
    el = jnp.where((lane >= lo) & (lane < lo + EXPERTS_PER_GROUP), logit, -jnp.inf)
    t1 = jnp.max(el, axis=-1, keepdims=True)
    i1 = jnp.min(jnp.where(el == t1, lane, big), axis=-1, keepdims=True)
    el2 = jnp.where(lane == i1, -jnp.inf, el)
    t2 = jnp.max(el2, axis=-1, keepdims=True)
    i2 = jnp.min(jnp.where(el2 == t2, lane, big), axis=-1, keepdims=True)
    d = jnp.exp(t2 - t1)
    cw1 = g_w * (1.0 / (1.0 + d))
    cw2 = g_w * (d / (1.0 + d))

    oh1 = lane == i1
    oh2 = lane == i2
    oh = jnp.where(oh1 | oh2, 1.0, 0.0)
    ri = lax.broadcasted_iota(jnp.int32, (tm, tm), 0)
    ci = lax.broadcasted_iota(jnp.int32, (tm, tm), 1)
    before = jnp.where(ci < ri, 1.0, 0.0).astype(BF16)
    run = run_ref[0:1, :]
    seen = _dot(before, oh.astype(BF16)) + run
    rank1 = jnp.sum(jnp.where(oh1, seen, 0.0), axis=-1, keepdims=True)
    rank2 = jnp.sum(jnp.where(oh2, seen, 0.0), axis=-1, keepdims=True)
    run_new = jnp.broadcast_to(run + jnp.sum(oh, axis=0, keepdims=True), run_ref.shape)
    run_ref[...] = run_new
    cnt_ref[...] = run_new

    cols = (cw1, cw2, i1, i2, rank1, rank2)
    route = jnp.zeros((tm, LANES), F32)
    for j, col in enumerate(cols):
        route = jnp.where(lane_i == j, col, route)
    route_ref[...] = route


def _stage2(attn_in, ob, ga, gb, x2d, gt1, sh2, sc2, wua, wub, wo, g2, wr, br, *, tm, mod_index, stride_index,
            merged, name):
    n = x2d.shape[0]
    row = lambda i: (i, 0)
    const2 = lambda i: (0, 0)
    mod_block = (1,) + gt1.shape[1:]
    scratch = [pltpu.VMEM((8, LANES), F32)]
    if merged:
        attn_specs = [pl.BlockSpec((tm, D_A), row)]
    else:
        attn_specs = [pl.BlockSpec((1, dil, tm // dil, D_A), stride_index) for _, dil in DILATED_GROUPS]
        attn_specs += [pl.BlockSpec((1, dil, tm // dil, LANES), stride_index) for _, dil in DILATED_GROUPS]
        attn_specs.append(pl.BlockSpec((LANES, D_A), const2))
        scratch += [pltpu.VMEM((D_A // LANES, tm, LANES), F32), pltpu.VMEM((1, tm, LANES), F32)]
    in_specs = attn_specs + [
        pl.BlockSpec((tm, D_B), row), pl.BlockSpec((tm, D_MODEL), row), pl.BlockSpec((tm, D_MODEL), row),
        pl.BlockSpec((tm, D_MODEL), row),
        pl.BlockSpec(mod_block, mod_index), pl.BlockSpec(mod_block, mod_index), pl.BlockSpec(mod_block, mod_index),
        pl.BlockSpec((D_A, D_MODEL), const2), pl.BlockSpec((D_B, D_MODEL), const2),
        pl.BlockSpec((D_MODEL, D_MODEL), const2), pl.BlockSpec((1, D_MODEL), const2),
        pl.BlockSpec((D_MODEL, LANES), const2), pl.BlockSpec((1, LANES), const2),
    ]
    out_shape = [jax.ShapeDtypeStruct((n, D_MODEL), F32), jax.ShapeDtypeStruct((n, D_MODEL), F32),
                 jax.ShapeDtypeStruct((n, LANES), F32), jax.ShapeDtypeStruct((8, LANES), F32)]
    out_specs = [pl.BlockSpec((tm, D_MODEL), row), pl.BlockSpec((tm, D_MODEL), row),
                 pl.BlockSpec((tm, LANES), row), pl.BlockSpec((8, LANES), const2)]
    return pl.pallas_call(
        functools.partial(_stage2_body, merged=merged),
        out_shape=out_shape, grid=(n // tm,), in_specs=in_specs, out_specs=out_specs,
        scratch_shapes=scratch,
        compiler_params=_params(1), name=name,
    )(*attn_in, ob, ga, gb, x2d, gt1, sh2, sc2, wua, wub, wo, g2, wr, br)


def _dispatch_body(cnt_ref, off_ref, pad_ref, nv_ref, posp_ref, poss_ref, hp_ref, hs_ref, xs_ref, zero_ref, sem,
                   *, tiles_p, n_tiles):
    step = pl.program_id(0)
    tile = zero_ref.shape[0]

    @pl.when(step == 0)
    def _():
        zero_ref[...] = jnp.zeros_like(zero_ref)

        def per_expert(e, total):
            n = pad_ref[e] - cnt_ref[e]
            base = off_ref[e] + cnt_ref[e]

            def put(i, c):
                _row_copy(zero_ref, 0, xs_ref, base + i, sem).start()
                return c

            lax.fori_loop(0, n, put, 0)
            return total + n

        total = lax.fori_loop(0, N_EXPERTS, per_expert, 0)

        def drain(i, c):
            _row_copy(zero_ref, 0, xs_ref, 0, sem).wait()
            return c

        lax.fori_loop(0, total, drain, 0)

        def tile_copy(j):
            return pltpu.make_async_copy(zero_ref, xs_ref.at[pl.ds(j * tile, tile)], sem)

        def put_tile(j, c):
            tile_copy(j).start()
            return c

        def drain_tile(j, c):
            tile_copy(j).wait()
            return c

        lax.fori_loop(nv_ref[0], n_tiles, put_tile, 0)
        lax.fori_loop(nv_ref[0], n_tiles, drain_tile, 0)

    def scatter(h_ref, pos_ref):
        tm = h_ref.shape[0]

        def issue(t, c):
            _row_copy(h_ref, t, xs_ref, pos_ref[0, 0, 2 * t], sem).start()
            _row_copy(h_ref, t, xs_ref, pos_ref[0, 0, 2 * t + 1], sem).start()
            return c

        lax.fori_loop(0, tm, issue, 0, unroll=8)

        def drain(t, c):
            _row_copy(h_ref, 0, xs_ref, 0, sem).wait()
            _row_copy(h_ref, 0, xs_ref, 0, sem).wait()
            return c

        lax.fori_loop(0, tm, drain, 0, unroll=8)

    @pl.when(step < tiles_p)
    def _():
        scatter(hp_ref, posp_ref)

    @pl.when(step == tiles_p)
    def _():
        scatter(hs_ref, poss_ref)


def _dispatch(count, offset, padded, n_valid, pos_p, pos_s, h2_p, h2_s, *, cap, tm, tile):
    n_p, n_s = h2_p.shape[0], h2_s.shape[0]
    tiles_p = n_p // tm
    last_p = lambda i, *_: (jnp.minimum(i, tiles_p - 1), 0, 0)
    in_specs = [pl.BlockSpec((1, 1, 2 * tm), last_p, memory_space=pltpu.SMEM),
                pl.BlockSpec((1, 1, 2 * n_s), lambda i, *_: (0, 0, 0), memory_space=pltpu.SMEM),
                pl.BlockSpec((tm, D_MODEL), lambda i, *_: (jnp.minimum(i, tiles_p - 1), 0)),
                pl.BlockSpec((n_s, D_MODEL), lambda i, *_: (0, 0))]
    grid_spec = pltpu.PrefetchScalarGridSpec(
        num_scalar_prefetch=4, grid=(tiles_p + 1,), in_specs=in_specs,
        out_specs=pl.BlockSpec(memory_space=pl.ANY),
        scratch_shapes=[pltpu.VMEM((tile, D_MODEL), F32), pltpu.SemaphoreType.DMA(())],
    )
    return pl.pallas_call(
        functools.partial(_dispatch_body, tiles_p=tiles_p, n_tiles=cap // tile),
        out_shape=jax.ShapeDtypeStruct((cap, D_MODEL), F32), grid_spec=grid_spec,
        compiler_params=pltpu.CompilerParams(dimension_semantics=("arbitrary",), vmem_limit_bytes=VMEM_LIMIT,
                                             has_side_effects=True),
        name="dispatch",
    )(count, offset, padded, n_valid, pos_p.reshape(tiles_p, 1, 2 * tm), pos_s.reshape(1, 1, 2 * n_s), h2_p, h2_s)


def _moe_body(te_ref, nv_ref, x_ref, wg_ref, wu_ref, wd_ref, y_ref, wg_s, wu_s, wd_s):
    i = pl.program_id(0)
    active = i < nv_ref[0]
    fresh = (i == 0) | (te_ref[i] != te_ref[jnp.maximum(i - 1, 0)])

    @pl.when(active & fresh)
    def _():
        wg_s[...] = wg_ref[0].astype(BF16)
        wu_s[...] = wu_ref[0].astype(BF16)
        wd_s[...] = wd_ref[0].astype(BF16)

    @pl.when(active)
    def _():
        x = x_ref[...].astype(BF16)
        a = _dot(x, wg_s[...])
        b = _dot(x, wu_s[...])
        mid = (a * _sigmoid(a) * b).astype(BF16)
        y_ref[...] = _dot(mid, wd_s[...])

    @pl.when(jnp.logical_not(active))
    def _():
        y_ref[...] = jnp.zeros_like(y_ref)


def _moe(tile_expert, n_valid, x_sorted, w_eg, w_eu, w_ed, tm):
    cap = x_sorted.shape[0]
    last = lambda i, nv: jnp.minimum(i, nv[0] - 1)
    xmap = lambda i, te, nv: (last(i, nv), 0)
    wmap = lambda i, te, nv: (te[last(i, nv)], 0, 0)
    grid_spec = pltpu.PrefetchScalarGridSpec(
        num_scalar_prefetch=2,
        grid=(cap // tm,),
        in_specs=[pl.BlockSpec((tm, D_MODEL), xmap),
                  pl.BlockSpec((1, D_MODEL, D_EXPERT), wmap),
                  pl.BlockSpec((1, D_MODEL, D_EXPERT), wmap),
                  pl.BlockSpec((1, D_EXPERT, D_MODEL), wmap)],
        out_specs=pl.BlockSpec((tm, D_MODEL), lambda i, te, nv: (i, 0)),
        scratch_shapes=[pltpu.VMEM((D_MODEL, D_EXPERT), BF16), pltpu.VMEM((D_MODEL, D_EXPERT), BF16),
                        pltpu.VMEM((D_EXPERT, D_MODEL), BF16)],
    )
    return pl.pallas_call(
        _moe_body, out_shape=jax.ShapeDtypeStruct((cap, D_MODEL), F32), grid_spec=grid_spec,
        compiler_params=_params(1), name="moe",
    )(tile_expert, n_valid, x_sorted, w_eg, w_eu, w_ed)


def _final_body(pos_ref, x1_ref, route_ref, gt2_ref, gf_ref, ys_ref, o_ref, y1_buf, y2_buf, sem):
    tm = x1_ref.shape[0]

    def issue(t, c):
        _row_copy(ys_ref, pos_ref[0, 0, 2 * t], y1_buf, t, sem).start()
        _row_copy(ys_ref, pos_ref[0, 0, 2 * t + 1], y2_buf, t, sem).start()
        return c

    lax.fori_loop(0, tm, issue, 0, unroll=8)

    def drain(t, c):
        _row_copy(ys_ref, 0, y1_buf, 0, sem).wait()
        _row_copy(ys_ref, 0, y2_buf, 0, sem).wait()
        return c

    lax.fori_loop(0, tm, drain, 0, unroll=8)

    route = route_ref[...]
    moe = route[:, 0:1] * y1_buf[...] + route[:, 1:2] * y2_buf[...]
    x = x1_ref[...] + gt2_ref[0] * moe
    o_ref[...] = x * lax.rsqrt(jnp.mean(x * x, axis=-1, keepdims=True) + EPS) * gf_ref[...]


def _final(pos, x1, route, gt2, gf, y_sorted, *, tm, mod_index, name):
    n = x1.shape[0]
    row = lambda i: (i, 0)
    mod_block = (1,) + gt2.shape[1:]
    return pl.pallas_call(
        _final_body,
        out_shape=jax.ShapeDtypeStruct((n, D_MODEL), F32),
        grid=(n // tm,),
        in_specs=[pl.BlockSpec((1, 1, 2 * tm), lambda i: (i, 0, 0), memory_space=pltpu.SMEM),
                  pl.BlockSpec((tm, D_MODEL), row), pl.BlockSpec((tm, LANES), row),
                  pl.BlockSpec(mod_block, mod_index), pl.BlockSpec((1, D_MODEL), lambda i: (0, 0)),
                  pl.BlockSpec(memory_space=pl.ANY)],
        out_specs=pl.BlockSpec((tm, D_MODEL), row),
        scratch_shapes=[pltpu.VMEM((tm, D_MODEL), F32), pltpu.VMEM((tm, D_MODEL), F32),
                        pltpu.SemaphoreType.DMA(())],
        compiler_params=_params(1), name=name,
    )(pos.reshape(n // tm, 1, 2 * tm), x1, route, gt2, gf, y_sorted)


def _t5_bucket(dist):
    max_exact = N_BUCKETS // 2
    d = jnp.maximum(dist, 1).astype(F32)
    large = max_exact + (jnp.log(d / max_exact) / math.log(MAX_DISTANCE / max_exact)
                         * (N_BUCKETS - max_exact)).astype(jnp.int32)
    return jnp.where(dist < max_exact, dist, jnp.minimum(large, N_BUCKETS - 1))


def _prompt_bias(tab, dil):
    delta = np.arange(BLOCK)[:, None] + BLOCK - np.arange(2 * BLOCK)[None, :]
    return jnp.transpose(tab[_t5_bucket(jnp.asarray(np.maximum(delta, 0) * dil, jnp.int32))], (2, 0, 1))


def _sample_tables(tab, win, dil, t_new):
    buf = np.concatenate([np.arange(win), win + np.arange(t_new), np.full(LANES - t_new, -1)])
    back = win + np.arange(t_new)[:, None] - buf[None, :]
    valid = (buf[None, :] >= 0) & (back >= 0) & (back % dil == 0) & (back // dil <= STEPS)
    bias = tab[_t5_bucket(jnp.asarray(np.maximum(back, 0), jnp.int32))]
    bias = jnp.transpose(bias, (2, 0, 1)).reshape(HEADS * t_new, buf.shape[0])
    mask = np.broadcast_to(valid[None], (HEADS, t_new, buf.shape[0])).reshape(HEADS * t_new, -1)
    return bias, jnp.asarray(mask.astype(np.float32))


def kernel(x_prompt, x_sample, c_prompt, c_sample, cache_kv_w128, cache_kv_w512, cache_kv_w2048, rel_bias, w_ada,
           b_ada, g_norm1, w_in, ln_v_g, ln_v_b, w_spatial, b_spatial, w_up_a, w_up_b, w_out, g_norm2,
           w_route_group, b_route_group, w_route_expert, b_route_expert, w_e_gate, w_e_up, w_e_down, g_final):
    assert w_ada.shape[0] == 1, "single layer"
    bp, s_len, _ = x_prompt.shape
    bs, t_new, _ = x_sample.shape
    n_p, n_s = bp * s_len, bs * t_new
    caches = (cache_kv_w128[0], cache_kv_w512[0], cache_kv_w2048[0])
    assert all(c.shape[1] == win for c, (win, _) in zip(caches, DILATED_GROUPS)), "cache holds one full window"

    mod = _ada(jnp.concatenate([c_prompt, c_sample], axis=0), w_ada[0], b_ada[0])
    mod_p = [mod[:bp, j * D_MODEL:(j + 1) * D_MODEL].reshape(bp, 1, D_MODEL) for j in range(6)]
    mod_s = [jnp.repeat(mod[bp:, j * D_MODEL:(j + 1) * D_MODEL], t_new, axis=0).reshape(1, n_s, D_MODEL)
             for j in range(6)]

    w_in_bf = w_in[0].astype(BF16)
    g1 = g_norm1[0].reshape(1, D_MODEL)
    lng, lnb = ln_v_g[0].reshape(1, D_B), ln_v_b[0].reshape(1, D_B)
    ws_p = w_spatial[0]
    bs_p = jnp.repeat(jnp.transpose(b_spatial[0]), SGU_DIM, axis=1)
    reps = n_s // t_new
    ws_s = jnp.tile(w_spatial[0][:, :t_new, :t_new], (1, reps, reps))
    bs_s = jnp.tile(bs_p[:t_new], (reps, 1))

    tpb = s_len // TM_STAGE1
    tail_p = [(bp, min(win, s_len)) for win, _ in DILATED_GROUPS]

    def tail_index_p(i, g, rows):
        first = tpb - tail_p[g][1] // rows
        return (i // tpb, jnp.maximum(i % tpb - first, 0), 0)

    outs = _stage1(
        x_prompt.reshape(n_p, D_MODEL), mod_p[0], mod_p[1], g1, w_in_bf, lng, lnb, ws_p, bs_p,
        tm=TM_STAGE1, mod_index=lambda i: (i // tpb, 0, 0),
        stride_index=(bp, s_len, lambda i: (i // tpb, 0, i % tpb, 0)), tail_shapes=tail_p, tail_index=tail_index_p,
        period=CHUNK, by_stride=True, emit_vn=False, name="stage1_prompt")
    q_g, k_g, v_g = outs[0:3], outs[3:6], outs[6:9]
    kv0_p, kv1_p, kv2_p, ob_p, ga_p, gb_p = outs[9:]
    q_s, kv0_s, kv1_s, kv2_s, ob_s, ga_s, gb_s, vn_s = _stage1(
        x_sample.reshape(n_s, D_MODEL), mod_s[0], mod_s[1], g1, w_in_bf, lng, lnb, ws_s, bs_s,
        tm=n_s, mod_index=lambda i: (0, 0, 0), stride_index=None, tail_shapes=[(1, n_s)] * N_DIL,
        tail_index=lambda i, g, rows: (0, 0, 0),
        period=t_new, by_stride=False, emit_vn=True, name="stage1_sample")

    o_groups, lse_groups = [], []
    sb, sm = [], []
    for g, (win, dil) in enumerate(DILATED_GROUPS):
        tab = rel_bias[:, g * HEADS:(g + 1) * HEADS].astype(F32)
        o, lse = _attn_prompt_group(q_g[g], k_g[g], v_g[g], _prompt_bias(tab, dil), dil)
        o_groups.append(o)
        lse_groups.append(lse)
        bias, mask = _sample_tables(tab, win, dil, t_new)
        sb.append(bias)
        sm.append(mask)
    caches_t = [jnp.transpose(c, (0, 2, 3, 4, 1)).reshape(bs, 2, D_A, c.shape[1]) for c in caches]
    tails = []
    for new in (kv0_s, kv1_s, kv2_s):
        new_t = jnp.transpose(new.reshape(bs, t_new, 2, D_A), (0, 2, 3, 1))
        tails.append(jnp.pad(new_t, ((0, 0), (0, 0), (0, 0), (LANES - t_new, 0))))
    cache0_t, cache1_t, cache2_t, oa_s = _sample_cache(q_s, (kv0_s, kv1_s, kv2_s), tails, caches_t, sb, sm, t_new)

    wua, wub, wo = w_up_a[0].astype(BF16), w_up_b[0].astype(BF16), w_out[0].astype(BF16)
    g2 = g_norm2[0].reshape(1, D_MODEL)
    wr = jnp.concatenate([w_route_expert[0].reshape(D_MODEL, N_EXPERTS), w_route_group[0]], axis=1)
    wr = jnp.pad(wr, ((0, 0), (0, LANES - wr.shape[1])))
    br = jnp.concatenate([b_route_expert[0].reshape(N_EXPERTS), b_route_group[0]])
    br = jnp.pad(br, (0, LANES - br.shape[0])).reshape(1, LANES)
    expand = jnp.asarray((np.arange(LANES)[:, None] == np.arange(D_A)[None, :] // HEAD_DIM), BF16)

    tpb2 = s_len // TM_STAGE2
    x1_p, h2_p, route_p, cnt_p = _stage2(
        o_groups + lse_groups + [expand], ob_p, ga_p, gb_p, x_prompt.reshape(n_p, D_MODEL),
        mod_p[2], mod_p[3], mod_p[4], wua, wub, wo, g2, wr, br,
        tm=TM_STAGE2, mod_index=lambda i: (i // tpb2, 0, 0), stride_index=lambda i: (i // tpb2, 0, i % tpb2, 0),
        merged=False, name="stage2_prompt")
    x1_s, h2_s, route_s, cnt_s = _stage2(
        [oa_s], ob_s, ga_s, gb_s, x_sample.reshape(n_s, D_MODEL),
        mod_s[2], mod_s[3], mod_s[4], wua, wub, wo, g2, wr, br,
        tm=n_s, mod_index=lambda i: (0, 0, 0), stride_index=None, merged=True, name="stage2_sample")

    cap = 2 * (n_p + n_s) + N_EXPERTS * TM_MOE
    n_tiles = cap // TM_MOE
    count_p = cnt_p[0, :N_EXPERTS].astype(jnp.int32)
    count = count_p + cnt_s[0, :N_EXPERTS].astype(jnp.int32)
    padded = ((count + TM_MOE - 1) // TM_MOE) * TM_MOE
    ends = jnp.cumsum(padded)
    offset = ends - padded
    n_valid = (ends[-1] // TM_MOE).astype(jnp.int32).reshape(1)
    tile_start = jnp.arange(n_tiles, dtype=jnp.int32) * TM_MOE
    tile_expert = jnp.minimum(jnp.sum(ends[None, :] <= tile_start[:, None], axis=1), N_EXPERTS - 1).astype(jnp.int32)

    def sorted_rows(route, earlier):
        expert = route[:, 2:4].astype(jnp.int32)
        return offset[expert] + earlier[expert] + route[:, 4:6].astype(jnp.int32)

    pos_p = sorted_rows(route_p, jnp.zeros_like(count_p))
    pos_s = sorted_rows(route_s, count_p)
    x_sorted = _dispatch(count, offset, padded, n_valid, pos_p, pos_s, h2_p, h2_s, cap=cap, tm=TM_FINAL, tile=TM_MOE)

    y_sorted = _moe(tile_expert, n_valid, x_sorted, w_e_gate[0], w_e_up[0], w_e_down[0], TM_MOE)

    gf = g_final.reshape(1, D_MODEL)
    tpb3 = s_len // TM_FINAL
    y_p = _final(pos_p, x1_p, route_p, mod_p[5], gf, y_sorted,
                 tm=TM_FINAL, mod_index=lambda i: (i // tpb3, 0, 0), name="final_prompt")
    y_s = _final(pos_s, x1_s, route_s, mod_s[5], gf, y_sorted,
                 tm=n_s, mod_index=lambda i: (0, 0, 0), name="final_sample")

    kv_p = [a.reshape(1, bp, a.shape[1], 2, HEADS, HEAD_DIM) for a in (kv0_p, kv1_p, kv2_p)]
    kv_s = [jnp.transpose(c.reshape(bs, 2, HEADS, HEAD_DIM, c.shape[-1]), (0, 4, 1, 2, 3))[None]
            for c in (cache0_t, cache1_t, cache2_t)]
    return (y_p.reshape(bp, s_len, D_MODEL), y_s.reshape(bs, t_new, D_MODEL),
            kv_p[0], kv_p[1], kv_p[2], kv_s[0], kv_s[1], kv_s[2],
            vn_s.reshape(1, bs, t_new, D_B))
```

```python
import functools
import math

import numpy as np
import jax
import jax.numpy as jnp
from jax import lax
from jax.experimental import pallas as pl
from jax.experimental.pallas import tpu as pltpu

F32 = jnp.float32
BF16 = jnp.bfloat16
HIGHEST = lax.Precision.HIGHEST

D_MODEL = 1024
HEAD_DIM = 64
HEADS = 8
DILATED_GROUPS = ((128, 1), (512, 4), (2048, 16))
N_DIL = 3
D_A = HEADS * HEAD_DIM
D_QKV = N_DIL * D_A
BLOCK = 128
STEPS = 128
SCALE = HEAD_DIM ** -0.5
CHUNK = 128
D_B = 512
SGU_GROUPS = 4
SGU_DIM = D_B // SGU_GROUPS
N_BUCKETS = 32
MAX_DISTANCE = 2048
MOE_GROUPS = 4
EXPERTS_PER_GROUP = 8
N_EXPERTS = 32
D_EXPERT = 512
EPS = 1e-6
NEG_INF = -1e30
LANES = 128
C_U = 3 * D_QKV
C_V = C_U + D_B
C_GA = C_V + D_B
C_GB = C_GA + D_MODEL
IN_COLS = C_GB + D_MODEL

TM_STAGE1 = 256
TM_STAGE2 = 512
TM_MOE = 512
ATTN_BLOCKS_PER_STEP = 4
TM_FINAL = 512
VMEM_LIMIT = 56 * 1024 * 1024


def _dot(a, b):
    return jnp.dot(a, b, preferred_element_type=F32)


def _dot_nt(a, b):
    return lax.dot_general(a, b, (((1,), (1,)), ((), ())), preferred_element_type=F32)


def _sigmoid(x):
    return 1.0 / (1.0 + jnp.exp(-x))


def _gelu(x):
    return 0.5 * x * (1.0 + lax.erf(x * (2.0 ** -0.5)))


def _params(n_grid):
    return pltpu.CompilerParams(dimension_semantics=("arbitrary",) * n_grid, vmem_limit_bytes=VMEM_LIMIT)


def _row_copy(src, src_row, dst, dst_row, sem):
    return pltpu.make_async_copy(src.at[pl.ds(src_row, 1)], dst.at[pl.ds(dst_row, 1)], sem)


def _ada_body(c_ref, w_ref, b_ref, o_ref):
    c = c_ref[...]
    s = c * _sigmoid(c)
    o_ref[...] = jnp.dot(s, w_ref[...], precision=HIGHEST, preferred_element_type=F32) + b_ref[...]


def _ada(c_all, w_ada, b_ada):
    n, d = c_all.shape
    cols = w_ada.shape[1]
    tn = 1024
    return pl.pallas_call(
        _ada_body,
        out_shape=jax.ShapeDtypeStruct((n, cols), F32),
        grid=(cols // tn,),
        in_specs=[pl.BlockSpec((n, d), lambda j: (0, 0)),
                  pl.BlockSpec((d, tn), lambda j: (0, j)),
                  pl.BlockSpec((1, tn), lambda j: (0, j))],
        out_specs=pl.BlockSpec((n, tn), lambda j: (0, j)),
        compiler_params=_params(1),
        name="ada",
    )(c_all, w_ada, b_ada.reshape(1, cols))


def _stage1_body(x_ref, sh_ref, sc_ref, g1_ref, w_ref, lng_ref, lnb_ref, ws_ref, bs_ref, *rest,
                 period, by_stride, emit_vn):
    rest = list(rest)
    if by_stride:
        qkv_refs = [[rest.pop(0) for _ in range(N_DIL)] for _ in range(3)]
    else:
        q_ref = rest.pop(0)
    kv_refs = [rest.pop(0) for _ in range(N_DIL)]
    ob_ref, ga_ref, gb_ref = rest.pop(0), rest.pop(0), rest.pop(0)
    vn_ref = rest.pop(0) if emit_vn else None
    stage_ref = rest.pop(0) if by_stride else None

    x = x_ref[...]
    tm = x.shape[0]
    h = x * lax.rsqrt(jnp.mean(x * x, axis=-1, keepdims=True) + EPS) * g1_ref[...]
    h = h * (1.0 + sc_ref[0]) + sh_ref[0]
    hb = h.astype(BF16)

    q = _dot(hb, w_ref[:, 0:D_QKV]) * SCALE
    k = _dot(hb, w_ref[:, D_QKV:2 * D_QKV])
    v = _dot(hb, w_ref[:, 2 * D_QKV:3 * D_QKV])
    if by_stride:
        per_group = D_A // LANES
        for val, refs in zip((q, k, v), qkv_refs):
            for c in range(D_QKV // LANES):
                stage_ref[c] = val[:, c * LANES:(c + 1) * LANES]
            for g, (_, dil) in enumerate(DILATED_GROUPS):
                for r in range(dil):
                    for c in range(per_group):
                        piece = stage_ref[g * per_group + c, pl.ds(r, tm // dil, stride=dil), :]
                        refs[g][0, r, :, c * LANES:(c + 1) * LANES] = piece.astype(BF16)
    else:
        q_ref[...] = q
    for g in range(N_DIL):
        rows = kv_refs[g].shape[1]
        kv_refs[g][0, :, 0:D_A] = k[tm - rows:, g * D_A:(g + 1) * D_A]
        kv_refs[g][0, :, D_A:2 * D_A] = v[tm - rows:, g * D_A:(g + 1) * D_A]

    ga_ref[...] = _sigmoid(_dot(hb, w_ref[:, C_GA:C_GB])).astype(BF16)
    gb_ref[...] = _sigmoid(_dot(hb, w_ref[:, C_GB:IN_COLS])).astype(BF16)

    u = _gelu(_dot(hb, w_ref[:, C_U:C_V]))
    vb = _gelu(_dot(hb, w_ref[:, C_V:C_GA]))
    xc = vb - jnp.mean(vb, axis=-1, keepdims=True)
    vn = xc * lax.rsqrt(jnp.mean(xc * xc, axis=-1, keepdims=True) + EPS) * lng_ref[...] + lnb_ref[...]
    if emit_vn:
        vn_ref[...] = vn
    vnb = vn.astype(BF16)

    r = ws_ref.shape[1]
    ri = lax.broadcasted_iota(jnp.int32, (r, r), 0)
    ci = lax.broadcasted_iota(jnp.int32, (r, r), 1)
    keep = ci <= ri
    if period != r:
        sh = int(math.log2(period))
        keep = keep & (jnp.right_shift(ri, sh) == jnp.right_shift(ci, sh))
    for g in range(SGU_GROUPS):
        wg = jnp.where(keep, ws_ref[g], 0.0).astype(BF16)
        cs = slice(g * SGU_DIM, (g + 1) * SGU_DIM)
        for c in range(tm // r):
            rs = slice(c * r, (c + 1) * r)
            mixed = _dot(wg, vnb[rs, cs]) + bs_ref[:, cs]
            ob_ref[rs, cs] = (u[rs, cs] * mixed).astype(BF16)


def _stage1(x2d, sh, sc, g1, w_in_bf, lng, lnb, ws, bs, *, tm, mod_index, stride_index, tail_shapes, tail_index,
            period, by_stride, emit_vn, name):
    n = x2d.shape[0]
    grid = (n // tm,)
    mod_block = (1,) + sh.shape[1:]
    r = ws.shape[1]
    const2 = lambda i: (0, 0)
    row = lambda i: (i, 0)
    in_specs = [
        pl.BlockSpec((tm, D_MODEL), row),
        pl.BlockSpec(mod_block, mod_index),
        pl.BlockSpec(mod_block, mod_index),
        pl.BlockSpec((1, D_MODEL), const2),
        pl.BlockSpec((D_MODEL, IN_COLS), const2, pipeline_mode=pl.Buffered(1)),
        pl.BlockSpec((1, D_B), const2),
        pl.BlockSpec((1, D_B), const2),
        pl.BlockSpec((SGU_GROUPS, r, r), lambda i: (0, 0, 0)),
        pl.BlockSpec((r, D_B), const2),
    ]
    out_shape, out_specs, scratch = [], [], []
    if by_stride:
        n_batch, s_len = stride_index[0], stride_index[1]
        for _ in range(3):
            for _, dil in DILATED_GROUPS:
                out_shape.append(jax.ShapeDtypeStruct((n_batch, dil, s_len // dil, D_A), BF16))
                out_specs.append(pl.BlockSpec((1, dil, tm // dil, D_A), stride_index[2]))
        scratch.append(pltpu.VMEM((D_QKV // LANES, tm, LANES), F32))
    else:
        out_shape.append(jax.ShapeDtypeStruct((n, D_QKV), F32))
        out_specs.append(pl.BlockSpec((tm, D_QKV), row))
    for g in range(N_DIL):
        nb, keep = tail_shapes[g]
        rows = min(tm, keep)
        out_shape.append(jax.ShapeDtypeStruct((nb, keep, 2 * D_A), F32))
        out_specs.append(pl.BlockSpec((1, rows, 2 * D_A), functools.partial(tail_index, g=g, rows=rows)))
    out_shape += [jax.ShapeDtypeStruct((n, D_B), BF16),
                  jax.ShapeDtypeStruct((n, D_MODEL), BF16),
                  jax.ShapeDtypeStruct((n, D_MODEL), BF16)]
    out_specs += [pl.BlockSpec((tm, D_B), row), pl.BlockSpec((tm, D_MODEL), row), pl.BlockSpec((tm, D_MODEL), row)]
    if emit_vn:
        out_shape.append(jax.ShapeDtypeStruct((n, D_B), F32))
        out_specs.append(pl.BlockSpec((tm, D_B), row))
    body = functools.partial(_stage1_body, period=period, by_stride=by_stride, emit_vn=emit_vn)
    return pl.pallas_call(
        body, out_shape=out_shape, grid=grid, in_specs=in_specs, out_specs=out_specs, scratch_shapes=scratch,
        compiler_params=_params(1), name=name,
    )(x2d, sh, sc, g1, w_in_bf, lng, lnb, ws, bs)


def _attn_prompt_body(q_ref, kp_ref, kc_ref, vp_ref, vc_ref, bias_ref, o_ref, lse_ref):
    n = pl.program_id(2)
    n_sub = q_ref.shape[2] // BLOCK
    ri = lax.broadcasted_iota(jnp.int32, (BLOCK, 2 * BLOCK), 0)
    ci = lax.broadcasted_iota(jnp.int32, (BLOCK, 2 * BLOCK), 1)
    delta = ri + BLOCK - ci
    band = (delta >= 0) & (delta <= STEPS)
    first_key = jnp.where(n > 0, 0, BLOCK)
    low_q = lax.broadcasted_iota(jnp.int32, (BLOCK, LANES), 1) < HEAD_DIM
    low_v = lax.broadcasted_iota(jnp.int32, (2 * BLOCK, LANES), 1) < HEAD_DIM
    ones_lo = jnp.where(low_v, 1.0, 0.0).astype(BF16)
    ones_hi = jnp.where(low_v, 0.0, 1.0).astype(BF16)
    for sub in range(n_sub):
        rows = slice(sub * BLOCK, (sub + 1) * BLOCK)
        valid = band & (ci >= first_key) if sub == 0 else band
        for pair in range(HEADS // 2):
            cs = slice(pair * LANES, (pair + 1) * LANES)
            qp = q_ref[0, 0, rows, cs]
            if sub == 0:
                kp = jnp.concatenate([kp_ref[0, 0, :, cs], kc_ref[0, 0, 0:BLOCK, cs]], axis=0)
                vp = jnp.concatenate([vp_ref[0, 0, :, cs], vc_ref[0, 0, 0:BLOCK, cs]], axis=0)
            else:
                kp = kc_ref[0, 0, (sub - 1) * BLOCK:(sub + 1) * BLOCK, cs]
                vp = vc_ref[0, 0, (sub - 1) * BLOCK:(sub + 1) * BLOCK, cs]
            acc, tops = None, []
            for half in range(2):
                own_q = low_q if half == 0 else jnp.logical_not(low_q)
                own_v = low_v if half == 0 else jnp.logical_not(low_v)
                s = _dot_nt(jnp.where(own_q, qp, jnp.zeros_like(qp)), kp) + bias_ref[2 * pair + half]
                s = jnp.where(valid, s, NEG_INF)
                m = jnp.max(s, axis=-1, keepdims=True)
                p = jnp.exp(s - m).astype(BF16)
                w = jnp.concatenate([jnp.where(own_v, vp, jnp.zeros_like(vp)), ones_lo if half == 0 else ones_hi],
                                    axis=1)
                r = _dot(p, w)
                acc = r if acc is None else acc + r
                tops.append(m)
            den = acc[:, LANES:]
            o_ref[0, 0, rows, cs] = (acc[:, :LANES] / den).astype(BF16)
            lse_ref[0, 0, rows, cs] = jnp.where(low_q, tops[0], tops[1]) + jnp.log(den)


def _attn_prompt_group(q4, k4, v4, bias, dil, n_sub):
    b_sz, _, m_len, _ = q4.shape
    span = n_sub * BLOCK
    cur = pl.BlockSpec((1, 1, span, D_A), lambda b, r, n: (b, r, n, 0))
    prev = pl.BlockSpec((1, 1, BLOCK, D_A), lambda b, r, n: (b, r, jnp.maximum(n * n_sub - 1, 0), 0))
    return pl.pallas_call(
        _attn_prompt_body,
        out_shape=[jax.ShapeDtypeStruct((b_sz, dil, m_len, D_A), BF16),
                   jax.ShapeDtypeStruct((b_sz, dil, m_len, D_A), F32)],
        grid=(b_sz, dil, m_len // span),
        in_specs=[cur, prev, cur, prev, cur,
                  pl.BlockSpec((HEADS, BLOCK, 2 * BLOCK), lambda b, r, n: (0, 0, 0))],
        out_specs=[cur, cur],
        compiler_params=_params(3),
        name=f"attn_prompt_d{dil}",
    )(q4, k4, k4, v4, v4, bias)


def _sample_cache_body(*refs, t_new):
    refs = list(refs)
    q_ref = refs.pop(0)
    new_refs = [refs.pop(0) for _ in range(N_DIL)]
    cache_refs = [refs.pop(0) for _ in range(N_DIL)]
    bias_refs = [refs.pop(0) for _ in range(N_DIL)]
    mask_refs = [refs.pop(0) for _ in range(N_DIL)]
    out_refs = [refs.pop(0) for _ in range(N_DIL)]
    o_ref = refs.pop(0)
    p_refs = [refs.pop(0) for _ in range(N_DIL)]
    scale_ref = refs.pop(0)
    kv = pl.program_id(1)
    rows = HEADS * t_new

    lane = lax.broadcasted_iota(jnp.int32, (D_A, LANES), 1)
    front = jnp.zeros((LANES - t_new, D_A), F32)
    for g in range(N_DIL):
        win = cache_refs[g].shape[-1]
        new = new_refs[g][0]
        half = jnp.where(kv == 0, new[:, 0:D_A], new[:, D_A:2 * D_A])
        tail = jnp.transpose(jnp.concatenate([front, half], axis=0))
        rolled = pltpu.roll(cache_refs[g][0, 0], win - t_new, axis=1)
        if win > LANES:
            out_refs[g][0, 0, :, 0:win - LANES] = rolled[:, 0:win - LANES]
        out_refs[g][0, 0, :, win - LANES:win] = jnp.where(lane >= LANES - t_new, tail, rolled[:, win - LANES:win])

    ri = lax.broadcasted_iota(jnp.int32, (rows, D_A), 0)
    ci = lax.broadcasted_iota(jnp.int32, (rows, D_A), 1)
    own_head = jnp.right_shift(ri, int(math.log2(t_new))) == jnp.right_shift(ci, int(math.log2(HEAD_DIM)))
    pad = jnp.zeros((LANES - t_new, D_A), F32)
    glane = lax.broadcasted_iota(jnp.int32, (rows, LANES), 1)

    @pl.when(kv == 0)
    def _():
        q = q_ref[...]
        stats = []
        for g in range(N_DIL):
            win = cache_refs[g].shape[-1]
            qg = q[:, g * D_A:(g + 1) * D_A]
            qblk = jnp.where(own_head, jnp.concatenate([qg] * HEADS, axis=0), 0.0).astype(BF16)
            s_c = _dot(qblk, cache_refs[g][0, 0].astype(BF16)) + bias_refs[g][:, 0:win]
            s_c = jnp.where(mask_refs[g][:, 0:win] > 0.5, s_c, NEG_INF)
            k_new = jnp.concatenate([new_refs[g][0][:, 0:D_A], pad], axis=0).astype(BF16)
            s_n = _dot_nt(qblk, k_new) + bias_refs[g][:, win:]
            s_n = jnp.where(mask_refs[g][:, win:] > 0.5, s_n, NEG_INF)
            m = jnp.maximum(jnp.max(s_c, axis=-1, keepdims=True), jnp.max(s_n, axis=-1, keepdims=True))
            p_c = jnp.exp(s_c - m)
            p_n = jnp.exp(s_n - m)
            l = jnp.sum(p_c, axis=-1, keepdims=True) + jnp.sum(p_n, axis=-1, keepdims=True)
            p_refs[g][:, 0:win] = p_c.astype(BF16)
            p_refs[g][:, win:] = p_n.astype(BF16)
            stats.append((l, m + jnp.log(l)))
        top = jnp.maximum(jnp.maximum(stats[0][1], stats[1][1]), stats[2][1])
        es = [jnp.exp(lse - top) for _, lse in stats]
        den = es[0] + es[1] + es[2]
        scale = jnp.zeros((rows, LANES), F32)
        for g in range(N_DIL):
            scale = jnp.where(glane == g, es[g] / den / stats[g][0], scale)
        scale_ref[...] = scale

    @pl.when(kv == 1)
    def _():
        acc = jnp.zeros((rows, D_A), F32)
        for g in range(N_DIL):
            win = cache_refs[g].shape[-1]
            part = _dot_nt(p_refs[g][:, 0:win], cache_refs[g][0, 0].astype(BF16))
            v_new = jnp.concatenate([new_refs[g][0][:, D_A:2 * D_A], pad], axis=0).astype(BF16)
            part = part + _dot(p_refs[g][:, win:], v_new)
            acc = acc + scale_ref[:, g:g + 1] * part
        acc = jnp.where(own_head, acc, 0.0)
        out = acc[0:t_new]
        for h in range(1, HEADS):
            out = out + acc[h * t_new:(h + 1) * t_new]
        o_ref[...] = out


def _sample_cache(q_s, kv_new, caches_t, biases, masks, t_new):
    n = q_s.shape[0]
    nb = n // t_new
    blk4 = lambda b, kv: (b, kv, 0, 0)
    const2 = lambda b, kv: (0, 0)
    in_specs = [pl.BlockSpec((t_new, D_QKV), lambda b, kv: (b, 0))]
    in_specs += [pl.BlockSpec((1, t_new, 2 * D_A), lambda b, kv: (b, 0, 0))] * N_DIL
    in_specs += [pl.BlockSpec((1, 1, D_A, c.shape[-1]), blk4) for c in caches_t]
    in_specs += [pl.BlockSpec(a.shape, const2) for a in biases]
    in_specs += [pl.BlockSpec(a.shape, const2) for a in masks]
    out_shape = [jax.ShapeDtypeStruct(c.shape, c.dtype) for c in caches_t]
    out_specs = [pl.BlockSpec((1, 1, D_A, c.shape[-1]), blk4) for c in caches_t]
    out_shape.append(jax.ShapeDtypeStruct((n, D_A), F32))
    out_specs.append(pl.BlockSpec((t_new, D_A), lambda b, kv: (b, 0)))
    rows = HEADS * t_new
    scratch = [pltpu.VMEM((rows, c.shape[-1] + LANES), BF16) for c in caches_t]
    scratch.append(pltpu.VMEM((rows, LANES), F32))
    return pl.pallas_call(
        functools.partial(_sample_cache_body, t_new=t_new),
        out_shape=out_shape, grid=(nb, 2), in_specs=in_specs, out_specs=out_specs, scratch_shapes=scratch,
        compiler_params=_params(2), name="sample_cache",
    )(q_s, *[a.reshape(nb, t_new, 2 * D_A) for a in kv_new], *caches_t, *biases, *masks)


def _stage2_body(*refs, merged):
    refs = list(refs)
    if merged:
        oa_ref = refs.pop(0)
    else:
        o_refs = [refs.pop(0) for _ in range(N_DIL)]
        l_refs = [refs.pop(0) for _ in range(N_DIL)]
    (ob_ref, ga_ref, gb_ref, x_ref, gt1_ref, sh2_ref, sc2_ref, wua_ref, wub_ref, wo_ref, g2_ref,
     wrh_ref, wrl_ref, br_ref, x1_ref, h2_ref, route_ref, code_ref, cnt_ref, run_ref) = refs[:20]
    step = pl.program_id(0)
    tm = x_ref.shape[0]

    @pl.when(step == 0)
    def _():
        run_ref[...] = jnp.zeros_like(run_ref)

    if merged:
        oa = oa_ref[...].astype(BF16)
    else:
        o_stage, l_stage = refs[20], refs[21]

        def by_position(ref, stage, dil):
            if dil == 1:
                return ref[0, 0].astype(F32)
            slabs = ref.shape[-1] // LANES
            for r in range(dil):
                rows = ref[0, r].astype(F32)
                for c in range(slabs):
                    stage[c, pl.ds(r, tm // dil, stride=dil), :] = rows[:, c * LANES:(c + 1) * LANES]
            return jnp.concatenate([stage[c] for c in range(slabs)], axis=1)

        ls = [by_position(l_refs[g], l_stage, dil) for g, (_, dil) in enumerate(DILATED_GROUPS)]
        top = jnp.maximum(jnp.maximum(ls[0], ls[1]), ls[2])
        es = [jnp.exp(l - top) for l in ls]
        den = es[0] + es[1] + es[2]
        oa = None
        for g, (_, dil) in enumerate(DILATED_GROUPS):
            term = (es[g] / den) * by_position(o_refs[g], o_stage, dil)
            oa = term if oa is None else oa + term
        oa = oa.astype(BF16)

    ya = _dot(oa, wua_ref[...])
    yb = _dot(ob_ref[...], wub_ref[...])
    z = (ga_ref[...].astype(F32) * ya + gb_ref[...].astype(F32) * yb).astype(BF16)
    x1 = x_ref[...] + gt1_ref[0] * _dot(z, wo_ref[...])
    x1_ref[...] = x1
    h2 = x1 * lax.rsqrt(jnp.mean(x1 * x1, axis=-1, keepdims=True) + EPS) * g2_ref[...]
    h2 = h2 * (1.0 + sc2_ref[0]) + sh2_ref[0]
    h2_ref[...] = h2

    h_hi = h2.astype(BF16)
    h_lo = (h2 - h_hi.astype(F32)).astype(BF16)
    logit = (_dot(h_hi, wrh_ref[...]) + _dot(h_hi, wrl_ref[...]) + _dot(h_lo, wrh_ref[...])) + br_ref[...]
    lane_i = lax.broadcasted_iota(jnp.int32, (tm, LANES), 1)
    lane = lane_i.astype(F32)
    big = float(LANES)
    is_g = (lane_i >= N_EXPERTS) & (lane_i < N_EXPERTS + MOE_GROUPS)
    gl = jnp.where(is_g, logit, -jnp.inf)
    gmax = jnp.max(gl, axis=-1, keepdims=True)
    g_idx = jnp.min(jnp.where(gl == gmax, lane, big), axis=-1, keepdims=True) - N_EXPERTS
    g_w = 1.0 / jnp.sum(jnp.where(is_g, jnp.exp(logit - gmax), 0.0), axis=-1, keepdims=True)
    lo = g_idx * EXPERTS_PER_GROUP
    el = jnp.where((lane >= lo) & (lane < lo + EXPERTS_PER_GROUP), logit, -jnp.inf)
    t1 = jnp.max(el, axis=-1, keepdims=True)
    i1 = jnp.min(jnp.where(el == t1, lane, big), axis=-1, keepdims=True)
    el2 = jnp.where(lane == i1, -jnp.inf, el)
    t2 = jnp.max(el2, axis=-1, keepdims=True)
    i2 = jnp.min(jnp.where(el2 == t2, lane, big), axis=-1, keepdims=True)
    d = jnp.exp(t2 - t1)
    cw1 = g_w * (1.0 / (1.0 + d))
    cw2 = g_w * (d / (1.0 + d))

    oh1 = lane == i1
    oh2 = lane == i2
    picked = jnp.where(oh1, 1, jnp.where(oh2, 2, 0))
    hit = jnp.minimum(picked, 1)
    row_i = lax.broadcasted_iota(jnp.int32, (tm, LANES), 0)
    seen = hit
    shift = 1
    while shift < tm:
        seen = seen + jnp.where(row_i >= shift, pltpu.roll(seen, shift, axis=0), 0)
        shift *= 2
    run = run_ref[0:1, :]
    code_ref[...] = jnp.where(hit > 0, picked + 4 * (seen - hit + run), 0)
    run_new = jnp.broadcast_to(run + seen[tm - 1:tm, :], run_ref.shape)
    run_ref[...] = run_new
    cnt_ref[...] = run_new

    route_ref[...] = jnp.where(lane_i == 0, cw1, jnp.where(lane_i == 1, cw2, 0.0))


def _stage2(attn_in, ob, ga, gb, x2d, gt1, sh2, sc2, wua, wub, wo, g2, wr_hi, wr_lo, br, *, tm, mod_index, stride_index,
            merged, name):
    n = x2d.shape[0]
    row = lambda i: (i, 0)
    const2 = lambda i: (0, 0)
    mod_block = (1,) + gt1.shape[1:]
    scratch = [pltpu.VMEM((8, LANES), jnp.int32)]
    if merged:
        attn_specs = [pl.BlockSpec((tm, D_A), row)]
    else:
        attn_specs = [pl.BlockSpec((1, dil, tm // dil, D_A), stride_index) for _, dil in DILATED_GROUPS]
        attn_specs += [pl.BlockSpec((1, dil, tm // dil, D_A), stride_index) for _, dil in DILATED_GROUPS]
        scratch += [pltpu.VMEM((D_A // LANES, tm, LANES), F32), pltpu.VMEM((D_A // LANES, tm, LANES), F32)]
    in_specs = attn_specs + [
        pl.BlockSpec((tm, D_B), row), pl.BlockSpec((tm, D_MODEL), row), pl.BlockSpec((tm, D_MODEL), row),
        pl.BlockSpec((tm, D_MODEL), row),
        pl.BlockSpec(mod_block, mod_index), pl.BlockSpec(mod_block, mod_index), pl.BlockSpec(mod_block, mod_index),
        pl.BlockSpec((D_A, D_MODEL), const2), pl.BlockSpec((D_B, D_MODEL), const2),
        pl.BlockSpec((D_MODEL, D_MODEL), const2), pl.BlockSpec((1, D_MODEL), const2),
        pl.BlockSpec((D_MODEL, LANES), const2), pl.BlockSpec((D_MODEL, LANES), const2),
        pl.BlockSpec((1, LANES), const2),
    ]
    out_shape = [jax.ShapeDtypeStruct((n, D_MODEL), F32), jax.ShapeDtypeStruct((n, D_MODEL), F32),
                 jax.ShapeDtypeStruct((n, LANES), F32), jax.ShapeDtypeStruct((n, LANES), jnp.int32),
                 jax.ShapeDtypeStruct((8, LANES), jnp.int32)]
    out_specs = [pl.BlockSpec((tm, D_MODEL), row), pl.BlockSpec((tm, D_MODEL), row),
                 pl.BlockSpec((tm, LANES), row), pl.BlockSpec((tm, LANES), row), pl.BlockSpec((8, LANES), const2)]
    return pl.pallas_call(
        functools.partial(_stage2_body, merged=merged),
        out_shape=out_shape, grid=(n // tm,), in_specs=in_specs, out_specs=out_specs,
        scratch_shapes=scratch,
        compiler_params=_params(1), name=name,
    )(*attn_in, ob, ga, gb, x2d, gt1, sh2, sc2, wua, wub, wo, g2, wr_hi, wr_lo, br)


def _dispatch_body(cnt_ref, off_ref, pad_ref, nv_ref, posp_ref, poss_ref, hp_ref, hs_ref, xs_ref, zero_ref, sem,
                   *, tiles_p, n_tiles):
    step = pl.program_id(0)
    tile = zero_ref.shape[0]

    @pl.when(step == 0)
    def _():
        zero_ref[...] = jnp.zeros_like(zero_ref)

        def per_expert(e, total):
            n = pad_ref[e] - cnt_ref[e]
            base = off_ref[e] + cnt_ref[e]

            def put(i, c):
                _row_copy(zero_ref, 0, xs_ref, base + i, sem).start()
                return c

            lax.fori_loop(0, n, put, 0)
            return total + n

        total = lax.fori_loop(0, N_EXPERTS, per_expert, 0)

        def drain(i, c):
            _row_copy(zero_ref, 0, xs_ref, 0, sem).wait()
            return c

        lax.fori_loop(0, total, drain, 0)

        def tile_copy(j):
            return pltpu.make_async_copy(zero_ref, xs_ref.at[pl.ds(j * tile, tile)], sem)

        def put_tile(j, c):
            tile_copy(j).start()
            return c

        def drain_tile(j, c):
            tile_copy(j).wait()
            return c

        lax.fori_loop(nv_ref[0], n_tiles, put_tile, 0)
        lax.fori_loop(nv_ref[0], n_tiles, drain_tile, 0)

    def scatter(h_ref, pos_ref):
        tm = h_ref.shape[0]

        def issue(t, c):
            _row_copy(h_ref, t, xs_ref, pos_ref[0, 0, 2 * t], sem).start()
            _row_copy(h_ref, t, xs_ref, pos_ref[0, 0, 2 * t + 1], sem).start(priority=1)
            return c

        lax.fori_loop(0, tm, issue, 0, unroll=8)

        def drain(t, c):
            _row_copy(h_ref, 0, xs_ref, 0, sem).wait()
            _row_copy(h_ref, 0, xs_ref, 0, sem).wait()
            return c

        lax.fori_loop(0, tm, drain, 0, unroll=8)

    @pl.when(step < tiles_p)
    def _():
        scatter(hp_ref, posp_ref)

    @pl.when(step == tiles_p)
    def _():
        scatter(hs_ref, poss_ref)


def _dispatch(count, offset, padded, n_valid, pos_p, pos_s, h2_p, h2_s, *, cap, tm, tile):
    n_p, n_s = h2_p.shape[0], h2_s.shape[0]
    tiles_p = n_p // tm
    last_p = lambda i, *_: (jnp.minimum(i, tiles_p - 1), 0, 0)
    in_specs = [pl.BlockSpec((1, 1, 2 * tm), last_p, memory_space=pltpu.SMEM),
                pl.BlockSpec((1, 1, 2 * n_s), lambda i, *_: (0, 0, 0), memory_space=pltpu.SMEM),
                pl.BlockSpec((tm, D_MODEL), lambda i, *_: (jnp.minimum(i, tiles_p - 1), 0)),
                pl.BlockSpec((n_s, D_MODEL), lambda i, *_: (0, 0))]
    grid_spec = pltpu.PrefetchScalarGridSpec(
        num_scalar_prefetch=4, grid=(tiles_p + 1,), in_specs=in_specs,
        out_specs=pl.BlockSpec(memory_space=pl.ANY),
        scratch_shapes=[pltpu.VMEM((tile, D_MODEL), F32), pltpu.SemaphoreType.DMA(())],
    )
    return pl.pallas_call(
        functools.partial(_dispatch_body, tiles_p=tiles_p, n_tiles=cap // tile),
        out_shape=jax.ShapeDtypeStruct((cap, D_MODEL), F32), grid_spec=grid_spec,
        compiler_params=pltpu.CompilerParams(dimension_semantics=("arbitrary",), vmem_limit_bytes=VMEM_LIMIT,
                                             has_side_effects=True),
        name="dispatch",
    )(count, offset, padded, n_valid, pos_p.reshape(tiles_p, 1, 2 * tm), pos_s.reshape(1, 1, 2 * n_s), h2_p, h2_s)


def _moe_body(te_ref, nv_ref, x_ref, wg_ref, wu_ref, wd_ref, y_ref, wg_s, wu_s, wd_s):
    i = pl.program_id(0)
    active = i < nv_ref[0]
    fresh = (i == 0) | (te_ref[i] != te_ref[jnp.maximum(i - 1, 0)])

    @pl.when(active & fresh)
    def _():
        wg_s[...] = wg_ref[0].astype(BF16)
        wu_s[...] = wu_ref[0].astype(BF16)
        wd_s[...] = wd_ref[0].astype(BF16)

    @pl.when(active)
    def _():
        x = x_ref[...].astype(BF16)
        a = _dot(x, wg_s[...])
        b = _dot(x, wu_s[...])
        mid = (a * _sigmoid(a) * b).astype(BF16)
        y_ref[...] = _dot(mid, wd_s[...])

    @pl.when(jnp.logical_not(active))
    def _():
        y_ref[...] = jnp.zeros_like(y_ref)


def _moe(tile_expert, n_valid, x_sorted, w_eg, w_eu, w_ed, tm):
    cap = x_sorted.shape[0]
    last = lambda i, nv: jnp.maximum(jnp.minimum(i, nv[0] - 1), 0)
    xmap = lambda i, te, nv: (last(i, nv), 0)
    wmap = lambda i, te, nv: (te[last(i, nv)], 0, 0)
    grid_spec = pltpu.PrefetchScalarGridSpec(
        num_scalar_prefetch=2,
        grid=(cap // tm,),
        in_specs=[pl.BlockSpec((tm, D_MODEL), xmap),
                  pl.BlockSpec((1, D_MODEL, D_EXPERT), wmap),
                  pl.BlockSpec((1, D_MODEL, D_EXPERT), wmap),
                  pl.BlockSpec((1, D_EXPERT, D_MODEL), wmap)],
        out_specs=pl.BlockSpec((tm, D_MODEL), lambda i, te, nv: (i, 0)),
        scratch_shapes=[pltpu.VMEM((D_MODEL, D_EXPERT), BF16), pltpu.VMEM((D_MODEL, D_EXPERT), BF16),
                        pltpu.VMEM((D_EXPERT, D_MODEL), BF16)],
    )
    return pl.pallas_call(
        _moe_body, out_shape=jax.ShapeDtypeStruct((cap, D_MODEL), F32), grid_spec=grid_spec,
        compiler_params=_params(1), name="moe",
    )(tile_expert, n_valid, x_sorted, w_eg, w_eu, w_ed)


def _final_body(pos_ref, x1_ref, route_ref, gt2_ref, gf_ref, ys_ref, o_ref, y1_buf, y2_buf, sem):
    tm = x1_ref.shape[0]

    def issue(t, c):
        _row_copy(ys_ref, pos_ref[0, 0, 2 * t], y1_buf, t, sem).start()
        _row_copy(ys_ref, pos_ref[0, 0, 2 * t + 1], y2_buf, t, sem).start(priority=1)
        return c

    lax.fori_loop(0, tm, issue, 0, unroll=8)

    def drain(t, c):
        _row_copy(ys_ref, 0, y1_buf, 0, sem).wait()
        _row_copy(ys_ref, 0, y2_buf, 0, sem).wait()
        return c

    lax.fori_loop(0, tm, drain, 0, unroll=8)

    route = route_ref[...]
    moe = route[:, 0:1] * y1_buf[...] + route[:, 1:2] * y2_buf[...]
    x = x1_ref[...] + gt2_ref[0] * moe
    o_ref[...] = x * lax.rsqrt(jnp.mean(x * x, axis=-1, keepdims=True) + EPS) * gf_ref[...]


def _final(pos, x1, route, gt2, gf, y_sorted, *, tm, mod_index, name):
    n = x1.shape[0]
    row = lambda i: (i, 0)
    mod_block = (1,) + gt2.shape[1:]
    return pl.pallas_call(
        _final_body,
        out_shape=jax.ShapeDtypeStruct((n, D_MODEL), F32),
        grid=(n // tm,),
        in_specs=[pl.BlockSpec((1, 1, 2 * tm), lambda i: (i, 0, 0), memory_space=pltpu.SMEM),
                  pl.BlockSpec((tm, D_MODEL), row), pl.BlockSpec((tm, LANES), row),
                  pl.BlockSpec(mod_block, mod_index), pl.BlockSpec((1, D_MODEL), lambda i: (0, 0)),
                  pl.BlockSpec(memory_space=pl.ANY)],
        out_specs=pl.BlockSpec((tm, D_MODEL), row),
        scratch_shapes=[pltpu.VMEM((tm, D_MODEL), F32), pltpu.VMEM((tm, D_MODEL), F32),
                        pltpu.SemaphoreType.DMA(())],
        compiler_params=_params(1), name=name,
    )(pos.reshape(n // tm, 1, 2 * tm), x1, route, gt2, gf, y_sorted)


def _t5_bucket(dist):
    max_exact = N_BUCKETS // 2
    dist = np.asarray(dist)
    d = np.maximum(dist, 1).astype(np.float32)
    large = max_exact + (np.log(d / np.float32(max_exact)) / np.float32(math.log(MAX_DISTANCE / max_exact))
                         * np.float32(N_BUCKETS - max_exact)).astype(np.int32)
    return np.where(dist < max_exact, dist, np.minimum(large, N_BUCKETS - 1)).astype(np.int32)


def _bias_lookup(tab, dist):
    onehot = (jnp.asarray(_t5_bucket(dist))[..., None] == jnp.arange(N_BUCKETS)).astype(F32)
    return jnp.einsum("...b,bh->...h", onehot, tab, precision=HIGHEST)


def _prompt_bias(tab, dil):
    delta = np.arange(BLOCK)[:, None] + BLOCK - np.arange(2 * BLOCK)[None, :]
    return jnp.transpose(_bias_lookup(tab, np.maximum(delta, 0) * dil), (2, 0, 1))


def _sample_tables(tab, win, dil, t_new):
    buf = np.concatenate([np.arange(win), win + np.arange(t_new), np.full(LANES - t_new, -1)])
    back = win + np.arange(t_new)[:, None] - buf[None, :]
    valid = (buf[None, :] >= 0) & (back >= 0) & (back % dil == 0) & (back // dil <= STEPS)
    bias = _bias_lookup(tab, np.maximum(back, 0))
    bias = jnp.transpose(bias, (2, 0, 1)).reshape(HEADS * t_new, buf.shape[0])
    mask = np.broadcast_to(valid[None], (HEADS, t_new, buf.shape[0])).reshape(HEADS * t_new, -1)
    return bias, jnp.asarray(mask.astype(np.float32))


def kernel(x_prompt, x_sample, c_prompt, c_sample, cache_kv_w128, cache_kv_w512, cache_kv_w2048, rel_bias, w_ada,
           b_ada, g_norm1, w_in, ln_v_g, ln_v_b, w_spatial, b_spatial, w_up_a, w_up_b, w_out, g_norm2,
           w_route_group, b_route_group, w_route_expert, b_route_expert, w_e_gate, w_e_up, w_e_down, g_final):
    assert w_ada.shape[0] == 1, "single layer"
    bp, s_len, _ = x_prompt.shape
    bs, t_new, _ = x_sample.shape
    n_p, n_s = bp * s_len, bs * t_new
    caches = (cache_kv_w128[0], cache_kv_w512[0], cache_kv_w2048[0])
    assert all(c.shape[1] == win for c, (win, _) in zip(caches, DILATED_GROUPS)), "cache holds one full window"

    mod = _ada(jnp.concatenate([c_prompt, c_sample], axis=0), w_ada[0], b_ada[0])
    mod_p = [mod[:bp, j * D_MODEL:(j + 1) * D_MODEL].reshape(bp, 1, D_MODEL) for j in range(6)]
    mod_s = [jnp.repeat(mod[bp:, j * D_MODEL:(j + 1) * D_MODEL], t_new, axis=0).reshape(1, n_s, D_MODEL)
             for j in range(6)]

    w_in_bf = w_in[0].astype(BF16)
    g1 = g_norm1[0].reshape(1, D_MODEL)
    lng, lnb = ln_v_g[0].reshape(1, D_B), ln_v_b[0].reshape(1, D_B)
    ws_p = w_spatial[0]
    bs_p = jnp.repeat(jnp.transpose(b_spatial[0]), SGU_DIM, axis=1)
    reps = n_s // t_new
    ws_s = jnp.tile(w_spatial[0][:, :t_new, :t_new], (1, reps, reps))
    bs_s = jnp.tile(bs_p[:t_new], (reps, 1))

    tpb = s_len // TM_STAGE1
    tail_p = [(bp, min(win, s_len)) for win, _ in DILATED_GROUPS]

    def tail_index_p(i, g, rows):
        first = tpb - tail_p[g][1] // rows
        return (i // tpb, jnp.maximum(i % tpb - first, 0), 0)

    outs = _stage1(
        x_prompt.reshape(n_p, D_MODEL), mod_p[0], mod_p[1], g1, w_in_bf, lng, lnb, ws_p, bs_p,
        tm=TM_STAGE1, mod_index=lambda i: (i // tpb, 0, 0),
        stride_index=(bp, s_len, lambda i: (i // tpb, 0, i % tpb, 0)), tail_shapes=tail_p, tail_index=tail_index_p,
        period=CHUNK, by_stride=True, emit_vn=False, name="stage1_prompt")
    q_g, k_g, v_g = outs[0:3], outs[3:6], outs[6:9]
    kv0_p, kv1_p, kv2_p, ob_p, ga_p, gb_p = outs[9:]
    q_s, kv0_s, kv1_s, kv2_s, ob_s, ga_s, gb_s, vn_s = _stage1(
        x_sample.reshape(n_s, D_MODEL), mod_s[0], mod_s[1], g1, w_in_bf, lng, lnb, ws_s, bs_s,
        tm=n_s, mod_index=lambda i: (0, 0, 0), stride_index=None, tail_shapes=[(1, n_s)] * N_DIL,
        tail_index=lambda i, g, rows: (0, 0, 0),
        period=t_new, by_stride=False, emit_vn=True, name="stage1_sample")

    o_groups, lse_groups = [], []
    sb, sm = [], []
    for g, (win, dil) in enumerate(DILATED_GROUPS):
        tab = rel_bias[:, g * HEADS:(g + 1) * HEADS].astype(F32)
        o, lse = _attn_prompt_group(q_g[g], k_g[g], v_g[g], _prompt_bias(tab, dil), dil, ATTN_BLOCKS_PER_STEP)
        o_groups.append(o)
        lse_groups.append(lse)
        bias, mask = _sample_tables(tab, win, dil, t_new)
        sb.append(bias)
        sm.append(mask)
    caches_t = [jnp.transpose(c, (0, 2, 3, 4, 1)).reshape(bs, 2, D_A, c.shape[1]) for c in caches]
    cache0_t, cache1_t, cache2_t, oa_s = _sample_cache(q_s, (kv0_s, kv1_s, kv2_s), caches_t, sb, sm, t_new)

    wua, wub, wo = w_up_a[0].astype(BF16), w_up_b[0].astype(BF16), w_out[0].astype(BF16)
    g2 = g_norm2[0].reshape(1, D_MODEL)
    wr = jnp.concatenate([w_route_expert[0].reshape(D_MODEL, N_EXPERTS), w_route_group[0]], axis=1)
    wr = jnp.pad(wr, ((0, 0), (0, LANES - wr.shape[1])))
    br = jnp.concatenate([b_route_expert[0].reshape(N_EXPERTS), b_route_group[0]])
    br = jnp.pad(br, (0, LANES - br.shape[0])).reshape(1, LANES)
    wr_hi = wr.astype(BF16)
    wr_lo = (wr - wr_hi.astype(F32)).astype(BF16)

    tpb2 = s_len // TM_STAGE2
    x1_p, h2_p, route_p, code_p, cnt_p = _stage2(
        o_groups + lse_groups, ob_p, ga_p, gb_p, x_prompt.reshape(n_p, D_MODEL),
        mod_p[2], mod_p[3], mod_p[4], wua, wub, wo, g2, wr_hi, wr_lo, br,
        tm=TM_STAGE2, mod_index=lambda i: (i // tpb2, 0, 0), stride_index=lambda i: (i // tpb2, 0, i % tpb2, 0),
        merged=False, name="stage2_prompt")
    x1_s, h2_s, route_s, code_s, cnt_s = _stage2(
        [oa_s], ob_s, ga_s, gb_s, x_sample.reshape(n_s, D_MODEL),
        mod_s[2], mod_s[3], mod_s[4], wua, wub, wo, g2, wr_hi, wr_lo, br,
        tm=n_s, mod_index=lambda i: (0, 0, 0), stride_index=None, merged=True, name="stage2_sample")

    cap = 2 * (n_p + n_s) + N_EXPERTS * TM_MOE
    n_tiles = cap // TM_MOE
    count_p = cnt_p[0, :N_EXPERTS]
    count = count_p + cnt_s[0, :N_EXPERTS]
    padded = ((count + TM_MOE - 1) // TM_MOE) * TM_MOE
    ends = jnp.cumsum(padded)
    offset = ends - padded
    n_valid = (ends[-1] // TM_MOE).astype(jnp.int32).reshape(1)
    tile_start = jnp.arange(n_tiles, dtype=jnp.int32) * TM_MOE
    tile_expert = jnp.minimum(jnp.sum(ends[None, :] <= tile_start[:, None], axis=1), N_EXPERTS - 1).astype(jnp.int32)

    def sorted_rows(code, earlier):
        row = jnp.pad(offset + earlier, (0, LANES - N_EXPERTS))[None, :] + jnp.right_shift(code, 2)
        return jnp.stack([jnp.sum(jnp.where((code & 3) == k, row, 0), axis=1) for k in (1, 2)], axis=1)

    pos_p = sorted_rows(code_p, jnp.zeros_like(count_p))
    pos_s = sorted_rows(code_s, count_p)
    x_sorted = _dispatch(count, offset, padded, n_valid, pos_p, pos_s, h2_p, h2_s, cap=cap, tm=TM_FINAL, tile=TM_MOE)

    y_sorted = _moe(tile_expert, n_valid, x_sorted, w_e_gate[0], w_e_up[0], w_e_down[0], TM_MOE)

    gf = g_final.reshape(1, D_MODEL)
    tpb3 = s_len // TM_FINAL
    y_p = _final(pos_p, x1_p, route_p, mod_p[5], gf, y_sorted,
                 tm=TM_FINAL, mod_index=lambda i: (i // tpb3, 0, 0), name="final_prompt")
    y_s = _final(pos_s, x1_s, route_s, mod_s[5], gf, y_sorted,
                 tm=n_s, mod_index=lambda i: (0, 0, 0), name="final_sample")

    kv_p = [a.reshape(1, bp, a.shape[1], 2, HEADS, HEAD_DIM) for a in (kv0_p, kv1_p, kv2_p)]
    kv_s = [jnp.transpose(c.reshape(bs, 2, HEADS, HEAD_DIM, c.shape[-1]), (0, 4, 1, 2, 3))[None]
            for c in (cache0_t, cache1_t, cache2_t)]
    return (y_p.reshape(bp, s_len, D_MODEL), y_s.reshape(bs, t_new, D_MODEL),
            kv_p[0], kv_p[1], kv_p[2], kv_s[0], kv_s[1], kv_s[2],
            vn_s.reshape(1, bs, t_new, D_B))
```

```python
import functools
import math

import numpy as np
import jax
import jax.numpy as jnp
from jax import lax
from jax.experimental import pallas as pl
from jax.experimental.pallas import tpu as pltpu

F32 = jnp.float32
BF16 = jnp.bfloat16
HIGHEST = lax.Precision.HIGHEST

D_MODEL = 1024
HEAD_DIM = 64
HEADS = 8
DILATED_GROUPS = ((128, 1), (512, 4), (2048, 16))
N_DIL = 3
D_A = HEADS * HEAD_DIM
D_QKV = N_DIL * D_A
BLOCK = 128
STEPS = 128
SCALE = HEAD_DIM ** -0.5
CHUNK = 128
D_B = 512
SGU_GROUPS = 4
SGU_DIM = D_B // SGU_GROUPS
N_BUCKETS = 32
MAX_DISTANCE = 2048
MOE_GROUPS = 4
EXPERTS_PER_GROUP = 8
N_EXPERTS = 32
D_EXPERT = 512
EPS = 1e-6
NEG_INF = -1e30
LANES = 128
C_U = 3 * D_QKV
C_V = C_U + D_B
C_GA = C_V + D_B
C_GB = C_GA + D_MODEL
IN_COLS = C_GB + D_MODEL

TM_STAGE1 = 256
TM_STAGE2 = 512
TM_MOE = 512
ATTN_BLOCKS_PER_STEP = 4
TM_FINAL = 512
VMEM_LIMIT = 56 * 1024 * 1024


def _dot(a, b):
    return jnp.dot(a, b, preferred_element_type=F32)


def _dot_nt(a, b):
    return lax.dot_general(a, b, (((1,), (1,)), ((), ())), preferred_element_type=F32)


def _sigmoid(x):
    return 1.0 / (1.0 + jnp.exp(-x))


def _gelu(x):
    return 0.5 * x * (1.0 + lax.erf(x * (2.0 ** -0.5)))


def _params(n_grid):
    return pltpu.CompilerParams(dimension_semantics=("arbitrary",) * n_grid, vmem_limit_bytes=VMEM_LIMIT)


def _row_copy(src, src_row, dst, dst_row, sem):
    return pltpu.make_async_copy(src.at[pl.ds(src_row, 1)], dst.at[pl.ds(dst_row, 1)], sem)


def _ada_body(c_ref, w_ref, b_ref, o_ref):
    c = c_ref[...]
    s = c * _sigmoid(c)
    o_ref[...] = jnp.dot(s, w_ref[...], precision=HIGHEST, preferred_element_type=F32) + b_ref[...]


def _ada(c_all, w_ada, b_ada):
    n, d = c_all.shape
    cols = w_ada.shape[1]
    tn = 1024
    return pl.pallas_call(
        _ada_body,
        out_shape=jax.ShapeDtypeStruct((n, cols), F32),
        grid=(cols // tn,),
        in_specs=[pl.BlockSpec((n, d), lambda j: (0, 0)),
                  pl.BlockSpec((d, tn), lambda j: (0, j)),
                  pl.BlockSpec((1, tn), lambda j: (0, j))],
        out_specs=pl.BlockSpec((n, tn), lambda j: (0, j)),
        compiler_params=_params(1),
        name="ada",
    )(c_all, w_ada, b_ada.reshape(1, cols))


def _stage1_body(x_ref, sh_ref, sc_ref, g1_ref, w_ref, lng_ref, lnb_ref, ws_ref, bs_ref, *rest,
                 period, by_stride, emit_vn):
    rest = list(rest)
    if by_stride:
        qkv_refs = [[rest.pop(0) for _ in range(N_DIL)] for _ in range(3)]
    else:
        q_ref = rest.pop(0)
    kv_refs = [rest.pop(0) for _ in range(N_DIL)]
    ob_ref, ga_ref, gb_ref = rest.pop(0), rest.pop(0), rest.pop(0)
    vn_ref = rest.pop(0) if emit_vn else None
    stage_ref = rest.pop(0) if by_stride else None

    x = x_ref[...]
    tm = x.shape[0]
    h = x * lax.rsqrt(jnp.mean(x * x, axis=-1, keepdims=True) + EPS) * g1_ref[...]
    h = h * (1.0 + sc_ref[0]) + sh_ref[0]
    hb = h.astype(BF16)

    q = _dot(hb, w_ref[:, 0:D_QKV]) * SCALE
    k = _dot(hb, w_ref[:, D_QKV:2 * D_QKV])
    v = _dot(hb, w_ref[:, 2 * D_QKV:3 * D_QKV])
    if by_stride:
        per_group = D_A // LANES
        for val, refs in zip((q, k, v), qkv_refs):
            for c in range(D_QKV // LANES):
                stage_ref[c] = val[:, c * LANES:(c + 1) * LANES]
            for g, (_, dil) in enumerate(DILATED_GROUPS):
                for r in range(dil):
                    for c in range(per_group):
                        piece = stage_ref[g * per_group + c, pl.ds(r, tm // dil, stride=dil), :]
                        refs[g][0, r, :, c * LANES:(c + 1) * LANES] = piece.astype(BF16)
    else:
        q_ref[...] = q
    for g in range(N_DIL):
        rows = kv_refs[g].shape[1]
        kv_refs[g][0, :, 0:D_A] = k[tm - rows:, g * D_A:(g + 1) * D_A]
        kv_refs[g][0, :, D_A:2 * D_A] = v[tm - rows:, g * D_A:(g + 1) * D_A]

    ga_ref[...] = _sigmoid(_dot(hb, w_ref[:, C_GA:C_GB])).astype(BF16)
    gb_ref[...] = _sigmoid(_dot(hb, w_ref[:, C_GB:IN_COLS])).astype(BF16)

    u = _gelu(_dot(hb, w_ref[:, C_U:C_V]))
    vb = _gelu(_dot(hb, w_ref[:, C_V:C_GA]))
    xc = vb - jnp.mean(vb, axis=-1, keepdims=True)
    vn = xc * lax.rsqrt(jnp.mean(xc * xc, axis=-1, keepdims=True) + EPS) * lng_ref[...] + lnb_ref[...]
    if emit_vn:
        vn_ref[...] = vn
    vnb = vn.astype(BF16)

    r = ws_ref.shape[1]
    ri = lax.broadcasted_iota(jnp.int32, (r, r), 0)
    ci = lax.broadcasted_iota(jnp.int32, (r, r), 1)
    keep = ci <= ri
    if period != r:
        sh = int(math.log2(period))
        keep = keep & (jnp.right_shift(ri, sh) == jnp.right_shift(ci, sh))
    for g in range(SGU_GROUPS):
        wg = jnp.where(keep, ws_ref[g], 0.0).astype(BF16)
        cs = slice(g * SGU_DIM, (g + 1) * SGU_DIM)
        for c in range(tm // r):
            rs = slice(c * r, (c + 1) * r)
            mixed = _dot(wg, vnb[rs, cs]) + bs_ref[:, cs]
            ob_ref[rs, cs] = (u[rs, cs] * mixed).astype(BF16)


def _stage1(x2d, sh, sc, g1, w_in_bf, lng, lnb, ws, bs, *, tm, mod_index, stride_index, tail_shapes, tail_index,
            period, by_stride, emit_vn, name):
    n = x2d.shape[0]
    grid = (n // tm,)
    mod_block = (1,) + sh.shape[1:]
    r = ws.shape[1]
    const2 = lambda i: (0, 0)
    row = lambda i: (i, 0)
    in_specs = [
        pl.BlockSpec((tm, D_MODEL), row),
        pl.BlockSpec(mod_block, mod_index),
        pl.BlockSpec(mod_block, mod_index),
        pl.BlockSpec((1, D_MODEL), const2),
        pl.BlockSpec((D_MODEL, IN_COLS), const2, pipeline_mode=pl.Buffered(1)),
        pl.BlockSpec((1, D_B), const2),
        pl.BlockSpec((1, D_B), const2),
        pl.BlockSpec((SGU_GROUPS, r, r), lambda i: (0, 0, 0)),
        pl.BlockSpec((r, D_B), const2),
    ]
    out_shape, out_specs, scratch = [], [], []
    if by_stride:
        n_batch, s_len = stride_index[0], stride_index[1]
        for _ in range(3):
            for _, dil in DILATED_GROUPS:
                out_shape.append(jax.ShapeDtypeStruct((n_batch, dil, s_len // dil, D_A), BF16))
                out_specs.append(pl.BlockSpec((1, dil, tm // dil, D_A), stride_index[2]))
        scratch.append(pltpu.VMEM((D_QKV // LANES, tm, LANES), F32))
    else:
        out_shape.append(jax.ShapeDtypeStruct((n, D_QKV), F32))
        out_specs.append(pl.BlockSpec((tm, D_QKV), row))
    for g in range(N_DIL):
        nb, keep = tail_shapes[g]
        rows = min(tm, keep)
        out_shape.append(jax.ShapeDtypeStruct((nb, keep, 2 * D_A), F32))
        out_specs.append(pl.BlockSpec((1, rows, 2 * D_A), functools.partial(tail_index, g=g, rows=rows)))
    out_shape += [jax.ShapeDtypeStruct((n, D_B), BF16),
                  jax.ShapeDtypeStruct((n, D_MODEL), BF16),
                  jax.ShapeDtypeStruct((n, D_MODEL), BF16)]
    out_specs += [pl.BlockSpec((tm, D_B), row), pl.BlockSpec((tm, D_MODEL), row), pl.BlockSpec((tm, D_MODEL), row)]
    if emit_vn:
        out_shape.append(jax.ShapeDtypeStruct((n, D_B), F32))
        out_specs.append(pl.BlockSpec((tm, D_B), row))
    body = functools.partial(_stage1_body, period=period, by_stride=by_stride, emit_vn=emit_vn)
    return pl.pallas_call(
        body, out_shape=out_shape, grid=grid, in_specs=in_specs, out_specs=out_specs, scratch_shapes=scratch,
        compiler_params=_params(1), name=name,
    )(x2d, sh, sc, g1, w_in_bf, lng, lnb, ws, bs)


def _attn_prompt_body(q_ref, kp_ref, kc_ref, vp_ref, vc_ref, bias_ref, o_ref, lse_ref):
    n = pl.program_id(2)
    n_sub = q_ref.shape[2] // BLOCK
    ri = lax.broadcasted_iota(jnp.int32, (BLOCK, 2 * BLOCK), 0)
    ci = lax.broadcasted_iota(jnp.int32, (BLOCK, 2 * BLOCK), 1)
    delta = ri + BLOCK - ci
    band = (delta >= 0) & (delta <= STEPS)
    first_key = jnp.where(n > 0, 0, BLOCK)
    low_q = lax.broadcasted_iota(jnp.int32, (BLOCK, LANES), 1) < HEAD_DIM
    low_v = lax.broadcasted_iota(jnp.int32, (2 * BLOCK, LANES), 1) < HEAD_DIM
    ones_lo = jnp.where(low_v, 1.0, 0.0).astype(BF16)
    ones_hi = jnp.where(low_v, 0.0, 1.0).astype(BF16)
    for sub in range(n_sub):
        rows = slice(sub * BLOCK, (sub + 1) * BLOCK)
        valid = band & (ci >= first_key) if sub == 0 else band
        for pair in range(HEADS // 2):
            cs = slice(pair * LANES, (pair + 1) * LANES)
            qp = q_ref[0, 0, rows, cs]
            if sub == 0:
                kp = jnp.concatenate([kp_ref[0, 0, :, cs], kc_ref[0, 0, 0:BLOCK, cs]], axis=0)
                vp = jnp.concatenate([vp_ref[0, 0, :, cs], vc_ref[0, 0, 0:BLOCK, cs]], axis=0)
            else:
                kp = kc_ref[0, 0, (sub - 1) * BLOCK:(sub + 1) * BLOCK, cs]
                vp = vc_ref[0, 0, (sub - 1) * BLOCK:(sub + 1) * BLOCK, cs]
            acc, tops = None, []
            for half in range(2):
                own_q = low_q if half == 0 else jnp.logical_not(low_q)
                own_v = low_v if half == 0 else jnp.logical_not(low_v)
                s = _dot_nt(jnp.where(own_q, qp, jnp.zeros_like(qp)), kp) + bias_ref[2 * pair + half]
                s = jnp.where(valid, s, NEG_INF)
                m = jnp.max(s, axis=-1, keepdims=True)
                p = jnp.exp(s - m).astype(BF16)
                w = jnp.concatenate([jnp.where(own_v, vp, jnp.zeros_like(vp)), ones_lo if half == 0 else ones_hi],
                                    axis=1)
                r = _dot(p, w)
                acc = r if acc is None else acc + r
                tops.append(m)
            den = acc[:, LANES:]
            o_ref[0, 0, rows, cs] = (acc[:, :LANES] / den).astype(BF16)
            lse_ref[0, 0, rows, cs] = jnp.where(low_q, tops[0], tops[1]) + jnp.log(den)


def _attn_prompt_group(q4, k4, v4, bias, dil, n_sub):
    b_sz, _, m_len, _ = q4.shape
    span = n_sub * BLOCK
    cur = pl.BlockSpec((1, 1, span, D_A), lambda b, r, n: (b, r, n, 0))
    prev = pl.BlockSpec((1, 1, BLOCK, D_A), lambda b, r, n: (b, r, jnp.maximum(n * n_sub - 1, 0), 0))
    return pl.pallas_call(
        _attn_prompt_body,
        out_shape=[jax.ShapeDtypeStruct((b_sz, dil, m_len, D_A), BF16),
                   jax.ShapeDtypeStruct((b_sz, dil, m_len, D_A), F32)],
        grid=(b_sz, dil, m_len // span),
        in_specs=[cur, prev, cur, prev, cur,
                  pl.BlockSpec((HEADS, BLOCK, 2 * BLOCK), lambda b, r, n: (0, 0, 0))],
        out_specs=[cur, cur],
        compiler_params=_params(3),
        name=f"attn_prompt_d{dil}",
    )(q4, k4, k4, v4, v4, bias)


def _sample_cache_body(*refs, t_new):
    refs = list(refs)
    q_ref = refs.pop(0)
    new_refs = [refs.pop(0) for _ in range(N_DIL)]
    cache_refs = [refs.pop(0) for _ in range(N_DIL)]
    bias_refs = [refs.pop(0) for _ in range(N_DIL)]
    mask_refs = [refs.pop(0) for _ in range(N_DIL)]
    out_refs = [refs.pop(0) for _ in range(N_DIL)]
    o_ref = refs.pop(0)
    p_refs = [refs.pop(0) for _ in range(N_DIL)]
    scale_ref = refs.pop(0)
    kv = pl.program_id(1)
    rows = HEADS * t_new

    lane = lax.broadcasted_iota(jnp.int32, (D_A, LANES), 1)
    front = jnp.zeros((LANES - t_new, D_A), F32)
    for g in range(N_DIL):
        win = cache_refs[g].shape[-1]
        new = new_refs[g][0]
        half = jnp.where(kv == 0, new[:, 0:D_A], new[:, D_A:2 * D_A])
        tail = jnp.transpose(jnp.concatenate([front, half], axis=0))
        rolled = pltpu.roll(cache_refs[g][0, 0], win - t_new, axis=1)
        if win > LANES:
            out_refs[g][0, 0, :, 0:win - LANES] = rolled[:, 0:win - LANES]
        out_refs[g][0, 0, :, win - LANES:win] = jnp.where(lane >= LANES - t_new, tail, rolled[:, win - LANES:win])

    ri = lax.broadcasted_iota(jnp.int32, (rows, D_A), 0)
    ci = lax.broadcasted_iota(jnp.int32, (rows, D_A), 1)
    own_head = jnp.right_shift(ri, int(math.log2(t_new))) == jnp.right_shift(ci, int(math.log2(HEAD_DIM)))
    pad = jnp.zeros((LANES - t_new, D_A), F32)
    glane = lax.broadcasted_iota(jnp.int32, (rows, LANES), 1)

    @pl.when(kv == 0)
    def _():
        q = q_ref[...]
        stats = []
        for g in range(N_DIL):
            win = cache_refs[g].shape[-1]
            qg = q[:, g * D_A:(g + 1) * D_A]
            qblk = jnp.where(own_head, jnp.concatenate([qg] * HEADS, axis=0), 0.0).astype(BF16)
            s_c = _dot(qblk, cache_refs[g][0, 0].astype(BF16)) + bias_refs[g][:, 0:win]
            s_c = jnp.where(mask_refs[g][:, 0:win] > 0.5, s_c, NEG_INF)
            k_new = jnp.concatenate([new_refs[g][0][:, 0:D_A], pad], axis=0).astype(BF16)
            s_n = _dot_nt(qblk, k_new) + bias_refs[g][:, win:]
            s_n = jnp.where(mask_refs[g][:, win:] > 0.5, s_n, NEG_INF)
            m = jnp.maximum(jnp.max(s_c, axis=-1, keepdims=True), jnp.max(s_n, axis=-1, keepdims=True))
            p_c = jnp.exp(s_c - m)
            p_n = jnp.exp(s_n - m)
            l = jnp.sum(p_c, axis=-1, keepdims=True) + jnp.sum(p_n, axis=-1, keepdims=True)
            p_refs[g][:, 0:win] = p_c.astype(BF16)
            p_refs[g][:, win:] = p_n.astype(BF16)
            stats.append((l, m + jnp.log(l)))
        top = jnp.maximum(jnp.maximum(stats[0][1], stats[1][1]), stats[2][1])
        es = [jnp.exp(lse - top) for _, lse in stats]
        den = es[0] + es[1] + es[2]
        scale = jnp.zeros((rows, LANES), F32)
        for g in range(N_DIL):
            scale = jnp.where(glane == g, es[g] / den / stats[g][0], scale)
        scale_ref[...] = scale

    @pl.when(kv == 1)
    def _():
        acc = jnp.zeros((rows, D_A), F32)
        for g in range(N_DIL):
            win = cache_refs[g].shape[-1]
            part = _dot_nt(p_refs[g][:, 0:win], cache_refs[g][0, 0].astype(BF16))
            v_new = jnp.concatenate([new_refs[g][0][:, D_A:2 * D_A], pad], axis=0).astype(BF16)
            part = part + _dot(p_refs[g][:, win:], v_new)
            acc = acc + scale_ref[:, g:g + 1] * part
        acc = jnp.where(own_head, acc, 0.0)
        out = acc[0:t_new]
        for h in range(1, HEADS):
            out = out + acc[h * t_new:(h + 1) * t_new]
        o_ref[...] = out


def _sample_cache(q_s, kv_new, caches_t, biases, masks, t_new):
    n = q_s.shape[0]
    nb = n // t_new
    blk4 = lambda b, kv: (b, kv, 0, 0)
    const2 = lambda b, kv: (0, 0)
    in_specs = [pl.BlockSpec((t_new, D_QKV), lambda b, kv: (b, 0))]
    in_specs += [pl.BlockSpec((1, t_new, 2 * D_A), lambda b, kv: (b, 0, 0))] * N_DIL
    in_specs += [pl.BlockSpec((1, 1, D_A, c.shape[-1]), blk4) for c in caches_t]
    in_specs += [pl.BlockSpec(a.shape, const2) for a in biases]
    in_specs += [pl.BlockSpec(a.shape, const2) for a in masks]
    out_shape = [jax.ShapeDtypeStruct(c.shape, c.dtype) for c in caches_t]
    out_specs = [pl.BlockSpec((1, 1, D_A, c.shape[-1]), blk4) for c in caches_t]
    out_shape.append(jax.ShapeDtypeStruct((n, D_A), F32))
    out_specs.append(pl.BlockSpec((t_new, D_A), lambda b, kv: (b, 0)))
    rows = HEADS * t_new
    scratch = [pltpu.VMEM((rows, c.shape[-1] + LANES), BF16) for c in caches_t]
    scratch.append(pltpu.VMEM((rows, LANES), F32))
    return pl.pallas_call(
        functools.partial(_sample_cache_body, t_new=t_new),
        out_shape=out_shape, grid=(nb, 2), in_specs=in_specs, out_specs=out_specs, scratch_shapes=scratch,
        compiler_params=_params(2), name="sample_cache",
    )(q_s, *[a.reshape(nb, t_new, 2 * D_A) for a in kv_new], *caches_t, *biases, *masks)


def _stage2_body(*refs, merged):
    refs = list(refs)
    if merged:
        oa_ref = refs.pop(0)
    else:
        o_refs = [refs.pop(0) for _ in range(N_DIL)]
        l_refs = [refs.pop(0) for _ in range(N_DIL)]
    (ob_ref, ga_ref, gb_ref, x_ref, gt1_ref, sh2_ref, sc2_ref, wua_ref, wub_ref, wo_ref, g2_ref,
     wrh_ref, wrl_ref, br_ref, x1_ref, h2_ref, route_ref, code_ref, cnt_ref, run_ref) = refs[:20]
    step = pl.program_id(0)
    tm = x_ref.shape[0]

    @pl.when(step == 0)
    def _():
        run_ref[...] = jnp.zeros_like(run_ref)

    if merged:
        oa = oa_ref[...].astype(BF16)
    else:
        o_stage, l_stage = refs[20], refs[21]

        def by_position(ref, stage, dil):
            if dil == 1:
                return ref[0, 0].astype(F32)
            slabs = ref.shape[-1] // LANES
            for r in range(dil):
                rows = ref[0, r].astype(F32)
                for c in range(slabs):
                    stage[c, pl.ds(r, tm // dil, stride=dil), :] = rows[:, c * LANES:(c + 1) * LANES]
            return jnp.concatenate([stage[c] for c in range(slabs)], axis=1)

        ls = [by_position(l_refs[g], l_stage, dil) for g, (_, dil) in enumerate(DILATED_GROUPS)]
        top = jnp.maximum(jnp.maximum(ls[0], ls[1]), ls[2])
        es = [jnp.exp(l - top) for l in ls]
        den = es[0] + es[1] + es[2]
        oa = None
        for g, (_, dil) in enumerate(DILATED_GROUPS):
            term = (es[g] / den) * by_position(o_refs[g], o_stage, dil)
            oa = term if oa is None else oa + term
        oa = oa.astype(BF16)

    ya = _dot(oa, wua_ref[...])
    yb = _dot(ob_ref[...], wub_ref[...])
    z = (ga_ref[...].astype(F32) * ya + gb_ref[...].astype(F32) * yb).astype(BF16)
    x1 = x_ref[...] + gt1_ref[0] * _dot(z, wo_ref[...])
    x1_ref[...] = x1
    h2 = x1 * lax.rsqrt(jnp.mean(x1 * x1, axis=-1, keepdims=True) + EPS) * g2_ref[...]
    h2 = h2 * (1.0 + sc2_ref[0]) + sh2_ref[0]
    h2_ref[...] = h2

    h_hi = h2.astype(BF16)
    h_lo = (h2 - h_hi.astype(F32)).astype(BF16)
    logit = (_dot(h_hi, wrh_ref[...]) + _dot(h_hi, wrl_ref[...]) + _dot(h_lo, wrh_ref[...])) + br_ref[...]
    lane_i = lax.broadcasted_iota(jnp.int32, (tm, LANES), 1)
    lane = lane_i.astype(F32)
    big = float(LANES)
    is_g = (lane_i >= N_EXPERTS) & (lane_i < N_EXPERTS + MOE_GROUPS)
    gl = jnp.where(is_g, logit, -jnp.inf)
    gmax = jnp.max(gl, axis=-1, keepdims=True)
    g_idx = jnp.min(jnp.where(gl == gmax, lane, big), axis=-1, keepdims=True) - N_EXPERTS
    g_w = 1.0 / jnp.sum(jnp.where(is_g, jnp.exp(logit - gmax), 0.0), axis=-1, keepdims=True)
    lo = g_idx * EXPERTS_PER_GROUP
    el = jnp.where((lane >= lo) & (lane < lo + EXPERTS_PER_GROUP), logit, -jnp.inf)
    t1 = jnp.max(el, axis=-1, keepdims=True)
    i1 = jnp.min(jnp.where(el == t1, lane, big), axis=-1, keepdims=True)
    el2 = jnp.where(lane == i1, -jnp.inf, el)
    t2 = jnp.max(el2, axis=-1, keepdims=True)
    i2 = jnp.min(jnp.where(el2 == t2, lane, big), axis=-1, keepdims=True)
    d = jnp.exp(t2 - t1)
    cw1 = g_w * (1.0 / (1.0 + d))
    cw2 = g_w * (d / (1.0 + d))

    oh1 = lane == i1
    oh2 = lane == i2
    picked = jnp.where(oh1, 1, jnp.where(oh2, 2, 0))
    hit = jnp.minimum(picked, 1)
    row_i = lax.broadcasted_iota(jnp.int32, (tm, LANES), 0)
    seen = hit
    shift = 1
    while shift < tm:
        seen = seen + jnp.where(row_i >= shift, pltpu.roll(seen, shift, axis=0), 0)
        shift *= 2
    run = run_ref[0:1, :]
    code_ref[...] = jnp.where(hit > 0, picked + 4 * (seen - hit + run), 0)
    run_new = jnp.broadcast_to(run + seen[tm - 1:tm, :], run_ref.shape)
    run_ref[...] = run_new
    cnt_ref[...] = run_new

    route_ref[...] = jnp.where(lane_i == 0, cw1, jnp.where(lane_i == 1, cw2, 0.0))


def _stage2(attn_in, ob, ga, gb, x2d, gt1, sh2, sc2, wua, wub, wo, g2, wr_hi, wr_lo, br, *, tm, mod_index, stride_index,
            merged, name):
    n = x2d.shape[0]
    row = lambda i: (i, 0)
    const2 = lambda i: (0, 0)
    mod_block = (1,) + gt1.shape[1:]
    scratch = [pltpu.VMEM((8, LANES), jnp.int32)]
    if merged:
        attn_specs = [pl.BlockSpec((tm, D_A), row)]
    else:
        attn_specs = [pl.BlockSpec((1, dil, tm // dil, D_A), stride_index) for _, dil in DILATED_GROUPS]
        attn_specs += [pl.BlockSpec((1, dil, tm // dil, D_A), stride_index) for _, dil in DILATED_GROUPS]
        scratch += [pltpu.VMEM((D_A // LANES, tm, LANES), F32), pltpu.VMEM((D_A // LANES, tm, LANES), F32)]
    in_specs = attn_specs + [
        pl.BlockSpec((tm, D_B), row), pl.BlockSpec((tm, D_MODEL), row), pl.BlockSpec((tm, D_MODEL), row),
        pl.BlockSpec((tm, D_MODEL), row),
        pl.BlockSpec(mod_block, mod_index), pl.BlockSpec(mod_block, mod_index), pl.BlockSpec(mod_block, mod_index),
        pl.BlockSpec((D_A, D_MODEL), const2), pl.BlockSpec((D_B, D_MODEL), const2),
        pl.BlockSpec((D_MODEL, D_MODEL), const2), pl.BlockSpec((1, D_MODEL), const2),
        pl.BlockSpec((D_MODEL, LANES), const2), pl.BlockSpec((D_MODEL, LANES), const2),
        pl.BlockSpec((1, LANES), const2),
    ]
    out_shape = [jax.ShapeDtypeStruct((n, D_MODEL), F32), jax.ShapeDtypeStruct((n, D_MODEL), F32),
                 jax.ShapeDtypeStruct((n, LANES), F32), jax.ShapeDtypeStruct((n, LANES), jnp.int32),
                 jax.ShapeDtypeStruct((8, LANES), jnp.int32)]
    out_specs = [pl.BlockSpec((tm, D_MODEL), row), pl.BlockSpec((tm, D_MODEL), row),
                 pl.BlockSpec((tm, LANES), row), pl.BlockSpec((tm, LANES), row), pl.BlockSpec((8, LANES), const2)]
    return pl.pallas_call(
        functools.partial(_stage2_body, merged=merged),
        out_shape=out_shape, grid=(n // tm,), in_specs=in_specs, out_specs=out_specs,
        scratch_shapes=scratch,
        compiler_params=_params(1), name=name,
    )(*attn_in, ob, ga, gb, x2d, gt1, sh2, sc2, wua, wub, wo, g2, wr_hi, wr_lo, br)


def _dispatch_body(cnt_ref, off_ref, pad_ref, nv_ref, posp_ref, poss_ref, hp_ref, hs_ref, xs_ref,
                   zero_ref, buf_ref, load_sem, scat_sem, zero_sem, *, tiles_p, n_tiles):
    step = pl.program_id(0)
    tile = zero_ref.shape[0]
    tm = buf_ref.shape[1]
    n_s = hs_ref.shape[0]
    slot = lax.rem(step, 3)
    ahead = lax.rem(step + 1, 3)

    def load_prompt(j, s):
        return pltpu.make_async_copy(hp_ref.at[pl.ds(j * tm, tm)], buf_ref.at[s], load_sem.at[s])

    def load_sample(s):
        return pltpu.make_async_copy(hs_ref, buf_ref.at[s, pl.ds(0, n_s)], load_sem.at[s])

    def scatter(rows, pos_ref, s):
        def issue(t, c):
            _row_copy(buf_ref.at[s], t, xs_ref, pos_ref[0, 0, 2 * t], scat_sem.at[s]).start()
            _row_copy(buf_ref.at[s], t, xs_ref, pos_ref[0, 0, 2 * t + 1], scat_sem.at[s]).start(priority=1)
            return c

        lax.fori_loop(0, rows, issue, 0, unroll=8)

    def drain(rows, s):
        def one(t, c):
            _row_copy(buf_ref.at[s], 0, xs_ref, 0, scat_sem.at[s]).wait()
            _row_copy(buf_ref.at[s], 0, xs_ref, 0, scat_sem.at[s]).wait()
            return c

        lax.fori_loop(0, rows, one, 0, unroll=8)

    @pl.when(step == 0)
    def _():
        zero_ref[...] = jnp.zeros_like(zero_ref)
        load_prompt(0, 0).start()
        bits = int(math.log2(tile))

        def pad_copies(e, wait):
            n = pad_ref[e] - cnt_ref[e]
            base = off_ref[e] + cnt_ref[e]
            end = off_ref[e] + pad_ref[e]

            def one_row(i, c):
                copy = _row_copy(zero_ref, 0, xs_ref, base + i, zero_sem)
                if wait:
                    copy.wait()
                else:
                    copy.start()
                return c

            lax.fori_loop(0, jnp.bitwise_and(n, 7), one_row, 0)
            for b in range(3, bits):
                size = 1 << b
                first = pl.multiple_of(end - jnp.bitwise_and(n, ~(2 * size - 1)) - size, 8)
                copy = pltpu.make_async_copy(zero_ref.at[pl.ds(0, size)], xs_ref.at[pl.ds(first, size)], zero_sem)

                @pl.when(jnp.bitwise_and(n, size) != 0)
                def _():
                    if wait:
                        copy.wait()
                    else:
                        copy.start()

        def start_pads(e, c):
            pad_copies(e, False)
            return c

        def wait_pads(e, c):
            pad_copies(e, True)
            return c

        def tile_copy(j):
            return pltpu.make_async_copy(zero_ref, xs_ref.at[pl.ds(j * tile, tile)], zero_sem)

        def start_tile(j, c):
            tile_copy(j).start()
            return c

        def wait_tile(j, c):
            tile_copy(j).wait()
            return c

        lax.fori_loop(0, N_EXPERTS, start_pads, 0)
        lax.fori_loop(nv_ref[0], n_tiles, start_tile, 0)
        lax.fori_loop(0, N_EXPERTS, wait_pads, 0)
        lax.fori_loop(nv_ref[0], n_tiles, wait_tile, 0)

    @pl.when(step >= 2)
    def _():
        drain(tm, ahead)

    @pl.when(step + 1 < tiles_p)
    def _():
        load_prompt(step + 1, ahead).start()

    @pl.when(step + 1 == tiles_p)
    def _():
        load_sample(ahead).start()

    @pl.when(step < tiles_p)
    def _():
        load_prompt(step, slot).wait()
        scatter(tm, posp_ref, slot)

    @pl.when(step == tiles_p)
    def _():
        load_sample(slot).wait()
        scatter(n_s, poss_ref, slot)
        drain(tm, lax.rem(step + 2, 3))
        drain(n_s, slot)


def _dispatch(count, offset, padded, n_valid, pos_p, pos_s, h2_p, h2_s, *, cap, tm, tile):
    n_p, n_s = h2_p.shape[0], h2_s.shape[0]
    tiles_p = n_p // tm
    assert tiles_p >= 2 and n_s <= tm
    last_p = lambda i, *_: (jnp.minimum(i, tiles_p - 1), 0, 0)
    in_specs = [pl.BlockSpec((1, 1, 2 * tm), last_p, memory_space=pltpu.SMEM),
                pl.BlockSpec((1, 1, 2 * n_s), lambda i, *_: (0, 0, 0), memory_space=pltpu.SMEM),
                pl.BlockSpec(memory_space=pl.ANY),
                pl.BlockSpec(memory_space=pl.ANY)]
    grid_spec = pltpu.PrefetchScalarGridSpec(
        num_scalar_prefetch=4, grid=(tiles_p + 1,), in_specs=in_specs,
        out_specs=pl.BlockSpec(memory_space=pl.ANY),
        scratch_shapes=[pltpu.VMEM((tile, D_MODEL), F32), pltpu.VMEM((3, tm, D_MODEL), F32),
                        pltpu.SemaphoreType.DMA((3,)), pltpu.SemaphoreType.DMA((3,)), pltpu.SemaphoreType.DMA(())],
    )
    return pl.pallas_call(
        functools.partial(_dispatch_body, tiles_p=tiles_p, n_tiles=cap // tile),
        out_shape=jax.ShapeDtypeStruct((cap, D_MODEL), F32), grid_spec=grid_spec,
        compiler_params=pltpu.CompilerParams(dimension_semantics=("arbitrary",), vmem_limit_bytes=VMEM_LIMIT,
                                             has_side_effects=True),
        name="dispatch",
    )(count, offset, padded, n_valid, pos_p.reshape(tiles_p, 1, 2 * tm), pos_s.reshape(1, 1, 2 * n_s), h2_p, h2_s)


def _moe_body(te_ref, nv_ref, x_ref, wg_ref, wu_ref, wd_ref, y_ref, wg_s, wu_s, wd_s):
    i = pl.program_id(0)
    active = i < nv_ref[0]
    fresh = (i == 0) | (te_ref[i] != te_ref[jnp.maximum(i - 1, 0)])

    @pl.when(active & fresh)
    def _():
        wg_s[...] = wg_ref[0].astype(BF16)
        wu_s[...] = wu_ref[0].astype(BF16)
        wd_s[...] = wd_ref[0].astype(BF16)

    @pl.when(active)
    def _():
        x = x_ref[...].astype(BF16)
        a = _dot(x, wg_s[...])
        b = _dot(x, wu_s[...])
        mid = (a * _sigmoid(a) * b).astype(BF16)
        y_ref[...] = _dot(mid, wd_s[...])

    @pl.when(jnp.logical_not(active))
    def _():
        y_ref[...] = jnp.zeros_like(y_ref)


def _moe(tile_expert, n_valid, x_sorted, w_eg, w_eu, w_ed, tm):
    cap = x_sorted.shape[0]
    last = lambda i, nv: jnp.maximum(jnp.minimum(i, nv[0] - 1), 0)
    xmap = lambda i, te, nv: (last(i, nv), 0)
    wmap = lambda i, te, nv: (te[last(i, nv)], 0, 0)
    grid_spec = pltpu.PrefetchScalarGridSpec(
        num_scalar_prefetch=2,
        grid=(cap // tm,),
        in_specs=[pl.BlockSpec((tm, D_MODEL), xmap),
                  pl.BlockSpec((1, D_MODEL, D_EXPERT), wmap),
                  pl.BlockSpec((1, D_MODEL, D_EXPERT), wmap),
                  pl.BlockSpec((1, D_EXPERT, D_MODEL), wmap)],
        out_specs=pl.BlockSpec((tm, D_MODEL), lambda i, te, nv: (i, 0)),
        scratch_shapes=[pltpu.VMEM((D_MODEL, D_EXPERT), BF16), pltpu.VMEM((D_MODEL, D_EXPERT), BF16),
                        pltpu.VMEM((D_EXPERT, D_MODEL), BF16)],
    )
    return pl.pallas_call(
        _moe_body, out_shape=jax.ShapeDtypeStruct((cap, D_MODEL), F32), grid_spec=grid_spec,
        compiler_params=_params(1), name="moe",
    )(tile_expert, n_valid, x_sorted, w_eg, w_eu, w_ed)


def _final_body(pos_ref, next_ref, x1_ref, route_ref, gt2_ref, gf_ref, ys_ref, o_ref, y_buf, sem):
    step = pl.program_id(0)
    tm = x1_ref.shape[0]
    slot = lax.rem(step, 2)

    def gather(p_ref, s):
        def issue(t, c):
            _row_copy(ys_ref, p_ref[0, 0, 2 * t], y_buf.at[s, 0], t, sem.at[s]).start()
            _row_copy(ys_ref, p_ref[0, 0, 2 * t + 1], y_buf.at[s, 1], t, sem.at[s]).start(priority=1)
            return c

        lax.fori_loop(0, tm, issue, 0, unroll=8)

    @pl.when(step == 0)
    def _():
        gather(pos_ref, 0)

    @pl.when(step + 1 < pl.num_programs(0))
    def _():
        gather(next_ref, 1 - slot)

    def drain(t, c):
        _row_copy(ys_ref, 0, y_buf.at[slot, 0], 0, sem.at[slot]).wait()
        _row_copy(ys_ref, 0, y_buf.at[slot, 1], 0, sem.at[slot]).wait()
        return c

    lax.fori_loop(0, tm, drain, 0, unroll=8)

    route = route_ref[...]
    moe = route[:, 0:1] * y_buf[slot, 0] + route[:, 1:2] * y_buf[slot, 1]
    x = x1_ref[...] + gt2_ref[0] * moe
    o_ref[...] = x * lax.rsqrt(jnp.mean(x * x, axis=-1, keepdims=True) + EPS) * gf_ref[...]


def _final(pos, x1, route, gt2, gf, y_sorted, *, tm, mod_index, name):
    n = x1.shape[0]
    steps = n // tm
    row = lambda i: (i, 0)
    mod_block = (1,) + gt2.shape[1:]
    pos3 = pos.reshape(steps, 1, 2 * tm)
    return pl.pallas_call(
        _final_body,
        out_shape=jax.ShapeDtypeStruct((n, D_MODEL), F32),
        grid=(steps,),
        in_specs=[pl.BlockSpec((1, 1, 2 * tm), lambda i: (i, 0, 0), memory_space=pltpu.SMEM),
                  pl.BlockSpec((1, 1, 2 * tm), lambda i: (jnp.minimum(i + 1, steps - 1), 0, 0),
                               memory_space=pltpu.SMEM),
                  pl.BlockSpec((tm, D_MODEL), row), pl.BlockSpec((tm, LANES), row),
                  pl.BlockSpec(mod_block, mod_index), pl.BlockSpec((1, D_MODEL), lambda i: (0, 0)),
                  pl.BlockSpec(memory_space=pl.ANY)],
        out_specs=pl.BlockSpec((tm, D_MODEL), row),
        scratch_shapes=[pltpu.VMEM((2, 2, tm, D_MODEL), F32), pltpu.SemaphoreType.DMA((2,))],
        compiler_params=_params(1), name=name,
    )(pos3, pos3, x1, route, gt2, gf, y_sorted)


def _t5_bucket(dist):
    max_exact = N_BUCKETS // 2
    dist = np.asarray(dist)
    d = np.maximum(dist, 1).astype(np.float32)
    large = max_exact + (np.log(d / np.float32(max_exact)) / np.float32(math.log(MAX_DISTANCE / max_exact))
                         * np.float32(N_BUCKETS - max_exact)).astype(np.int32)
    return np.where(dist < max_exact, dist, np.minimum(large, N_BUCKETS - 1)).astype(np.int32)


def _bias_lookup(tab, dist):
    onehot = (jnp.asarray(_t5_bucket(dist))[..., None] == jnp.arange(N_BUCKETS)).astype(F32)
    return jnp.einsum("...b,bh->...h", onehot, tab, precision=HIGHEST)


def _prompt_bias(tab, dil):
    delta = np.arange(BLOCK)[:, None] + BLOCK - np.arange(2 * BLOCK)[None, :]
    return jnp.transpose(_bias_lookup(tab, np.maximum(delta, 0) * dil), (2, 0, 1))


def _sample_tables(tab, win, dil, t_new):
    buf = np.concatenate([np.arange(win), win + np.arange(t_new), np.full(LANES - t_new, -1)])
    back = win + np.arange(t_new)[:, None] - buf[None, :]
    valid = (buf[None, :] >= 0) & (back >= 0) & (back % dil == 0) & (back // dil <= STEPS)
    bias = _bias_lookup(tab, np.maximum(back, 0))
    bias = jnp.transpose(bias, (2, 0, 1)).reshape(HEADS * t_new, buf.shape[0])
    mask = np.broadcast_to(valid[None], (HEADS, t_new, buf.shape[0])).reshape(HEADS * t_new, -1)
    return bias, jnp.asarray(mask.astype(np.float32))


def kernel(x_prompt, x_sample, c_prompt, c_sample, cache_kv_w128, cache_kv_w512, cache_kv_w2048, rel_bias, w_ada,
           b_ada, g_norm1, w_in, ln_v_g, ln_v_b, w_spatial, b_spatial, w_up_a, w_up_b, w_out, g_norm2,
           w_route_group, b_route_group, w_route_expert, b_route_expert, w_e_gate, w_e_up, w_e_down, g_final):
    assert w_ada.shape[0] == 1, "single layer"
    bp, s_len, _ = x_prompt.shape
    bs, t_new, _ = x_sample.shape
    n_p, n_s = bp * s_len, bs * t_new
    caches = (cache_kv_w128[0], cache_kv_w512[0], cache_kv_w2048[0])
    assert all(c.shape[1] == win for c, (win, _) in zip(caches, DILATED_GROUPS)), "cache holds one full window"

    mod = _ada(jnp.concatenate([c_prompt, c_sample], axis=0), w_ada[0], b_ada[0])
    mod_p = [mod[:bp, j * D_MODEL:(j + 1) * D_MODEL].reshape(bp, 1, D_MODEL) for j in range(6)]
    mod_s = [jnp.repeat(mod[bp:, j * D_MODEL:(j + 1) * D_MODEL], t_new, axis=0).reshape(1, n_s, D_MODEL)
             for j in range(6)]

    w_in_bf = w_in[0].astype(BF16)
    g1 = g_norm1[0].reshape(1, D_MODEL)
    lng, lnb = ln_v_g[0].reshape(1, D_B), ln_v_b[0].reshape(1, D_B)
    ws_p = w_spatial[0]
    bs_p = jnp.repeat(jnp.transpose(b_spatial[0]), SGU_DIM, axis=1)
    reps = n_s // t_new
    ws_s = jnp.tile(w_spatial[0][:, :t_new, :t_new], (1, reps, reps))
    bs_s = jnp.tile(bs_p[:t_new], (reps, 1))

    tpb = s_len // TM_STAGE1
    tail_p = [(bp, min(win, s_len)) for win, _ in DILATED_GROUPS]

    def tail_index_p(i, g, rows):
        first = tpb - tail_p[g][1] // rows
        return (i // tpb, jnp.maximum(i % tpb - first, 0), 0)

    outs = _stage1(
        x_prompt.reshape(n_p, D_MODEL), mod_p[0], mod_p[1], g1, w_in_bf, lng, lnb, ws_p, bs_p,
        tm=TM_STAGE1, mod_index=lambda i: (i // tpb, 0, 0),
        stride_index=(bp, s_len, lambda i: (i // tpb, 0, i % tpb, 0)), tail_shapes=tail_p, tail_index=tail_index_p,
        period=CHUNK, by_stride=True, emit_vn=False, name="stage1_prompt")
    q_g, k_g, v_g = outs[0:3], outs[3:6], outs[6:9]
    kv0_p, kv1_p, kv2_p, ob_p, ga_p, gb_p = outs[9:]
    q_s, kv0_s, kv1_s, kv2_s, ob_s, ga_s, gb_s, vn_s = _stage1(
        x_sample.reshape(n_s, D_MODEL), mod_s[0], mod_s[1], g1, w_in_bf, lng, lnb, ws_s, bs_s,
        tm=n_s, mod_index=lambda i: (0, 0, 0), stride_index=None, tail_shapes=[(1, n_s)] * N_DIL,
        tail_index=lambda i, g, rows: (0, 0, 0),
        period=t_new, by_stride=False, emit_vn=True, name="stage1_sample")

    o_groups, lse_groups = [], []
    sb, sm = [], []
    for g, (win, dil) in enumerate(DILATED_GROUPS):
        tab = rel_bias[:, g * HEADS:(g + 1) * HEADS].astype(F32)
        o, lse = _attn_prompt_group(q_g[g], k_g[g], v_g[g], _prompt_bias(tab, dil), dil, ATTN_BLOCKS_PER_STEP)
        o_groups.append(o)
        lse_groups.append(lse)
        bias, mask = _sample_tables(tab, win, dil, t_new)
        sb.append(bias)
        sm.append(mask)
    caches_t = [jnp.transpose(c, (0, 2, 3, 4, 1)).reshape(bs, 2, D_A, c.shape[1]) for c in caches]
    cache0_t, cache1_t, cache2_t, oa_s = _sample_cache(q_s, (kv0_s, kv1_s, kv2_s), caches_t, sb, sm, t_new)

    wua, wub, wo = w_up_a[0].astype(BF16), w_up_b[0].astype(BF16), w_out[0].astype(BF16)
    g2 = g_norm2[0].reshape(1, D_MODEL)
    wr = jnp.concatenate([w_route_expert[0].reshape(D_MODEL, N_EXPERTS), w_route_group[0]], axis=1)
    wr = jnp.pad(wr, ((0, 0), (0, LANES - wr.shape[1])))
    br = jnp.concatenate([b_route_expert[0].reshape(N_EXPERTS), b_route_group[0]])
    br = jnp.pad(br, (0, LANES - br.shape[0])).reshape(1, LANES)
    wr_hi = wr.astype(BF16)
    wr_lo = (wr - wr_hi.astype(F32)).astype(BF16)

    tpb2 = s_len // TM_STAGE2
    x1_p, h2_p, route_p, code_p, cnt_p = _stage2(
        o_groups + lse_groups, ob_p, ga_p, gb_p, x_prompt.reshape(n_p, D_MODEL),
        mod_p[2], mod_p[3], mod_p[4], wua, wub, wo, g2, wr_hi, wr_lo, br,
        tm=TM_STAGE2, mod_index=lambda i: (i // tpb2, 0, 0), stride_index=lambda i: (i // tpb2, 0, i % tpb2, 0),
        merged=False, name="stage2_prompt")
    x1_s, h2_s, route_s, code_s, cnt_s = _stage2(
        [oa_s], ob_s, ga_s, gb_s, x_sample.reshape(n_s, D_MODEL),
        mod_s[2], mod_s[3], mod_s[4], wua, wub, wo, g2, wr_hi, wr_lo, br,
        tm=n_s, mod_index=lambda i: (0, 0, 0), stride_index=None, merged=True, name="stage2_sample")

    cap = 2 * (n_p + n_s) + N_EXPERTS * TM_MOE
    n_tiles = cap // TM_MOE
    count_p = cnt_p[0, :N_EXPERTS]
    count = count_p + cnt_s[0, :N_EXPERTS]
    padded = ((count + TM_MOE - 1) // TM_MOE) * TM_MOE
    ends = jnp.cumsum(padded)
    offset = ends - padded
    n_valid = (ends[-1] // TM_MOE).astype(jnp.int32).reshape(1)
    tile_start = jnp.arange(n_tiles, dtype=jnp.int32) * TM_MOE
    tile_expert = jnp.minimum(jnp.sum(ends[None, :] <= tile_start[:, None], axis=1), N_EXPERTS - 1).astype(jnp.int32)

    def sorted_rows(code, earlier):
        row = jnp.pad(offset + earlier, (0, LANES - N_EXPERTS))[None, :] + jnp.right_shift(code, 2)
        return jnp.stack([jnp.sum(jnp.where((code & 3) == k, row, 0), axis=1) for k in (1, 2)], axis=1)

    pos_p = sorted_rows(code_p, jnp.zeros_like(count_p))
    pos_s = sorted_rows(code_s, count_p)
    x_sorted = _dispatch(count, offset, padded, n_valid, pos_p, pos_s, h2_p, h2_s, cap=cap, tm=TM_FINAL, tile=TM_MOE)

    y_sorted = _moe(tile_expert, n_valid, x_sorted, w_e_gate[0], w_e_up[0], w_e_down[0], TM_MOE)

    gf = g_final.reshape(1, D_MODEL)
    tpb3 = s_len // TM_FINAL
    y_p = _final(pos_p, x1_p, route_p, mod_p[5], gf, y_sorted,
                 tm=TM_FINAL, mod_index=lambda i: (i // tpb3, 0, 0), name="final_prompt")
    y_s = _final(pos_s, x1_s, route_s, mod_s[5], gf, y_sorted,
                 tm=n_s, mod_index=lambda i: (0, 0, 0), name="final_sample")

    kv_p = [a.reshape(1, bp, a.shape[1], 2, HEADS, HEAD_DIM) for a in (kv0_p, kv1_p, kv2_p)]
    kv_s = [jnp.transpose(c.reshape(bs, 2, HEADS, HEAD_DIM, c.shape[-1]), (0, 4, 1, 2, 3))[None]
            for c in (cache0_t, cache1_t, cache2_t)]
    return (y_p.reshape(bp, s_len, D_MODEL), y_s.reshape(bs, t_new, D_MODEL),
            kv_p[0], kv_p[1], kv_p[2], kv_s[0], kv_s[1], kv_s[2],
            vn_s.reshape(1, bs, t_new, D_B))
```

```python
import functools
import math

import numpy as np
import jax
import jax.numpy as jnp
from jax import lax
from jax.experimental import pallas as pl
from jax.experimental.pallas import tpu as pltpu

F32 = jnp.float32
BF16 = jnp.bfloat16
HIGHEST = lax.Precision.HIGHEST

D_MODEL = 1024
HEAD_DIM = 64
HEADS = 8
DILATED_GROUPS = ((128, 1), (512, 4), (2048, 16))
N_DIL = 3
D_A = HEADS * HEAD_DIM
D_QKV = N_DIL * D_A
BLOCK = 128
STEPS = 128
SCALE = HEAD_DIM ** -0.5
CHUNK = 128
D_B = 512
SGU_GROUPS = 4
SGU_DIM = D_B // SGU_GROUPS
N_BUCKETS = 32
MAX_DISTANCE = 2048
MOE_GROUPS = 4
EXPERTS_PER_GROUP = 8
N_EXPERTS = 32
D_EXPERT = 512
EPS = 1e-6
NEG_INF = -1e30
LANES = 128
C_U = 3 * D_QKV
C_V = C_U + D_B
C_GA = C_V + D_B
C_GB = C_GA + D_MODEL
IN_COLS = C_GB + D_MODEL

TM_STAGE1 = 256
TM_STAGE2 = 512
TM_MOE = 512
ATTN_BLOCKS_PER_STEP = 4
TM_FINAL = 512
VMEM_LIMIT = 56 * 1024 * 1024


def _dot(a, b):
    return jnp.dot(a, b, preferred_element_type=F32)


def _dot_nt(a, b):
    return lax.dot_general(a, b, (((1,), (1,)), ((), ())), preferred_element_type=F32)


def _sigmoid(x):
    return 1.0 / (1.0 + jnp.exp(-x))


def _gelu(x):
    return 0.5 * x * (1.0 + lax.erf(x * (2.0 ** -0.5)))


def _params(n_grid):
    return pltpu.CompilerParams(dimension_semantics=("arbitrary",) * n_grid, vmem_limit_bytes=VMEM_LIMIT)


def _row_copy(src, src_row, dst, dst_row, sem):
    return pltpu.make_async_copy(src.at[pl.ds(src_row, 1)], dst.at[pl.ds(dst_row, 1)], sem)


def _ada_body(c_ref, w_ref, b_ref, o_ref):
    c = c_ref[...]
    s = c * _sigmoid(c)
    o_ref[...] = jnp.dot(s, w_ref[...], precision=HIGHEST, preferred_element_type=F32) + b_ref[...]


def _ada(c_all, w_ada, b_ada):
    n, d = c_all.shape
    cols = w_ada.shape[1]
    tn = 1024
    return pl.pallas_call(
        _ada_body,
        out_shape=jax.ShapeDtypeStruct((n, cols), F32),
        grid=(cols // tn,),
        in_specs=[pl.BlockSpec((n, d), lambda j: (0, 0)),
                  pl.BlockSpec((d, tn), lambda j: (0, j)),
                  pl.BlockSpec((1, tn), lambda j: (0, j))],
        out_specs=pl.BlockSpec((n, tn), lambda j: (0, j)),
        compiler_params=_params(1),
        name="ada",
    )(c_all, w_ada, b_ada.reshape(1, cols))


def _stage1_body(x_ref, sh_ref, sc_ref, g1_ref, w_ref, lng_ref, lnb_ref, ws_ref, bs_ref, *rest,
                 period, by_stride, emit_vn, cache_t_new):
    rest = list(rest)
    cache_refs = [rest.pop(0) for _ in range(1 + 3 * N_DIL)] if cache_t_new else []
    if by_stride:
        qkv_refs = [[rest.pop(0) for _ in range(N_DIL)] for _ in range(3)]
    else:
        q_ref = rest.pop(0)
    kv_refs = [rest.pop(0) for _ in range(N_DIL)]
    ob_ref, ga_ref, gb_ref = rest.pop(0), rest.pop(0), rest.pop(0)
    vn_ref = rest.pop(0) if emit_vn else None
    if cache_t_new:
        cache_refs += [rest.pop(0) for _ in range(N_DIL + 1)]
    stage_ref = rest.pop(0) if by_stride else None
    if cache_t_new:
        _sample_cache_step(pl.program_id(0), cache_refs + rest, cache_t_new)

    x = x_ref[...]
    tm = x.shape[0]
    h = x * lax.rsqrt(jnp.mean(x * x, axis=-1, keepdims=True) + EPS) * g1_ref[...]
    h = h * (1.0 + sc_ref[0]) + sh_ref[0]
    hb = h.astype(BF16)

    q = _dot(hb, w_ref[:, 0:D_QKV]) * SCALE
    k = _dot(hb, w_ref[:, D_QKV:2 * D_QKV])
    v = _dot(hb, w_ref[:, 2 * D_QKV:3 * D_QKV])
    if by_stride:
        per_group = D_A // LANES
        for val, refs in zip((q, k, v), qkv_refs):
            for c in range(D_QKV // LANES):
                stage_ref[c] = val[:, c * LANES:(c + 1) * LANES]
            for g, (_, dil) in enumerate(DILATED_GROUPS):
                for r in range(dil):
                    for c in range(per_group):
                        piece = stage_ref[g * per_group + c, pl.ds(r, tm // dil, stride=dil), :]
                        refs[g][0, r, :, c * LANES:(c + 1) * LANES] = piece.astype(BF16)
    else:
        q_ref[...] = q
    for g in range(N_DIL):
        rows = kv_refs[g].shape[1]
        kv_refs[g][0, :, 0:D_A] = k[tm - rows:, g * D_A:(g + 1) * D_A]
        kv_refs[g][0, :, D_A:2 * D_A] = v[tm - rows:, g * D_A:(g + 1) * D_A]

    ga_ref[...] = _sigmoid(_dot(hb, w_ref[:, C_GA:C_GB])).astype(BF16)
    gb_ref[...] = _sigmoid(_dot(hb, w_ref[:, C_GB:IN_COLS])).astype(BF16)

    u = _gelu(_dot(hb, w_ref[:, C_U:C_V]))
    vb = _gelu(_dot(hb, w_ref[:, C_V:C_GA]))
    xc = vb - jnp.mean(vb, axis=-1, keepdims=True)
    vn = xc * lax.rsqrt(jnp.mean(xc * xc, axis=-1, keepdims=True) + EPS) * lng_ref[...] + lnb_ref[...]
    if emit_vn:
        vn_ref[...] = vn
    vnb = vn.astype(BF16)

    r = ws_ref.shape[1]
    ri = lax.broadcasted_iota(jnp.int32, (r, r), 0)
    ci = lax.broadcasted_iota(jnp.int32, (r, r), 1)
    keep = ci <= ri
    if period != r:
        sh = int(math.log2(period))
        keep = keep & (jnp.right_shift(ri, sh) == jnp.right_shift(ci, sh))
    for g in range(SGU_GROUPS):
        wg = jnp.where(keep, ws_ref[g], 0.0).astype(BF16)
        cs = slice(g * SGU_DIM, (g + 1) * SGU_DIM)
        for c in range(tm // r):
            rs = slice(c * r, (c + 1) * r)
            mixed = _dot(wg, vnb[rs, cs]) + bs_ref[:, cs]
            ob_ref[rs, cs] = (u[rs, cs] * mixed).astype(BF16)


def _stage1(x2d, sh, sc, g1, w_in_bf, lng, lnb, ws, bs, *, tm, mod_index, stride_index, tail_shapes, tail_index,
            period, by_stride, emit_vn, name, cache=None):
    n = x2d.shape[0]
    grid = (n // tm,)
    mod_block = (1,) + sh.shape[1:]
    r = ws.shape[1]
    const2 = lambda i: (0, 0)
    row = lambda i: (i, 0)
    in_specs = [
        pl.BlockSpec((tm, D_MODEL), row),
        pl.BlockSpec(mod_block, mod_index),
        pl.BlockSpec(mod_block, mod_index),
        pl.BlockSpec((1, D_MODEL), const2),
        pl.BlockSpec((D_MODEL, IN_COLS), const2, pipeline_mode=pl.Buffered(1)),
        pl.BlockSpec((1, D_B), const2),
        pl.BlockSpec((1, D_B), const2),
        pl.BlockSpec((SGU_GROUPS, r, r), lambda i: (0, 0, 0)),
        pl.BlockSpec((r, D_B), const2),
    ]
    out_shape, out_specs, scratch = [], [], []
    if by_stride:
        n_batch, s_len = stride_index[0], stride_index[1]
        for _ in range(3):
            for _, dil in DILATED_GROUPS:
                out_shape.append(jax.ShapeDtypeStruct((n_batch, dil, s_len // dil, D_A), BF16))
                out_specs.append(pl.BlockSpec((1, dil, tm // dil, D_A), stride_index[2]))
        scratch.append(pltpu.VMEM((D_QKV // LANES, tm, LANES), F32))
    else:
        out_shape.append(jax.ShapeDtypeStruct((n, D_QKV), F32))
        out_specs.append(pl.BlockSpec((tm, D_QKV), row))
    for g in range(N_DIL):
        nb, keep = tail_shapes[g]
        rows = min(tm, keep)
        out_shape.append(jax.ShapeDtypeStruct((nb, keep, 2 * D_A), F32))
        out_specs.append(pl.BlockSpec((1, rows, 2 * D_A), functools.partial(tail_index, g=g, rows=rows)))
    out_shape += [jax.ShapeDtypeStruct((n, D_B), BF16),
                  jax.ShapeDtypeStruct((n, D_MODEL), BF16),
                  jax.ShapeDtypeStruct((n, D_MODEL), BF16)]
    out_specs += [pl.BlockSpec((tm, D_B), row), pl.BlockSpec((tm, D_MODEL), row), pl.BlockSpec((tm, D_MODEL), row)]
    if emit_vn:
        out_shape.append(jax.ShapeDtypeStruct((n, D_B), F32))
        out_specs.append(pl.BlockSpec((tm, D_B), row))
    args = [x2d, sh, sc, g1, w_in_bf, lng, lnb, ws, bs]
    cache_t_new = 0
    if cache is not None:
        c_args, c_in, c_shape, c_out, c_scratch, units = _sample_cache_specs(*cache)
        assert units == grid[0], "one sample-cache unit per projection tile"
        args += c_args
        in_specs += c_in
        out_shape += c_shape
        out_specs += c_out
        scratch += c_scratch
        cache_t_new = cache[-1]
    body = functools.partial(_stage1_body, period=period, by_stride=by_stride, emit_vn=emit_vn,
                             cache_t_new=cache_t_new)
    return pl.pallas_call(
        body, out_shape=out_shape, grid=grid, in_specs=in_specs, out_specs=out_specs, scratch_shapes=scratch,
        compiler_params=_params(1), name=name,
    )(*args)


def _attn_prompt_body(q_ref, kp_ref, kc_ref, vp_ref, vc_ref, bias_ref, o_ref, lse_ref):
    n = pl.program_id(2)
    n_sub = q_ref.shape[2] // BLOCK
    ri = lax.broadcasted_iota(jnp.int32, (BLOCK, 2 * BLOCK), 0)
    ci = lax.broadcasted_iota(jnp.int32, (BLOCK, 2 * BLOCK), 1)
    delta = ri + BLOCK - ci
    band = (delta >= 0) & (delta <= STEPS)
    first_key = jnp.where(n > 0, 0, BLOCK)
    low_q = lax.broadcasted_iota(jnp.int32, (BLOCK, LANES), 1) < HEAD_DIM
    low_v = lax.broadcasted_iota(jnp.int32, (2 * BLOCK, LANES), 1) < HEAD_DIM
    ones_lo = jnp.where(low_v, 1.0, 0.0).astype(BF16)
    ones_hi = jnp.where(low_v, 0.0, 1.0).astype(BF16)
    for sub in range(n_sub):
        rows = slice(sub * BLOCK, (sub + 1) * BLOCK)
        valid = band & (ci >= first_key) if sub == 0 else band
        for pair in range(HEADS // 2):
            cs = slice(pair * LANES, (pair + 1) * LANES)
            qp = q_ref[0, 0, rows, cs]
            if sub == 0:
                kp = jnp.concatenate([kp_ref[0, 0, :, cs], kc_ref[0, 0, 0:BLOCK, cs]], axis=0)
                vp = jnp.concatenate([vp_ref[0, 0, :, cs], vc_ref[0, 0, 0:BLOCK, cs]], axis=0)
            else:
                kp = kc_ref[0, 0, (sub - 1) * BLOCK:(sub + 1) * BLOCK, cs]
                vp = vc_ref[0, 0, (sub - 1) * BLOCK:(sub + 1) * BLOCK, cs]
            acc, tops = None, []
            for half in range(2):
                own_q = low_q if half == 0 else jnp.logical_not(low_q)
                own_v = low_v if half == 0 else jnp.logical_not(low_v)
                s = _dot_nt(jnp.where(own_q, qp, jnp.zeros_like(qp)), kp) + bias_ref[2 * pair + half]
                s = jnp.where(valid, s, NEG_INF)
                m = jnp.max(s, axis=-1, keepdims=True)
                p = jnp.exp(s - m).astype(BF16)
                w = jnp.concatenate([jnp.where(own_v, vp, jnp.zeros_like(vp)), ones_lo if half == 0 else ones_hi],
                                    axis=1)
                r = _dot(p, w)
                acc = r if acc is None else acc + r
                tops.append(m)
            den = acc[:, LANES:]
            o_ref[0, 0, rows, cs] = (acc[:, :LANES] / den).astype(BF16)
            lse_ref[0, 0, rows, cs] = jnp.where(low_q, tops[0], tops[1]) + jnp.log(den)


def _attn_prompt_group(q4, k4, v4, bias, dil, n_sub):
    b_sz, _, m_len, _ = q4.shape
    span = n_sub * BLOCK
    cur = pl.BlockSpec((1, 1, span, D_A), lambda b, r, n: (b, r, n, 0))
    prev = pl.BlockSpec((1, 1, BLOCK, D_A), lambda b, r, n: (b, r, jnp.maximum(n * n_sub - 1, 0), 0))
    return pl.pallas_call(
        _attn_prompt_body,
        out_shape=[jax.ShapeDtypeStruct((b_sz, dil, m_len, D_A), BF16),
                   jax.ShapeDtypeStruct((b_sz, dil, m_len, D_A), F32)],
        grid=(b_sz, dil, m_len // span),
        in_specs=[cur, prev, cur, prev, cur,
                  pl.BlockSpec((HEADS, BLOCK, 2 * BLOCK), lambda b, r, n: (0, 0, 0))],
        out_specs=[cur, cur],
        compiler_params=_params(3),
        name=f"attn_prompt_d{dil}",
    )(q4, k4, k4, v4, v4, bias)


def _sample_cache_step(unit, refs, t_new):
    refs = list(refs)
    q_ref = refs.pop(0)
    new_refs = [refs.pop(0) for _ in range(N_DIL)]
    cache_refs = [refs.pop(0) for _ in range(N_DIL)]
    bias_refs = [refs.pop(0) for _ in range(N_DIL)]
    out_refs = [refs.pop(0) for _ in range(N_DIL)]
    o_ref = refs.pop(0)
    p_refs = [refs.pop(0) for _ in range(N_DIL)]
    scale_ref = refs.pop(0)
    kv = lax.rem(unit, 2)
    rows = HEADS * t_new

    def allowed(g, col0, cols):
        win, dil = DILATED_GROUPS[g]
        tok = jnp.bitwise_and(lax.broadcasted_iota(jnp.int32, (rows, cols), 0), t_new - 1)
        col = lax.broadcasted_iota(jnp.int32, (rows, cols), 1) + col0
        back = win + tok - col
        return ((col < win + t_new) & (back >= 0) & (jnp.bitwise_and(back, dil - 1) == 0)
                & (back <= STEPS * dil))

    lane = lax.broadcasted_iota(jnp.int32, (D_A, LANES), 1)
    front = jnp.zeros((LANES - t_new, D_A), F32)
    for g in range(N_DIL):
        win = cache_refs[g].shape[-1]
        new = new_refs[g][0]
        half = jnp.where(kv == 0, new[:, 0:D_A], new[:, D_A:2 * D_A])
        tail = jnp.transpose(jnp.concatenate([front, half], axis=0))
        rolled = pltpu.roll(cache_refs[g][0, 0], win - t_new, axis=1)
        if win > LANES:
            out_refs[g][0, 0, :, 0:win - LANES] = rolled[:, 0:win - LANES]
        out_refs[g][0, 0, :, win - LANES:win] = jnp.where(lane >= LANES - t_new, tail, rolled[:, win - LANES:win])

    ri = lax.broadcasted_iota(jnp.int32, (rows, D_A), 0)
    ci = lax.broadcasted_iota(jnp.int32, (rows, D_A), 1)
    own_head = jnp.right_shift(ri, int(math.log2(t_new))) == jnp.right_shift(ci, int(math.log2(HEAD_DIM)))
    pad = jnp.zeros((LANES - t_new, D_A), F32)
    glane = lax.broadcasted_iota(jnp.int32, (rows, LANES), 1)

    @pl.when(kv == 0)
    def _():
        q = q_ref[...]
        stats = []
        for g in range(N_DIL):
            win = cache_refs[g].shape[-1]
            qg = q[:, g * D_A:(g + 1) * D_A]
            qblk = jnp.where(own_head, jnp.concatenate([qg] * HEADS, axis=0), 0.0).astype(BF16)
            s_c = _dot(qblk, cache_refs[g][0, 0].astype(BF16)) + bias_refs[g][:, 0:win]
            s_c = jnp.where(allowed(g, 0, win), s_c, NEG_INF)
            k_new = jnp.concatenate([new_refs[g][0][:, 0:D_A], pad], axis=0).astype(BF16)
            s_n = _dot_nt(qblk, k_new) + bias_refs[g][:, win:]
            s_n = jnp.where(allowed(g, win, LANES), s_n, NEG_INF)
            m = jnp.maximum(jnp.max(s_c, axis=-1, keepdims=True), jnp.max(s_n, axis=-1, keepdims=True))
            p_c = jnp.exp(s_c - m)
            p_n = jnp.exp(s_n - m)
            l = jnp.sum(p_c, axis=-1, keepdims=True) + jnp.sum(p_n, axis=-1, keepdims=True)
            p_refs[g][:, 0:win] = p_c.astype(BF16)
            p_refs[g][:, win:] = p_n.astype(BF16)
            stats.append((l, m + jnp.log(l)))
        top = jnp.maximum(jnp.maximum(stats[0][1], stats[1][1]), stats[2][1])
        es = [jnp.exp(lse - top) for _, lse in stats]
        den = es[0] + es[1] + es[2]
        scale = jnp.zeros((rows, LANES), F32)
        for g in range(N_DIL):
            scale = jnp.where(glane == g, es[g] / den / stats[g][0], scale)
        scale_ref[...] = scale

    @pl.when(kv == 1)
    def _():
        acc = jnp.zeros((rows, D_A), F32)
        for g in range(N_DIL):
            win = cache_refs[g].shape[-1]
            part = _dot_nt(p_refs[g][:, 0:win], cache_refs[g][0, 0].astype(BF16))
            v_new = jnp.concatenate([new_refs[g][0][:, D_A:2 * D_A], pad], axis=0).astype(BF16)
            part = part + _dot(p_refs[g][:, win:], v_new)
            acc = acc + scale_ref[:, g:g + 1] * part
        acc = jnp.where(own_head, acc, 0.0)
        out = acc[0:t_new]
        for h in range(1, HEADS):
            out = out + acc[h * t_new:(h + 1) * t_new]
        o_ref[...] = out


def _sample_cache_specs(q_s, kv_new, caches_t, biases, t_new):
    n = q_s.shape[0]
    nb = n // t_new
    blk4 = lambda i: (i // 2, lax.rem(i, 2), 0, 0)
    in_specs = [pl.BlockSpec((t_new, D_QKV), lambda i: (i // 2, 0))]
    in_specs += [pl.BlockSpec((1, t_new, 2 * D_A), lambda i: (i // 2, 0, 0))] * N_DIL
    in_specs += [pl.BlockSpec((1, 1, D_A, c.shape[-1]), blk4) for c in caches_t]
    in_specs += [pl.BlockSpec(a.shape, lambda i: (0, 0), pipeline_mode=pl.Buffered(1)) for a in biases]
    out_shape = [jax.ShapeDtypeStruct(c.shape, c.dtype) for c in caches_t]
    out_specs = [pl.BlockSpec((1, 1, D_A, c.shape[-1]), blk4) for c in caches_t]
    out_shape.append(jax.ShapeDtypeStruct((n, D_A), F32))
    out_specs.append(pl.BlockSpec((t_new, D_A), lambda i: (i // 2, 0)))
    rows = HEADS * t_new
    scratch = [pltpu.VMEM((rows, c.shape[-1] + LANES), BF16) for c in caches_t]
    scratch.append(pltpu.VMEM((rows, LANES), F32))
    args = [q_s, *[a.reshape(nb, t_new, 2 * D_A) for a in kv_new], *caches_t, *biases]
    return args, in_specs, out_shape, out_specs, scratch, 2 * nb


def _stage2_body(*refs, merged):
    refs = list(refs)
    if merged:
        oa_ref = refs.pop(0)
    else:
        o_refs = [refs.pop(0) for _ in range(N_DIL)]
        l_refs = [refs.pop(0) for _ in range(N_DIL)]
    (ob_ref, ga_ref, gb_ref, x_ref, gt1_ref, sh2_ref, sc2_ref, wua_ref, wub_ref, wo_ref, g2_ref,
     wrh_ref, wrl_ref, br_ref, x1_ref, h2_ref, route_ref, code_ref, cnt_ref, run_ref) = refs[:20]
    step = pl.program_id(0)
    tm = x_ref.shape[0]

    @pl.when(step == 0)
    def _():
        run_ref[...] = jnp.zeros_like(run_ref)

    if merged:
        oa = oa_ref[...].astype(BF16)
    else:
        o_stage, l_stage = refs[20], refs[21]

        def by_position(ref, stage, dil):
            if dil == 1:
                return ref[0, 0].astype(F32)
            slabs = ref.shape[-1] // LANES
            for r in range(dil):
                rows = ref[0, r].astype(F32)
                for c in range(slabs):
                    stage[c, pl.ds(r, tm // dil, stride=dil), :] = rows[:, c * LANES:(c + 1) * LANES]
            return jnp.concatenate([stage[c] for c in range(slabs)], axis=1)

        ls = [by_position(l_refs[g], l_stage, dil) for g, (_, dil) in enumerate(DILATED_GROUPS)]
        top = jnp.maximum(jnp.maximum(ls[0], ls[1]), ls[2])
        es = [jnp.exp(l - top) for l in ls]
        den = es[0] + es[1] + es[2]
        oa = None
        for g, (_, dil) in enumerate(DILATED_GROUPS):
            term = (es[g] / den) * by_position(o_refs[g], o_stage, dil)
            oa = term if oa is None else oa + term
        oa = oa.astype(BF16)

    ya = _dot(oa, wua_ref[...])
    yb = _dot(ob_ref[...], wub_ref[...])
    z = (ga_ref[...].astype(F32) * ya + gb_ref[...].astype(F32) * yb).astype(BF16)
    x1 = x_ref[...] + gt1_ref[0] * _dot(z, wo_ref[...])
    x1_ref[...] = x1
    h2 = x1 * lax.rsqrt(jnp.mean(x1 * x1, axis=-1, keepdims=True) + EPS) * g2_ref[...]
    h2 = h2 * (1.0 + sc2_ref[0]) + sh2_ref[0]
    h2_ref[...] = h2

    h_hi = h2.astype(BF16)
    h_lo = (h2 - h_hi.astype(F32)).astype(BF16)
    logit = (_dot(h_hi, wrh_ref[...]) + _dot(h_hi, wrl_ref[...]) + _dot(h_lo, wrh_ref[...])) + br_ref[...]
    lane_i = lax.broadcasted_iota(jnp.int32, (tm, LANES), 1)
    lane = lane_i.astype(F32)
    big = float(LANES)
    is_g = (lane_i >= N_EXPERTS) & (lane_i < N_EXPERTS + MOE_GROUPS)
    gl = jnp.where(is_g, logit, -jnp.inf)
    gmax = jnp.max(gl, axis=-1, keepdims=True)
    g_idx = jnp.min(jnp.where(gl == gmax, lane, big), axis=-1, keepdims=True) - N_EXPERTS
    g_w = 1.0 / jnp.sum(jnp.where(is_g, jnp.exp(logit - gmax), 0.0), axis=-1, keepdims=True)
    lo = g_idx * EXPERTS_PER_GROUP
    el = jnp.where((lane >= lo) & (lane < lo + EXPERTS_PER_GROUP), logit, -jnp.inf)
    t1 = jnp.max(el, axis=-1, keepdims=True)
    i1 = jnp.min(jnp.where(el == t1, lane, big), axis=-1, keepdims=True)
    el2 = jnp.where(lane == i1, -jnp.inf, el)
    t2 = jnp.max(el2, axis=-1, keepdims=True)
    i2 = jnp.min(jnp.where(el2 == t2, lane, big), axis=-1, keepdims=True)
    d = jnp.exp(t2 - t1)
    cw1 = g_w * (1.0 / (1.0 + d))
    cw2 = g_w * (d / (1.0 + d))

    oh1 = lane == i1
    oh2 = lane == i2
    picked = jnp.where(oh1, 1, jnp.where(oh2, 2, 0))
    hit = jnp.minimum(picked, 1)
    row_i = lax.broadcasted_iota(jnp.int32, (tm, LANES), 0)
    seen = hit
    shift = 1
    while shift < tm:
        seen = seen + jnp.where(row_i >= shift, pltpu.roll(seen, shift, axis=0), 0)
        shift *= 2
    run = run_ref[0:1, :]
    code_ref[...] = jnp.where(hit > 0, picked + 4 * (seen - hit + run), 0)
    run_new = jnp.broadcast_to(run + seen[tm - 1:tm, :], run_ref.shape)
    run_ref[...] = run_new
    cnt_ref[...] = run_new

    route_ref[...] = jnp.where(lane_i == 0, cw1, jnp.where(lane_i == 1, cw2, 0.0))


def _stage2(attn_in, ob, ga, gb, x2d, gt1, sh2, sc2, wua, wub, wo, g2, wr_hi, wr_lo, br, *, tm, mod_index, stride_index,
            merged, name):
    n = x2d.shape[0]
    row = lambda i: (i, 0)
    const2 = lambda i: (0, 0)
    mod_block = (1,) + gt1.shape[1:]
    scratch = [pltpu.VMEM((8, LANES), jnp.int32)]
    if merged:
        attn_specs = [pl.BlockSpec((tm, D_A), row)]
    else:
        attn_specs = [pl.BlockSpec((1, dil, tm // dil, D_A), stride_index) for _, dil in DILATED_GROUPS]
        attn_specs += [pl.BlockSpec((1, dil, tm // dil, D_A), stride_index) for _, dil in DILATED_GROUPS]
        scratch += [pltpu.VMEM((D_A // LANES, tm, LANES), F32), pltpu.VMEM((D_A // LANES, tm, LANES), F32)]
    in_specs = attn_specs + [
        pl.BlockSpec((tm, D_B), row), pl.BlockSpec((tm, D_MODEL), row), pl.BlockSpec((tm, D_MODEL), row),
        pl.BlockSpec((tm, D_MODEL), row),
        pl.BlockSpec(mod_block, mod_index), pl.BlockSpec(mod_block, mod_index), pl.BlockSpec(mod_block, mod_index),
        pl.BlockSpec((D_A, D_MODEL), const2), pl.BlockSpec((D_B, D_MODEL), const2),
        pl.BlockSpec((D_MODEL, D_MODEL), const2), pl.BlockSpec((1, D_MODEL), const2),
        pl.BlockSpec((D_MODEL, LANES), const2), pl.BlockSpec((D_MODEL, LANES), const2),
        pl.BlockSpec((1, LANES), const2),
    ]
    out_shape = [jax.ShapeDtypeStruct((n, D_MODEL), F32), jax.ShapeDtypeStruct((n, D_MODEL), F32),
                 jax.ShapeDtypeStruct((n, LANES), F32), jax.ShapeDtypeStruct((n, LANES), jnp.int32),
                 jax.ShapeDtypeStruct((8, LANES), jnp.int32)]
    out_specs = [pl.BlockSpec((tm, D_MODEL), row), pl.BlockSpec((tm, D_MODEL), row),
                 pl.BlockSpec((tm, LANES), row), pl.BlockSpec((tm, LANES), row), pl.BlockSpec((8, LANES), const2)]
    return pl.pallas_call(
        functools.partial(_stage2_body, merged=merged),
        out_shape=out_shape, grid=(n // tm,), in_specs=in_specs, out_specs=out_specs,
        scratch_shapes=scratch,
        compiler_params=_params(1), name=name,
    )(*attn_in, ob, ga, gb, x2d, gt1, sh2, sc2, wua, wub, wo, g2, wr_hi, wr_lo, br)


def _dispatch_body(cnt_ref, off_ref, pad_ref, nv_ref, posp_ref, poss_ref, hp_ref, hs_ref, xs_ref,
                   zero_ref, buf_ref, load_sem, scat_sem, zero_sem, *, tiles_p, n_tiles):
    step = pl.program_id(0)
    tile = zero_ref.shape[0]
    tm = buf_ref.shape[1]
    n_s = hs_ref.shape[0]
    slot = lax.rem(step, 3)
    ahead = lax.rem(step + 1, 3)

    def load_prompt(j, s):
        return pltpu.make_async_copy(hp_ref.at[pl.ds(j * tm, tm)], buf_ref.at[s], load_sem.at[s])

    def load_sample(s):
        return pltpu.make_async_copy(hs_ref, buf_ref.at[s, pl.ds(0, n_s)], load_sem.at[s])

    def scatter(rows, pos_ref, s):
        def issue(t, c):
            _row_copy(buf_ref.at[s], t, xs_ref, pos_ref[0, 0, 2 * t], scat_sem.at[s]).start()
            _row_copy(buf_ref.at[s], t, xs_ref, pos_ref[0, 0, 2 * t + 1], scat_sem.at[s]).start(priority=1)
            return c

        lax.fori_loop(0, rows, issue, 0, unroll=8)

    def drain(rows, s):
        def one(t, c):
            _row_copy(buf_ref.at[s], 0, xs_ref, 0, scat_sem.at[s]).wait()
            _row_copy(buf_ref.at[s], 0, xs_ref, 0, scat_sem.at[s]).wait()
            return c

        lax.fori_loop(0, rows, one, 0, unroll=8)

    @pl.when(step == 0)
    def _():
        zero_ref[...] = jnp.zeros_like(zero_ref)
        load_prompt(0, 0).start()
        bits = int(math.log2(tile))

        def pad_copies(e, wait):
            n = pad_ref[e] - cnt_ref[e]
            base = off_ref[e] + cnt_ref[e]
            end = off_ref[e] + pad_ref[e]

            def one_row(i, c):
                copy = _row_copy(zero_ref, 0, xs_ref, base + i, zero_sem)
                if wait:
                    copy.wait()
                else:
                    copy.start()
                return c

            lax.fori_loop(0, jnp.bitwise_and(n, 7), one_row, 0)
            for b in range(3, bits):
                size = 1 << b
                first = pl.multiple_of(end - jnp.bitwise_and(n, ~(2 * size - 1)) - size, 8)
                copy = pltpu.make_async_copy(zero_ref.at[pl.ds(0, size)], xs_ref.at[pl.ds(first, size)], zero_sem)

                @pl.when(jnp.bitwise_and(n, size) != 0)
                def _():
                    if wait:
                        copy.wait()
                    else:
                        copy.start()

        def start_pads(e, c):
            pad_copies(e, False)
            return c

        def wait_pads(e, c):
            pad_copies(e, True)
            return c

        def tile_copy(j):
            return pltpu.make_async_copy(zero_ref, xs_ref.at[pl.ds(j * tile, tile)], zero_sem)

        def start_tile(j, c):
            tile_copy(j).start()
            return c

        def wait_tile(j, c):
            tile_copy(j).wait()
            return c

        lax.fori_loop(0, N_EXPERTS, start_pads, 0)
        lax.fori_loop(nv_ref[0], n_tiles, start_tile, 0)
        lax.fori_loop(0, N_EXPERTS, wait_pads, 0)
        lax.fori_loop(nv_ref[0], n_tiles, wait_tile, 0)

    @pl.when(step >= 2)
    def _():
        drain(tm, ahead)

    @pl.when(step + 1 < tiles_p)
    def _():
        load_prompt(step + 1, ahead).start()

    @pl.when(step + 1 == tiles_p)
    def _():
        load_sample(ahead).start()

    @pl.when(step < tiles_p)
    def _():
        load_prompt(step, slot).wait()
        scatter(tm, posp_ref, slot)

    @pl.when(step == tiles_p)
    def _():
        load_sample(slot).wait()
        scatter(n_s, poss_ref, slot)
        drain(tm, lax.rem(step + 2, 3))
        drain(n_s, slot)


def _dispatch(count, offset, padded, n_valid, pos_p, pos_s, h2_p, h2_s, *, cap, tm, tile):
    n_p, n_s = h2_p.shape[0], h2_s.shape[0]
    tiles_p = n_p // tm
    assert tiles_p >= 2 and n_s <= tm
    last_p = lambda i, *_: (jnp.minimum(i, tiles_p - 1), 0, 0)
    in_specs = [pl.BlockSpec((1, 1, 2 * tm), last_p, memory_space=pltpu.SMEM),
                pl.BlockSpec((1, 1, 2 * n_s), lambda i, *_: (0, 0, 0), memory_space=pltpu.SMEM),
                pl.BlockSpec(memory_space=pl.ANY),
                pl.BlockSpec(memory_space=pl.ANY)]
    grid_spec = pltpu.PrefetchScalarGridSpec(
        num_scalar_prefetch=4, grid=(tiles_p + 1,), in_specs=in_specs,
        out_specs=pl.BlockSpec(memory_space=pl.ANY),
        scratch_shapes=[pltpu.VMEM((tile, D_MODEL), F32), pltpu.VMEM((3, tm, D_MODEL), F32),
                        pltpu.SemaphoreType.DMA((3,)), pltpu.SemaphoreType.DMA((3,)), pltpu.SemaphoreType.DMA(())],
    )
    return pl.pallas_call(
        functools.partial(_dispatch_body, tiles_p=tiles_p, n_tiles=cap // tile),
        out_shape=jax.ShapeDtypeStruct((cap, D_MODEL), F32), grid_spec=grid_spec,
        compiler_params=pltpu.CompilerParams(dimension_semantics=("arbitrary",), vmem_limit_bytes=VMEM_LIMIT,
                                             has_side_effects=True),
        name="dispatch",
    )(count, offset, padded, n_valid, pos_p.reshape(tiles_p, 1, 2 * tm), pos_s.reshape(1, 1, 2 * n_s), h2_p, h2_s)


def _moe_body(te_ref, nv_ref, x_ref, wg_ref, wu_ref, wd_ref, y_ref, wg_s, wu_s, wd_s):
    i = pl.program_id(0)
    active = i < nv_ref[0]
    fresh = (i == 0) | (te_ref[i] != te_ref[jnp.maximum(i - 1, 0)])

    @pl.when(active & fresh)
    def _():
        wg_s[...] = wg_ref[0].astype(BF16)
        wu_s[...] = wu_ref[0].astype(BF16)
        wd_s[...] = wd_ref[0].astype(BF16)

    @pl.when(active)
    def _():
        x = x_ref[...].astype(BF16)
        a = _dot(x, wg_s[...])
        b = _dot(x, wu_s[...])
        mid = (a * _sigmoid(a) * b).astype(BF16)
        y_ref[...] = _dot(mid, wd_s[...])

    @pl.when(jnp.logical_not(active))
    def _():
        y_ref[...] = jnp.zeros_like(y_ref)


def _moe(tile_expert, n_valid, x_sorted, w_eg, w_eu, w_ed, tm):
    cap = x_sorted.shape[0]
    last = lambda i, nv: jnp.maximum(jnp.minimum(i, nv[0] - 1), 0)
    xmap = lambda i, te, nv: (last(i, nv), 0)
    wmap = lambda i, te, nv: (te[last(i, nv)], 0, 0)
    grid_spec = pltpu.PrefetchScalarGridSpec(
        num_scalar_prefetch=2,
        grid=(cap // tm,),
        in_specs=[pl.BlockSpec((tm, D_MODEL), xmap),
                  pl.BlockSpec((1, D_MODEL, D_EXPERT), wmap),
                  pl.BlockSpec((1, D_MODEL, D_EXPERT), wmap),
                  pl.BlockSpec((1, D_EXPERT, D_MODEL), wmap)],
        out_specs=pl.BlockSpec((tm, D_MODEL), lambda i, te, nv: (i, 0)),
        scratch_shapes=[pltpu.VMEM((D_MODEL, D_EXPERT), BF16), pltpu.VMEM((D_MODEL, D_EXPERT), BF16),
                        pltpu.VMEM((D_EXPERT, D_MODEL), BF16)],
    )
    return pl.pallas_call(
        _moe_body, out_shape=jax.ShapeDtypeStruct((cap, D_MODEL), F32), grid_spec=grid_spec,
        compiler_params=_params(1), name="moe",
    )(tile_expert, n_valid, x_sorted, w_eg, w_eu, w_ed)


def _final_body(pos_ref, next_ref, x1_ref, route_ref, gt2_ref, gf_ref, ys_ref, o_ref, y_buf, sem):
    step = pl.program_id(0)
    tm = x1_ref.shape[0]
    slot = lax.rem(step, 2)

    def gather(p_ref, s):
        def issue(t, c):
            _row_copy(ys_ref, p_ref[0, 0, 2 * t], y_buf.at[s, 0], t, sem.at[s]).start()
            _row_copy(ys_ref, p_ref[0, 0, 2 * t + 1], y_buf.at[s, 1], t, sem.at[s]).start(priority=1)
            return c

        lax.fori_loop(0, tm, issue, 0, unroll=8)

    @pl.when(step == 0)
    def _():
        gather(pos_ref, 0)

    @pl.when(step + 1 < pl.num_programs(0))
    def _():
        gather(next_ref, 1 - slot)

    def drain(t, c):
        _row_copy(ys_ref, 0, y_buf.at[slot, 0], 0, sem.at[slot]).wait()
        _row_copy(ys_ref, 0, y_buf.at[slot, 1], 0, sem.at[slot]).wait()
        return c

    lax.fori_loop(0, tm, drain, 0, unroll=8)

    route = route_ref[...]
    moe = route[:, 0:1] * y_buf[slot, 0] + route[:, 1:2] * y_buf[slot, 1]
    x = x1_ref[...] + gt2_ref[0] * moe
    o_ref[...] = x * lax.rsqrt(jnp.mean(x * x, axis=-1, keepdims=True) + EPS) * gf_ref[...]


def _final(pos, x1, route, gt2, gf, y_sorted, *, tm, mod_index, name):
    n = x1.shape[0]
    steps = n // tm
    row = lambda i: (i, 0)
    mod_block = (1,) + gt2.shape[1:]
    pos3 = pos.reshape(steps, 1, 2 * tm)
    return pl.pallas_call(
        _final_body,
        out_shape=jax.ShapeDtypeStruct((n, D_MODEL), F32),
        grid=(steps,),
        in_specs=[pl.BlockSpec((1, 1, 2 * tm), lambda i: (i, 0, 0), memory_space=pltpu.SMEM),
                  pl.BlockSpec((1, 1, 2 * tm), lambda i: (jnp.minimum(i + 1, steps - 1), 0, 0),
                               memory_space=pltpu.SMEM),
                  pl.BlockSpec((tm, D_MODEL), row), pl.BlockSpec((tm, LANES), row),
                  pl.BlockSpec(mod_block, mod_index), pl.BlockSpec((1, D_MODEL), lambda i: (0, 0)),
                  pl.BlockSpec(memory_space=pl.ANY)],
        out_specs=pl.BlockSpec((tm, D_MODEL), row),
        scratch_shapes=[pltpu.VMEM((2, 2, tm, D_MODEL), F32), pltpu.SemaphoreType.DMA((2,))],
        compiler_params=_params(1), name=name,
    )(pos3, pos3, x1, route, gt2, gf, y_sorted)


def _t5_bucket(dist):
    max_exact = N_BUCKETS // 2
    dist = np.asarray(dist)
    d = np.maximum(dist, 1).astype(np.float32)
    large = max_exact + (np.log(d / np.float32(max_exact)) / np.float32(math.log(MAX_DISTANCE / max_exact))
                         * np.float32(N_BUCKETS - max_exact)).astype(np.int32)
    return np.where(dist < max_exact, dist, np.minimum(large, N_BUCKETS - 1)).astype(np.int32)


def _bias_lookup(tab, dist):
    onehot = (jnp.asarray(_t5_bucket(dist))[..., None] == jnp.arange(N_BUCKETS)).astype(F32)
    return jnp.einsum("...b,bh->...h", onehot, tab, precision=HIGHEST)


def _prompt_bias(tab, dil):
    delta = np.arange(BLOCK)[:, None] + BLOCK - np.arange(2 * BLOCK)[None, :]
    return jnp.transpose(_bias_lookup(tab, np.maximum(delta, 0) * dil), (2, 0, 1))


def _sample_bias(tab, win, dil, t_new):
    cols = win + LANES
    back = win + np.arange(t_new)[:, None] - np.arange(cols)[None, :]
    bias = _bias_lookup(tab, np.maximum(back, 0))
    return jnp.transpose(bias, (2, 0, 1)).reshape(HEADS * t_new, cols)


def kernel(x_prompt, x_sample, c_prompt, c_sample, cache_kv_w128, cache_kv_w512, cache_kv_w2048, rel_bias, w_ada,
           b_ada, g_norm1, w_in, ln_v_g, ln_v_b, w_spatial, b_spatial, w_up_a, w_up_b, w_out, g_norm2,
           w_route_group, b_route_group, w_route_expert, b_route_expert, w_e_gate, w_e_up, w_e_down, g_final):
    assert w_ada.shape[0] == 1, "single layer"
    bp, s_len, _ = x_prompt.shape
    bs, t_new, _ = x_sample.shape
    n_p, n_s = bp * s_len, bs * t_new
    caches = (cache_kv_w128[0], cache_kv_w512[0], cache_kv_w2048[0])
    assert all(c.shape[1] == win for c, (win, _) in zip(caches, DILATED_GROUPS)), "cache holds one full window"

    mod = _ada(jnp.concatenate([c_prompt, c_sample], axis=0), w_ada[0], b_ada[0])
    mod_p = [mod[:bp, j * D_MODEL:(j + 1) * D_MODEL].reshape(bp, 1, D_MODEL) for j in range(6)]
    mod_s = [jnp.repeat(mod[bp:, j * D_MODEL:(j + 1) * D_MODEL], t_new, axis=0).reshape(1, n_s, D_MODEL)
             for j in range(6)]

    w_in_bf = w_in[0].astype(BF16)
    g1 = g_norm1[0].reshape(1, D_MODEL)
    lng, lnb = ln_v_g[0].reshape(1, D_B), ln_v_b[0].reshape(1, D_B)
    ws_p = w_spatial[0]
    bs_p = jnp.repeat(jnp.transpose(b_spatial[0]), SGU_DIM, axis=1)
    reps = n_s // t_new
    ws_s = jnp.tile(w_spatial[0][:, :t_new, :t_new], (1, reps, reps))
    bs_s = jnp.tile(bs_p[:t_new], (reps, 1))

    tpb = s_len // TM_STAGE1
    tail_p = [(bp, min(win, s_len)) for win, _ in DILATED_GROUPS]

    def tail_index_p(i, g, rows):
        first = tpb - tail_p[g][1] // rows
        return (i // tpb, jnp.maximum(i % tpb - first, 0), 0)

    q_s, kv0_s, kv1_s, kv2_s, ob_s, ga_s, gb_s, vn_s = _stage1(
        x_sample.reshape(n_s, D_MODEL), mod_s[0], mod_s[1], g1, w_in_bf, lng, lnb, ws_s, bs_s,
        tm=n_s, mod_index=lambda i: (0, 0, 0), stride_index=None, tail_shapes=[(1, n_s)] * N_DIL,
        tail_index=lambda i, g, rows: (0, 0, 0),
        period=t_new, by_stride=False, emit_vn=True, name="stage1_sample")

    tabs = [rel_bias[:, g * HEADS:(g + 1) * HEADS].astype(F32) for g in range(N_DIL)]
    sample_bias = [_sample_bias(tabs[g], win, dil, t_new) for g, (win, dil) in enumerate(DILATED_GROUPS)]
    caches_t = [jnp.transpose(c, (0, 2, 3, 4, 1)).reshape(bs, 2, D_A, c.shape[1]) for c in caches]
    outs = _stage1(
        x_prompt.reshape(n_p, D_MODEL), mod_p[0], mod_p[1], g1, w_in_bf, lng, lnb, ws_p, bs_p,
        tm=TM_STAGE1, mod_index=lambda i: (i // tpb, 0, 0),
        stride_index=(bp, s_len, lambda i: (i // tpb, 0, i % tpb, 0)), tail_shapes=tail_p, tail_index=tail_index_p,
        period=CHUNK, by_stride=True, emit_vn=False, name="stage1_prompt",
        cache=(q_s, (kv0_s, kv1_s, kv2_s), caches_t, sample_bias, t_new))
    q_g, k_g, v_g = outs[0:3], outs[3:6], outs[6:9]
    kv0_p, kv1_p, kv2_p, ob_p, ga_p, gb_p, cache0_t, cache1_t, cache2_t, oa_s = outs[9:]

    o_groups, lse_groups = [], []
    for g, (win, dil) in enumerate(DILATED_GROUPS):
        o, lse = _attn_prompt_group(q_g[g], k_g[g], v_g[g], _prompt_bias(tabs[g], dil), dil, ATTN_BLOCKS_PER_STEP)
        o_groups.append(o)
        lse_groups.append(lse)

    wua, wub, wo = w_up_a[0].astype(BF16), w_up_b[0].astype(BF16), w_out[0].astype(BF16)
    g2 = g_norm2[0].reshape(1, D_MODEL)
    wr = jnp.concatenate([w_route_expert[0].reshape(D_MODEL, N_EXPERTS), w_route_group[0]], axis=1)
    wr = jnp.pad(wr, ((0, 0), (0, LANES - wr.shape[1])))
    br = jnp.concatenate([b_route_expert[0].reshape(N_EXPERTS), b_route_group[0]])
    br = jnp.pad(br, (0, LANES - br.shape[0])).reshape(1, LANES)
    wr_hi = wr.astype(BF16)
    wr_lo = (wr - wr_hi.astype(F32)).astype(BF16)

    tpb2 = s_len // TM_STAGE2
    x1_p, h2_p, route_p, code_p, cnt_p = _stage2(
        o_groups + lse_groups, ob_p, ga_p, gb_p, x_prompt.reshape(n_p, D_MODEL),
        mod_p[2], mod_p[3], mod_p[4], wua, wub, wo, g2, wr_hi, wr_lo, br,
        tm=TM_STAGE2, mod_index=lambda i: (i // tpb2, 0, 0), stride_index=lambda i: (i // tpb2, 0, i % tpb2, 0),
        merged=False, name="stage2_prompt")
    x1_s, h2_s, route_s, code_s, cnt_s = _stage2(
        [oa_s], ob_s, ga_s, gb_s, x_sample.reshape(n_s, D_MODEL),
        mod_s[2], mod_s[3], mod_s[4], wua, wub, wo, g2, wr_hi, wr_lo, br,
        tm=n_s, mod_index=lambda i: (0, 0, 0), stride_index=None, merged=True, name="stage2_sample")

    cap = 2 * (n_p + n_s) + N_EXPERTS * TM_MOE
    n_tiles = cap // TM_MOE
    count_p = cnt_p[0, :N_EXPERTS]
    count = count_p + cnt_s[0, :N_EXPERTS]
    padded = ((count + TM_MOE - 1) // TM_MOE) * TM_MOE
    ends = jnp.cumsum(padded)
    offset = ends - padded
    n_valid = (ends[-1] // TM_MOE).astype(jnp.int32).reshape(1)
    tile_start = jnp.arange(n_tiles, dtype=jnp.int32) * TM_MOE
    tile_expert = jnp.minimum(jnp.sum(ends[None, :] <= tile_start[:, None], axis=1), N_EXPERTS - 1).astype(jnp.int32)

    def sorted_rows(code, earlier):
        row = jnp.pad(offset + earlier, (0, LANES - N_EXPERTS))[None, :] + jnp.right_shift(code, 2)
        return jnp.stack([jnp.sum(jnp.where((code & 3) == k, row, 0), axis=1) for k in (1, 2)], axis=1)

    pos_p = sorted_rows(code_p, jnp.zeros_like(count_p))
    pos_s = sorted_rows(code_s, count_p)
    x_sorted = _dispatch(count, offset, padded, n_valid, pos_p, pos_s, h2_p, h2_s, cap=cap, tm=TM_FINAL, tile=TM_MOE)

    y_sorted = _moe(tile_expert, n_valid, x_sorted, w_e_gate[0], w_e_up[0], w_e_down[0], TM_MOE)

    gf = g_final.reshape(1, D_MODEL)
    tpb3 = s_len // TM_FINAL
    y_p = _final(pos_p, x1_p, route_p, mod_p[5], gf, y_sorted,
                 tm=TM_FINAL, mod_index=lambda i: (i // tpb3, 0, 0), name="final_prompt")
    y_s = _final(pos_s, x1_s, route_s, mod_s[5], gf, y_sorted,
                 tm=n_s, mod_index=lambda i: (0, 0, 0), name="final_sample")

    kv_p = [a.reshape(1, bp, a.shape[1], 2, HEADS, HEAD_DIM) for a in (kv0_p, kv1_p, kv2_p)]
    kv_s = [jnp.transpose(c.reshape(bs, 2, HEADS, HEAD_DIM, c.shape[-1]), (0, 4, 1, 2, 3))[None]
            for c in (cache0_t, cache1_t, cache2_t)]
    return (y_p.reshape(bp, s_len, D_MODEL), y_s.reshape(bs, t_new, D_MODEL),
            kv_p[0], kv_p[1], kv_p[2], kv_s[0], kv_s[1], kv_s[2],
            vn_s.reshape(1, bs, t_new, D_B))
```

```python
import functools
import math

import numpy as np
import jax
import jax.numpy as jnp
from jax import lax
from jax.experimental import pallas as pl
from jax.experimental.pallas import tpu as pltpu

F32 = jnp.float32
BF16 = jnp.bfloat16
HIGHEST = lax.Precision.HIGHEST

D_MODEL = 1024
HEAD_DIM = 64
HEADS = 8
DILATED_GROUPS = ((128, 1), (512, 4), (2048, 16))
N_DIL = 3
D_A = HEADS * HEAD_DIM
D_QKV = N_DIL * D_A
BLOCK = 128
STEPS = 128
SCALE = HEAD_DIM ** -0.5
CHUNK = 128
D_B = 512
SGU_GROUPS = 4
SGU_DIM = D_B // SGU_GROUPS
N_BUCKETS = 32
MAX_DISTANCE = 2048
MOE_GROUPS = 4
EXPERTS_PER_GROUP = 8
N_EXPERTS = 32
D_EXPERT = 512
EPS = 1e-6
NEG_INF = -1e30
LANES = 128
ROW_TILE = 8
C_U = 3 * D_QKV
C_V = C_U + D_B
C_GA = C_V + D_B
C_GB = C_GA + D_MODEL
IN_COLS = C_GB + D_MODEL

TM_STAGE1 = 256
TM_STAGE2 = 512
TM_MOE = 512
ATTN_BLOCKS_PER_STEP = 4
TM_FINAL = 512
VMEM_LIMIT = 56 * 1024 * 1024


def _dot(a, b):
    return jnp.dot(a, b, preferred_element_type=F32)


def _dot_nt(a, b):
    return lax.dot_general(a, b, (((1,), (1,)), ((), ())), preferred_element_type=F32)


def _sigmoid(x):
    return 1.0 / (1.0 + jnp.exp(-x))


def _gelu(x):
    return 0.5 * x * (1.0 + lax.erf(x * (2.0 ** -0.5)))


def _params(n_grid):
    return pltpu.CompilerParams(dimension_semantics=("arbitrary",) * n_grid, vmem_limit_bytes=VMEM_LIMIT)


def _mod_spec(vectors_col, batch_index):
    vectors, col = vectors_col
    return pl.BlockSpec((1, vectors.shape[1], D_MODEL), lambda i: batch_index(i)[:2] + (col,))


def _row_copy(src, src_row, dst, dst_row, sem):
    return pltpu.make_async_copy(src.at[pl.ds(src_row, 1)], dst.at[pl.ds(dst_row, 1)], sem)


def _ada_body(c_ref, w_ref, b_ref, o_ref):
    c = c_ref[...]
    s = c * _sigmoid(c)
    o_ref[...] = jnp.dot(s, w_ref[...], precision=HIGHEST, preferred_element_type=F32) + b_ref[...]


def _ada(c_all, w_ada, b_ada):
    n, d = c_all.shape
    cols = w_ada.shape[1]
    tn = 1024
    return pl.pallas_call(
        _ada_body,
        out_shape=jax.ShapeDtypeStruct((n, cols), F32),
        grid=(cols // tn,),
        in_specs=[pl.BlockSpec((n, d), lambda j: (0, 0)),
                  pl.BlockSpec((d, tn), lambda j: (0, j)),
                  pl.BlockSpec((1, tn), lambda j: (0, j))],
        out_specs=pl.BlockSpec((n, tn), lambda j: (0, j)),
        compiler_params=_params(1),
        name="ada",
    )(c_all, w_ada, b_ada.reshape(1, cols))


def _stage1_body(x_ref, sh_ref, sc_ref, g1_ref, w_ref, lng_ref, lnb_ref, ws_ref, bs_ref, *rest,
                 period, by_stride, emit_vn, cache_t_new):
    rest = list(rest)
    cache_refs = [rest.pop(0) for _ in range(1 + 3 * N_DIL)] if cache_t_new else []
    if by_stride:
        qkv_refs = [[rest.pop(0) for _ in range(N_DIL)] for _ in range(3)]
    else:
        q_ref = rest.pop(0)
    kv_refs = [rest.pop(0) for _ in range(N_DIL)]
    ob_ref, ga_ref, gb_ref = rest.pop(0), rest.pop(0), rest.pop(0)
    vn_ref = rest.pop(0) if emit_vn else None
    if cache_t_new:
        cache_refs += [rest.pop(0) for _ in range(N_DIL + 1)]
    stage_ref = rest.pop(0) if by_stride else None
    if cache_t_new:
        _sample_cache_step(pl.program_id(0), cache_refs + rest, cache_t_new)

    x = x_ref[...]
    tm = x.shape[0]
    h = x * lax.rsqrt(jnp.mean(x * x, axis=-1, keepdims=True) + EPS) * g1_ref[...]
    h = h * (1.0 + sc_ref[0]) + sh_ref[0]
    hb = h.astype(BF16)

    q = _dot(hb, w_ref[:, 0:D_QKV]) * SCALE
    k = _dot(hb, w_ref[:, D_QKV:2 * D_QKV])
    v = _dot(hb, w_ref[:, 2 * D_QKV:3 * D_QKV])
    if by_stride:
        per_group = D_A // LANES
        for val, refs in zip((q, k, v), qkv_refs):
            for c in range(D_QKV // LANES):
                stage_ref[c] = val[:, c * LANES:(c + 1) * LANES]
            for g, (_, dil) in enumerate(DILATED_GROUPS):
                for r in range(dil):
                    for c in range(per_group):
                        piece = stage_ref[g * per_group + c, pl.ds(r, tm // dil, stride=dil), :]
                        refs[g][0, r, :, c * LANES:(c + 1) * LANES] = piece.astype(BF16)
    else:
        q_ref[...] = q
    for g in range(N_DIL):
        rows = kv_refs[g].shape[1]
        kv_refs[g][0, :, 0:D_A] = k[tm - rows:, g * D_A:(g + 1) * D_A]
        kv_refs[g][0, :, D_A:2 * D_A] = v[tm - rows:, g * D_A:(g + 1) * D_A]

    ga_ref[...] = _sigmoid(_dot(hb, w_ref[:, C_GA:C_GB])).astype(BF16)
    gb_ref[...] = _sigmoid(_dot(hb, w_ref[:, C_GB:IN_COLS])).astype(BF16)

    u = _gelu(_dot(hb, w_ref[:, C_U:C_V]))
    vb = _gelu(_dot(hb, w_ref[:, C_V:C_GA]))
    xc = vb - jnp.mean(vb, axis=-1, keepdims=True)
    vn = xc * lax.rsqrt(jnp.mean(xc * xc, axis=-1, keepdims=True) + EPS) * lng_ref[...] + lnb_ref[...]
    if emit_vn:
        vn_ref[...] = vn
    vnb = vn.astype(BF16)

    r = ws_ref.shape[1]
    ri = lax.broadcasted_iota(jnp.int32, (r, r), 0)
    ci = lax.broadcasted_iota(jnp.int32, (r, r), 1)
    keep = ci <= ri
    if period != r:
        sh = int(math.log2(period))
        keep = keep & (jnp.right_shift(ri, sh) == jnp.right_shift(ci, sh))
    for g in range(SGU_GROUPS):
        wg = jnp.where(keep, ws_ref[g], 0.0).astype(BF16)
        cs = slice(g * SGU_DIM, (g + 1) * SGU_DIM)
        for c in range(tm // r):
            rs = slice(c * r, (c + 1) * r)
            mixed = _dot(wg, vnb[rs, cs]) + bs_ref[:, cs]
            ob_ref[rs, cs] = (u[rs, cs] * mixed).astype(BF16)


def _stage1(x2d, sh, sc, g1, w_in_bf, lng, lnb, ws, bs, *, tm, mod_index, stride_index, tail_shapes, tail_index,
            period, by_stride, emit_vn, name, cache=None):
    n = x2d.shape[0]
    grid = (n // tm,)
    r = ws.shape[1]
    const2 = lambda i: (0, 0)
    row = lambda i: (i, 0)
    in_specs = [
        pl.BlockSpec((tm, D_MODEL), row),
        _mod_spec(sh, mod_index),
        _mod_spec(sc, mod_index),
        pl.BlockSpec((1, D_MODEL), const2),
        pl.BlockSpec((D_MODEL, IN_COLS), const2, pipeline_mode=pl.Buffered(1)),
        pl.BlockSpec((1, D_B), const2),
        pl.BlockSpec((1, D_B), const2),
        pl.BlockSpec((SGU_GROUPS, r, r), lambda i: (0, 0, 0)),
        pl.BlockSpec((r, D_B), const2),
    ]
    out_shape, out_specs, scratch = [], [], []
    if by_stride:
        n_batch, s_len = stride_index[0], stride_index[1]
        for _ in range(3):
            for _, dil in DILATED_GROUPS:
                out_shape.append(jax.ShapeDtypeStruct((n_batch, dil, s_len // dil, D_A), BF16))
                out_specs.append(pl.BlockSpec((1, dil, tm // dil, D_A), stride_index[2]))
        scratch.append(pltpu.VMEM((D_QKV // LANES, tm, LANES), F32))
    else:
        out_shape.append(jax.ShapeDtypeStruct((n, D_QKV), F32))
        out_specs.append(pl.BlockSpec((tm, D_QKV), row))
    for g in range(N_DIL):
        nb, keep = tail_shapes[g]
        rows = min(tm, keep)
        out_shape.append(jax.ShapeDtypeStruct((nb, keep, 2 * D_A), F32))
        out_specs.append(pl.BlockSpec((1, rows, 2 * D_A), functools.partial(tail_index, g=g, rows=rows)))
    out_shape += [jax.ShapeDtypeStruct((n, D_B), BF16),
                  jax.ShapeDtypeStruct((n, D_MODEL), BF16),
                  jax.ShapeDtypeStruct((n, D_MODEL), BF16)]
    out_specs += [pl.BlockSpec((tm, D_B), row), pl.BlockSpec((tm, D_MODEL), row), pl.BlockSpec((tm, D_MODEL), row)]
    if emit_vn:
        out_shape.append(jax.ShapeDtypeStruct((n, D_B), F32))
        out_specs.append(pl.BlockSpec((tm, D_B), row))
    args = [x2d, sh[0], sc[0], g1, w_in_bf, lng, lnb, ws, bs]
    cache_t_new = 0
    if cache is not None:
        c_args, c_in, c_shape, c_out, c_scratch, units = _sample_cache_specs(*cache)
        assert units == grid[0], "one sample-cache unit per projection tile"
        args += c_args
        in_specs += c_in
        out_shape += c_shape
        out_specs += c_out
        scratch += c_scratch
        cache_t_new = cache[-1]
    body = functools.partial(_stage1_body, period=period, by_stride=by_stride, emit_vn=emit_vn,
                             cache_t_new=cache_t_new)
    return pl.pallas_call(
        body, out_shape=out_shape, grid=grid, in_specs=in_specs, out_specs=out_specs, scratch_shapes=scratch,
        compiler_params=_params(1), name=name,
    )(*args)


def _attn_prompt_body(q_ref, kp_ref, kc_ref, vp_ref, vc_ref, bias_ref, o_ref, lse_ref):
    n = pl.program_id(2)
    n_sub = q_ref.shape[2] // BLOCK
    ri = lax.broadcasted_iota(jnp.int32, (BLOCK, 2 * BLOCK), 0)
    ci = lax.broadcasted_iota(jnp.int32, (BLOCK, 2 * BLOCK), 1)
    delta = ri + BLOCK - ci
    band = (delta >= 0) & (delta <= STEPS)
    first_key = jnp.where(n > 0, 0, BLOCK)
    low_q = lax.broadcasted_iota(jnp.int32, (BLOCK, LANES), 1) < HEAD_DIM
    low_v = lax.broadcasted_iota(jnp.int32, (2 * BLOCK, LANES), 1) < HEAD_DIM
    ones_lo = jnp.where(low_v, 1.0, 0.0).astype(BF16)
    ones_hi = jnp.where(low_v, 0.0, 1.0).astype(BF16)
    for sub in range(n_sub):
        rows = slice(sub * BLOCK, (sub + 1) * BLOCK)
        valid = band & (ci >= first_key) if sub == 0 else band
        for pair in range(HEADS // 2):
            cs = slice(pair * LANES, (pair + 1) * LANES)
            qp = q_ref[0, 0, rows, cs]
            if sub == 0:
                kp = jnp.concatenate([kp_ref[0, 0, :, cs], kc_ref[0, 0, 0:BLOCK, cs]], axis=0)
                vp = jnp.concatenate([vp_ref[0, 0, :, cs], vc_ref[0, 0, 0:BLOCK, cs]], axis=0)
            else:
                kp = kc_ref[0, 0, (sub - 1) * BLOCK:(sub + 1) * BLOCK, cs]
                vp = vc_ref[0, 0, (sub - 1) * BLOCK:(sub + 1) * BLOCK, cs]
            acc, tops = None, []
            for half in range(2):
                own_q = low_q if half == 0 else jnp.logical_not(low_q)
                own_v = low_v if half == 0 else jnp.logical_not(low_v)
                s = _dot_nt(jnp.where(own_q, qp, jnp.zeros_like(qp)), kp) + bias_ref[2 * pair + half]
                s = jnp.where(valid, s, NEG_INF)
                m = jnp.max(s, axis=-1, keepdims=True)
                p = jnp.exp(s - m).astype(BF16)
                w = jnp.concatenate([jnp.where(own_v, vp, jnp.zeros_like(vp)), ones_lo if half == 0 else ones_hi],
                                    axis=1)
                r = _dot(p, w)
                acc = r if acc is None else acc + r
                tops.append(m)
            den = acc[:, LANES:]
            o_ref[0, 0, rows, cs] = (acc[:, :LANES] / den).astype(BF16)
            lse_ref[0, 0, rows, cs] = jnp.where(low_q, tops[0], tops[1]) + jnp.log(den)


def _attn_prompt_group(q4, k4, v4, bias, dil, n_sub):
    b_sz, _, m_len, _ = q4.shape
    span = n_sub * BLOCK
    cur = pl.BlockSpec((1, 1, span, D_A), lambda b, r, n: (b, r, n, 0))
    prev = pl.BlockSpec((1, 1, BLOCK, D_A), lambda b, r, n: (b, r, jnp.maximum(n * n_sub - 1, 0), 0))
    return pl.pallas_call(
        _attn_prompt_body,
        out_shape=[jax.ShapeDtypeStruct((b_sz, dil, m_len, D_A), BF16),
                   jax.ShapeDtypeStruct((b_sz, dil, m_len, D_A), F32)],
        grid=(b_sz, dil, m_len // span),
        in_specs=[cur, prev, cur, prev, cur,
                  pl.BlockSpec((HEADS, BLOCK, 2 * BLOCK), lambda b, r, n: (0, 0, 0))],
        out_specs=[cur, cur],
        compiler_params=_params(3),
        name=f"attn_prompt_d{dil}",
    )(q4, k4, k4, v4, v4, bias)


def _sample_cache_step(unit, refs, t_new):
    refs = list(refs)
    q_ref = refs.pop(0)
    new_refs = [refs.pop(0) for _ in range(N_DIL)]
    cache_refs = [refs.pop(0) for _ in range(N_DIL)]
    bias_refs = [refs.pop(0) for _ in range(N_DIL)]
    out_refs = [refs.pop(0) for _ in range(N_DIL)]
    o_ref = refs.pop(0)
    p_refs = [refs.pop(0) for _ in range(N_DIL)]
    scale_ref = refs.pop(0)
    kv = lax.rem(unit, 2)
    rows = HEADS * t_new

    def allowed(g, col0, cols):
        win, dil = DILATED_GROUPS[g]
        tok = jnp.bitwise_and(lax.broadcasted_iota(jnp.int32, (rows, cols), 0), t_new - 1)
        col = lax.broadcasted_iota(jnp.int32, (rows, cols), 1) + col0
        back = win + tok - col
        return ((col < win + t_new) & (back >= 0) & (jnp.bitwise_and(back, dil - 1) == 0)
                & (back <= STEPS * dil))

    lane = lax.broadcasted_iota(jnp.int32, (D_A, LANES), 1)
    front = jnp.zeros((LANES - t_new, D_A), F32)
    for g in range(N_DIL):
        win = cache_refs[g].shape[-1]
        new = new_refs[g][0]
        half = jnp.where(kv == 0, new[:, 0:D_A], new[:, D_A:2 * D_A])
        tail = jnp.transpose(jnp.concatenate([front, half], axis=0))
        rolled = pltpu.roll(cache_refs[g][0, 0], win - t_new, axis=1)
        if win > LANES:
            out_refs[g][0, 0, :, 0:win - LANES] = rolled[:, 0:win - LANES]
        out_refs[g][0, 0, :, win - LANES:win] = jnp.where(lane >= LANES - t_new, tail, rolled[:, win - LANES:win])

    ri = lax.broadcasted_iota(jnp.int32, (rows, D_A), 0)
    ci = lax.broadcasted_iota(jnp.int32, (rows, D_A), 1)
    own_head = jnp.right_shift(ri, int(math.log2(t_new))) == jnp.right_shift(ci, int(math.log2(HEAD_DIM)))
    pad = jnp.zeros((LANES - t_new, D_A), F32)
    glane = lax.broadcasted_iota(jnp.int32, (rows, LANES), 1)

    @pl.when(kv == 0)
    def _():
        q = q_ref[...]
        stats = []
        for g in range(N_DIL):
            win = cache_refs[g].shape[-1]
            qg = q[:, g * D_A:(g + 1) * D_A]
            qblk = jnp.where(own_head, jnp.concatenate([qg] * HEADS, axis=0), 0.0).astype(BF16)
            s_c = _dot(qblk, cache_refs[g][0, 0].astype(BF16)) + bias_refs[g][:, 0:win]
            s_c = jnp.where(allowed(g, 0, win), s_c, NEG_INF)
            k_new = jnp.concatenate([new_refs[g][0][:, 0:D_A], pad], axis=0).astype(BF16)
            s_n = _dot_nt(qblk, k_new) + bias_refs[g][:, win:]
            s_n = jnp.where(allowed(g, win, LANES), s_n, NEG_INF)
            m = jnp.maximum(jnp.max(s_c, axis=-1, keepdims=True), jnp.max(s_n, axis=-1, keepdims=True))
            p_c = jnp.exp(s_c - m)
            p_n = jnp.exp(s_n - m)
            l = jnp.sum(p_c, axis=-1, keepdims=True) + jnp.sum(p_n, axis=-1, keepdims=True)
            p_refs[g][:, 0:win] = p_c.astype(BF16)
            p_refs[g][:, win:] = p_n.astype(BF16)
            stats.append((l, m + jnp.log(l)))
        top = jnp.maximum(jnp.maximum(stats[0][1], stats[1][1]), stats[2][1])
        es = [jnp.exp(lse - top) for _, lse in stats]
        den = es[0] + es[1] + es[2]
        scale = jnp.zeros((rows, LANES), F32)
        for g in range(N_DIL):
            scale = jnp.where(glane == g, es[g] / den / stats[g][0], scale)
        scale_ref[...] = scale

    @pl.when(kv == 1)
    def _():
        acc = jnp.zeros((rows, D_A), F32)
        for g in range(N_DIL):
            win = cache_refs[g].shape[-1]
            part = _dot_nt(p_refs[g][:, 0:win], cache_refs[g][0, 0].astype(BF16))
            v_new = jnp.concatenate([new_refs[g][0][:, D_A:2 * D_A], pad], axis=0).astype(BF16)
            part = part + _dot(p_refs[g][:, win:], v_new)
            acc = acc + scale_ref[:, g:g + 1] * part
        acc = jnp.where(own_head, acc, 0.0)
        out = acc[0:t_new]
        for h in range(1, HEADS):
            out = out + acc[h * t_new:(h + 1) * t_new]
        o_ref[...] = out


def _sample_cache_specs(q_s, kv_new, caches_t, biases, t_new):
    n = q_s.shape[0]
    nb = n // t_new
    blk4 = lambda i: (i // 2, lax.rem(i, 2), 0, 0)
    in_specs = [pl.BlockSpec((t_new, D_QKV), lambda i: (i // 2, 0))]
    in_specs += [pl.BlockSpec((1, t_new, 2 * D_A), lambda i: (i // 2, 0, 0))] * N_DIL
    in_specs += [pl.BlockSpec((1, 1, D_A, c.shape[-1]), blk4) for c in caches_t]
    in_specs += [pl.BlockSpec(a.shape, lambda i: (0, 0), pipeline_mode=pl.Buffered(1)) for a in biases]
    out_shape = [jax.ShapeDtypeStruct(c.shape, c.dtype) for c in caches_t]
    out_specs = [pl.BlockSpec((1, 1, D_A, c.shape[-1]), blk4) for c in caches_t]
    out_shape.append(jax.ShapeDtypeStruct((n, D_A), F32))
    out_specs.append(pl.BlockSpec((t_new, D_A), lambda i: (i // 2, 0)))
    rows = HEADS * t_new
    scratch = [pltpu.VMEM((rows, c.shape[-1] + LANES), BF16) for c in caches_t]
    scratch.append(pltpu.VMEM((rows, LANES), F32))
    args = [q_s, *[a.reshape(nb, t_new, 2 * D_A) for a in kv_new], *caches_t, *biases]
    return args, in_specs, out_shape, out_specs, scratch, 2 * nb


def _stage2_body(*refs, merged):
    refs = list(refs)
    if merged:
        oa_ref = refs.pop(0)
    else:
        o_refs = [refs.pop(0) for _ in range(N_DIL)]
        l_refs = [refs.pop(0) for _ in range(N_DIL)]
    (ob_ref, ga_ref, gb_ref, x_ref, gt1_ref, sh2_ref, sc2_ref, wua_ref, wub_ref, wo_ref, g2_ref,
     wrh_ref, wrl_ref, br_ref, x1_ref, h2_ref, route_ref, code_ref, cnt_ref, run_ref) = refs[:20]
    step = pl.program_id(0)
    tm = x_ref.shape[0]

    @pl.when(step == 0)
    def _():
        run_ref[...] = jnp.zeros_like(run_ref)

    if merged:
        oa = oa_ref[...].astype(BF16)
    else:
        o_stage, l_stage = refs[20], refs[21]

        def by_position(ref, stage, dil):
            if dil == 1:
                return ref[0, 0].astype(F32)
            slabs = ref.shape[-1] // LANES
            for r in range(dil):
                rows = ref[0, r].astype(F32)
                for c in range(slabs):
                    stage[c, pl.ds(r, tm // dil, stride=dil), :] = rows[:, c * LANES:(c + 1) * LANES]
            return jnp.concatenate([stage[c] for c in range(slabs)], axis=1)

        ls = [by_position(l_refs[g], l_stage, dil) for g, (_, dil) in enumerate(DILATED_GROUPS)]
        top = jnp.maximum(jnp.maximum(ls[0], ls[1]), ls[2])
        es = [jnp.exp(l - top) for l in ls]
        den = es[0] + es[1] + es[2]
        oa = None
        for g, (_, dil) in enumerate(DILATED_GROUPS):
            term = (es[g] / den) * by_position(o_refs[g], o_stage, dil)
            oa = term if oa is None else oa + term
        oa = oa.astype(BF16)

    ya = _dot(oa, wua_ref[...])
    yb = _dot(ob_ref[...], wub_ref[...])
    z = (ga_ref[...].astype(F32) * ya + gb_ref[...].astype(F32) * yb).astype(BF16)
    x1 = x_ref[...] + gt1_ref[0] * _dot(z, wo_ref[...])
    x1_ref[...] = x1
    h2 = x1 * lax.rsqrt(jnp.mean(x1 * x1, axis=-1, keepdims=True) + EPS) * g2_ref[...]
    h2 = h2 * (1.0 + sc2_ref[0]) + sh2_ref[0]
    h2_ref[...] = h2

    h_hi = h2.astype(BF16)
    h_lo = (h2 - h_hi.astype(F32)).astype(BF16)
    logit = (_dot(h_hi, wrh_ref[...]) + _dot(h_hi, wrl_ref[...]) + _dot(h_lo, wrh_ref[...])) + br_ref[...]
    lane_i = lax.broadcasted_iota(jnp.int32, (tm, LANES), 1)
    lane = lane_i.astype(F32)
    big = float(LANES)
    is_g = (lane_i >= N_EXPERTS) & (lane_i < N_EXPERTS + MOE_GROUPS)
    gl = jnp.where(is_g, logit, -jnp.inf)
    gmax = jnp.max(gl, axis=-1, keepdims=True)
    g_idx = jnp.min(jnp.where(gl == gmax, lane, big), axis=-1, keepdims=True) - N_EXPERTS
    g_w = 1.0 / jnp.sum(jnp.where(is_g, jnp.exp(logit - gmax), 0.0), axis=-1, keepdims=True)
    lo = g_idx * EXPERTS_PER_GROUP
    el = jnp.where((lane >= lo) & (lane < lo + EXPERTS_PER_GROUP), logit, -jnp.inf)
    t1 = jnp.max(el, axis=-1, keepdims=True)
    i1 = jnp.min(jnp.where(el == t1, lane, big), axis=-1, keepdims=True)
    el2 = jnp.where(lane == i1, -jnp.inf, el)
    t2 = jnp.max(el2, axis=-1, keepdims=True)
    i2 = jnp.min(jnp.where(el2 == t2, lane, big), axis=-1, keepdims=True)
    d = jnp.exp(t2 - t1)
    cw1 = g_w * (1.0 / (1.0 + d))
    cw2 = g_w * (d / (1.0 + d))

    oh1 = lane == i1
    oh2 = lane == i2
    picked = jnp.where(oh1, 1, jnp.where(oh2, 2, 0))
    hit = jnp.minimum(picked, 1)
    row_i = lax.broadcasted_iota(jnp.int32, (tm, LANES), 0)
    seen = hit
    shift = 1
    while shift < tm:
        seen = seen + jnp.where(row_i >= shift, pltpu.roll(seen, shift, axis=0), 0)
        shift *= 2
    run = run_ref[0:1, :]
    code_ref[...] = jnp.where(hit > 0, picked + 4 * (seen - hit + run), 0)
    run_new = jnp.broadcast_to(run + seen[tm - 1:tm, :], run_ref.shape)
    run_ref[...] = run_new
    cnt_ref[...] = run_new

    route_ref[...] = jnp.where(lane_i == 0, cw1, jnp.where(lane_i == 1, cw2, 0.0))


def _stage2(attn_in, ob, ga, gb, x2d, gt1, sh2, sc2, wua, wub, wo, g2, wr_hi, wr_lo, br, *, tm, mod_index, stride_index,
            merged, name):
    n = x2d.shape[0]
    row = lambda i: (i, 0)
    const2 = lambda i: (0, 0)
    scratch = [pltpu.VMEM((8, LANES), jnp.int32)]
    if merged:
        attn_specs = [pl.BlockSpec((tm, D_A), row)]
    else:
        attn_specs = [pl.BlockSpec((1, dil, tm // dil, D_A), stride_index) for _, dil in DILATED_GROUPS]
        attn_specs += [pl.BlockSpec((1, dil, tm // dil, D_A), stride_index) for _, dil in DILATED_GROUPS]
        scratch += [pltpu.VMEM((D_A // LANES, tm, LANES), F32), pltpu.VMEM((D_A // LANES, tm, LANES), F32)]
    in_specs = attn_specs + [
        pl.BlockSpec((tm, D_B), row), pl.BlockSpec((tm, D_MODEL), row), pl.BlockSpec((tm, D_MODEL), row),
        pl.BlockSpec((tm, D_MODEL), row),
        _mod_spec(gt1, mod_index), _mod_spec(sh2, mod_index), _mod_spec(sc2, mod_index),
        pl.BlockSpec((D_A, D_MODEL), const2), pl.BlockSpec((D_B, D_MODEL), const2),
        pl.BlockSpec((D_MODEL, D_MODEL), const2), pl.BlockSpec((1, D_MODEL), const2),
        pl.BlockSpec((D_MODEL, LANES), const2), pl.BlockSpec((D_MODEL, LANES), const2),
        pl.BlockSpec((1, LANES), const2),
    ]
    out_shape = [jax.ShapeDtypeStruct((n, D_MODEL), F32), jax.ShapeDtypeStruct((n, D_MODEL), F32),
                 jax.ShapeDtypeStruct((n, LANES), F32), jax.ShapeDtypeStruct((n, LANES), jnp.int32),
                 jax.ShapeDtypeStruct((8, LANES), jnp.int32)]
    out_specs = [pl.BlockSpec((tm, D_MODEL), row), pl.BlockSpec((tm, D_MODEL), row),
                 pl.BlockSpec((tm, LANES), row), pl.BlockSpec((tm, LANES), row), pl.BlockSpec((8, LANES), const2)]
    return pl.pallas_call(
        functools.partial(_stage2_body, merged=merged),
        out_shape=out_shape, grid=(n // tm,), in_specs=in_specs, out_specs=out_specs,
        scratch_shapes=scratch,
        compiler_params=_params(1), name=name,
    )(*attn_in, ob, ga, gb, x2d, gt1[0], sh2[0], sc2[0], wua, wub, wo, g2, wr_hi, wr_lo, br)


def _dispatch_body(cnt_ref, off_ref, pad_ref, nv_ref, posp_ref, poss_ref, hp_ref, hs_ref, xs_ref,
                   zero_ref, buf_ref, load_sem, scat_sem, zero_sem, *, tiles_p, n_tiles):
    step = pl.program_id(0)
    tile = zero_ref.shape[0]
    groups = buf_ref.shape[1]
    tm = groups * ROW_TILE
    groups_s = hs_ref.shape[0]
    n_s = groups_s * ROW_TILE
    slot = lax.rem(step, 3)
    ahead = lax.rem(step + 1, 3)

    def load_prompt(j, s):
        return pltpu.make_async_copy(hp_ref.at[pl.ds(j * groups, groups)], buf_ref.at[s], load_sem.at[s])

    def load_sample(s):
        return pltpu.make_async_copy(hs_ref, buf_ref.at[s, pl.ds(0, groups_s)], load_sem.at[s])

    def row_copy(s, group, j, dst_row):
        return pltpu.make_async_copy(buf_ref.at[s, group, pl.ds(j, 1)], xs_ref.at[pl.ds(dst_row, 1)], scat_sem.at[s])

    def scatter(rows, pos_ref, s):
        def issue(group, c):
            for j in range(ROW_TILE):
                t = group * ROW_TILE + j
                row_copy(s, group, j, pos_ref[0, 0, 2 * t]).start()
                row_copy(s, group, j, pos_ref[0, 0, 2 * t + 1]).start(priority=1)
            return c

        lax.fori_loop(0, rows // ROW_TILE, issue, 0)

    def drain(rows, s):
        def one(t, c):
            row_copy(s, 0, 0, 0).wait()
            row_copy(s, 0, 0, 0).wait()
            return c

        lax.fori_loop(0, rows, one, 0, unroll=8)

    @pl.when(step == 0)
    def _():
        zero_ref[...] = jnp.zeros_like(zero_ref)
        load_prompt(0, 0).start()
        bits = int(math.log2(tile))

        def pad_copies(e, wait):
            n = pad_ref[e] - cnt_ref[e]
            base = off_ref[e] + cnt_ref[e]
            end = off_ref[e] + pad_ref[e]

            def one_row(i, c):
                copy = _row_copy(zero_ref, 0, xs_ref, base + i, zero_sem)
                if wait:
                    copy.wait()
                else:
                    copy.start()
                return c

            lax.fori_loop(0, jnp.bitwise_and(n, 7), one_row, 0)
            for b in range(3, bits):
                size = 1 << b
                first = pl.multiple_of(end - jnp.bitwise_and(n, ~(2 * size - 1)) - size, 8)
                copy = pltpu.make_async_copy(zero_ref.at[pl.ds(0, size)], xs_ref.at[pl.ds(first, size)], zero_sem)

                @pl.when(jnp.bitwise_and(n, size) != 0)
                def _():
                    if wait:
                        copy.wait()
                    else:
                        copy.start()

        def start_pads(e, c):
            pad_copies(e, False)
            return c

        def wait_pads(e, c):
            pad_copies(e, True)
            return c

        def tile_copy(j):
            return pltpu.make_async_copy(zero_ref, xs_ref.at[pl.ds(j * tile, tile)], zero_sem)

        def start_tile(j, c):
            tile_copy(j).start()
            return c

        def wait_tile(j, c):
            tile_copy(j).wait()
            return c

        lax.fori_loop(0, N_EXPERTS, start_pads, 0)
        lax.fori_loop(nv_ref[0], n_tiles, start_tile, 0)
        lax.fori_loop(0, N_EXPERTS, wait_pads, 0)
        lax.fori_loop(nv_ref[0], n_tiles, wait_tile, 0)

    @pl.when(step >= 2)
    def _():
        drain(tm, ahead)

    @pl.when(step + 1 < tiles_p)
    def _():
        load_prompt(step + 1, ahead).start()

    @pl.when(step + 1 == tiles_p)
    def _():
        load_sample(ahead).start()

    @pl.when(step < tiles_p)
    def _():
        load_prompt(step, slot).wait()
        scatter(tm, posp_ref, slot)

    @pl.when(step == tiles_p)
    def _():
        load_sample(slot).wait()
        scatter(n_s, poss_ref, slot)
        drain(tm, lax.rem(step + 2, 3))
        drain(n_s, slot)


def _dispatch(count, offset, padded, n_valid, pos_p, pos_s, h2_p, h2_s, *, cap, tm, tile):
    n_p, n_s = h2_p.shape[0], h2_s.shape[0]
    tiles_p = n_p // tm
    assert tiles_p >= 2 and n_s <= tm
    last_p = lambda i, *_: (jnp.minimum(i, tiles_p - 1), 0, 0)
    in_specs = [pl.BlockSpec((1, 1, 2 * tm), last_p, memory_space=pltpu.SMEM),
                pl.BlockSpec((1, 1, 2 * n_s), lambda i, *_: (0, 0, 0), memory_space=pltpu.SMEM),
                pl.BlockSpec(memory_space=pl.ANY),
                pl.BlockSpec(memory_space=pl.ANY)]
    grid_spec = pltpu.PrefetchScalarGridSpec(
        num_scalar_prefetch=4, grid=(tiles_p + 1,), in_specs=in_specs,
        out_specs=pl.BlockSpec(memory_space=pl.ANY),
        scratch_shapes=[pltpu.VMEM((tile, D_MODEL), F32), pltpu.VMEM((3, tm // ROW_TILE, ROW_TILE, D_MODEL), F32),
                        pltpu.SemaphoreType.DMA((3,)), pltpu.SemaphoreType.DMA((3,)), pltpu.SemaphoreType.DMA(())],
    )
    by_tile = lambda a: a.reshape(a.shape[0] // ROW_TILE, ROW_TILE, D_MODEL)
    return pl.pallas_call(
        functools.partial(_dispatch_body, tiles_p=tiles_p, n_tiles=cap // tile),
        out_shape=jax.ShapeDtypeStruct((cap, D_MODEL), F32), grid_spec=grid_spec,
        compiler_params=pltpu.CompilerParams(dimension_semantics=("arbitrary",), vmem_limit_bytes=VMEM_LIMIT,
                                             has_side_effects=True),
        name="dispatch",
    )(count, offset, padded, n_valid, pos_p.reshape(tiles_p, 1, 2 * tm), pos_s.reshape(1, 1, 2 * n_s),
      by_tile(h2_p), by_tile(h2_s))


def _moe_body(te_ref, nv_ref, run_ref, nxt_ref, x_ref, wg_hbm, wu_hbm, wd_hbm, y_ref,
              wg_f, wu_f, wd_f, wg_s, wu_s, wd_s, sem):
    i = pl.program_id(0)
    active = i < nv_ref[0]
    fresh = (i == 0) | (te_ref[i] != te_ref[jnp.maximum(i - 1, 0)])
    slot = lax.rem(run_ref[i], 2)

    def fetch(e, s):
        return (pltpu.make_async_copy(wg_hbm.at[e], wg_f.at[s], sem.at[s]),
                pltpu.make_async_copy(wu_hbm.at[e], wu_f.at[s], sem.at[s]),
                pltpu.make_async_copy(wd_hbm.at[e], wd_f.at[s], sem.at[s]))

    @pl.when(active & (i == 0))
    def _():
        for copy in fetch(te_ref[0], 0):
            copy.start()

    @pl.when(active & fresh)
    def _():
        for copy in fetch(te_ref[i], slot):
            copy.wait()

        @pl.when(nxt_ref[i] >= 0)
        def _():
            for copy in fetch(nxt_ref[i], 1 - slot):
                copy.start()

        wg_s[...] = wg_f[slot].astype(BF16)
        wu_s[...] = wu_f[slot].astype(BF16)
        wd_s[...] = wd_f[slot].astype(BF16)

    @pl.when(active)
    def _():
        x = x_ref[...].astype(BF16)
        a = _dot(x, wg_s[...])
        b = _dot(x, wu_s[...])
        mid = (a * _sigmoid(a) * b).astype(BF16)
        y_ref[...] = _dot(mid, wd_s[...])

    @pl.when(jnp.logical_not(active))
    def _():
        y_ref[...] = jnp.zeros_like(y_ref)


def _moe(tile_expert, n_valid, x_sorted, w_eg, w_eu, w_ed, tm):
    cap = x_sorted.shape[0]
    n_tiles = cap // tm
    tile = jnp.arange(n_tiles, dtype=jnp.int32)
    live = tile < n_valid[0]
    starts = jnp.concatenate([jnp.zeros((1,), jnp.bool_), tile_expert[1:] != tile_expert[:-1]]) & live
    run = jnp.cumsum(starts.astype(jnp.int32))
    later = (tile_expert[None, :] > tile_expert[:, None]) & live[None, :]
    nxt = jnp.min(jnp.where(later, tile_expert[None, :], N_EXPERTS), axis=1)
    nxt = jnp.where(nxt < N_EXPERTS, nxt, -1).astype(jnp.int32)

    xmap = lambda i, te, nv, rn, nx: (jnp.maximum(jnp.minimum(i, nv[0] - 1), 0), 0)
    grid_spec = pltpu.PrefetchScalarGridSpec(
        num_scalar_prefetch=4,
        grid=(n_tiles,),
        in_specs=[pl.BlockSpec((tm, D_MODEL), xmap),
                  pl.BlockSpec(memory_space=pl.ANY), pl.BlockSpec(memory_space=pl.ANY),
                  pl.BlockSpec(memory_space=pl.ANY)],
        out_specs=pl.BlockSpec((tm, D_MODEL), lambda i, te, nv, rn, nx: (i, 0)),
        scratch_shapes=[pltpu.VMEM((2, D_MODEL, D_EXPERT), F32), pltpu.VMEM((2, D_MODEL, D_EXPERT), F32),
                        pltpu.VMEM((2, D_EXPERT, D_MODEL), F32),
                        pltpu.VMEM((D_MODEL, D_EXPERT), BF16), pltpu.VMEM((D_MODEL, D_EXPERT), BF16),
                        pltpu.VMEM((D_EXPERT, D_MODEL), BF16), pltpu.SemaphoreType.DMA((2,))],
    )
    return pl.pallas_call(
        _moe_body, out_shape=jax.ShapeDtypeStruct((cap, D_MODEL), F32), grid_spec=grid_spec,
        compiler_params=_params(1), name="moe",
    )(tile_expert, n_valid, run, nxt, x_sorted, w_eg, w_eu, w_ed)


def _final_body(pos_ref, next_ref, x1_ref, route_ref, gt2_ref, gf_ref, ys_ref, o_ref, y_buf, sem):
    step = pl.program_id(0)
    tm = x1_ref.shape[0]
    slot = lax.rem(step, 2)

    def row_copy(s, k, group, j, src_row):
        return pltpu.make_async_copy(ys_ref.at[pl.ds(src_row, 1)], y_buf.at[s, k, group, pl.ds(j, 1)], sem.at[s])

    def gather(p_ref, s):
        def issue(group, c):
            for j in range(ROW_TILE):
                t = group * ROW_TILE + j
                row_copy(s, 0, group, j, p_ref[0, 0, 2 * t]).start()
                row_copy(s, 1, group, j, p_ref[0, 0, 2 * t + 1]).start(priority=1)
            return c

        lax.fori_loop(0, tm // ROW_TILE, issue, 0)

    @pl.when(step == 0)
    def _():
        gather(pos_ref, 0)

    @pl.when(step + 1 < pl.num_programs(0))
    def _():
        gather(next_ref, 1 - slot)

    def drain(t, c):
        row_copy(slot, 0, 0, 0, 0).wait()
        row_copy(slot, 1, 0, 0, 0).wait()
        return c

    lax.fori_loop(0, tm, drain, 0, unroll=8)

    route = route_ref[...]
    moe = (route[:, 0:1] * y_buf[slot, 0].reshape(tm, D_MODEL)
           + route[:, 1:2] * y_buf[slot, 1].reshape(tm, D_MODEL))
    x = x1_ref[...] + gt2_ref[0] * moe
    o_ref[...] = x * lax.rsqrt(jnp.mean(x * x, axis=-1, keepdims=True) + EPS) * gf_ref[...]


def _final(pos, x1, route, gt2, gf, y_sorted, *, tm, mod_index, name):
    n = x1.shape[0]
    steps = n // tm
    row = lambda i: (i, 0)
    pos3 = pos.reshape(steps, 1, 2 * tm)
    return pl.pallas_call(
        _final_body,
        out_shape=jax.ShapeDtypeStruct((n, D_MODEL), F32),
        grid=(steps,),
        in_specs=[pl.BlockSpec((1, 1, 2 * tm), lambda i: (i, 0, 0), memory_space=pltpu.SMEM),
                  pl.BlockSpec((1, 1, 2 * tm), lambda i: (jnp.minimum(i + 1, steps - 1), 0, 0),
                               memory_space=pltpu.SMEM),
                  pl.BlockSpec((tm, D_MODEL), row), pl.BlockSpec((tm, LANES), row),
                  _mod_spec(gt2, mod_index), pl.BlockSpec((1, D_MODEL), lambda i: (0, 0)),
                  pl.BlockSpec(memory_space=pl.ANY)],
        out_specs=pl.BlockSpec((tm, D_MODEL), row),
        scratch_shapes=[pltpu.VMEM((2, 2, tm // ROW_TILE, ROW_TILE, D_MODEL), F32), pltpu.SemaphoreType.DMA((2,))],
        compiler_params=_params(1), name=name,
    )(pos3, pos3, x1, route, gt2[0], gf, y_sorted)


def _t5_bucket(dist):
    max_exact = N_BUCKETS // 2
    dist = np.asarray(dist)
    d = np.maximum(dist, 1).astype(np.float32)
    large = max_exact + (np.log(d / np.float32(max_exact)) / np.float32(math.log(MAX_DISTANCE / max_exact))
                         * np.float32(N_BUCKETS - max_exact)).astype(np.int32)
    return np.where(dist < max_exact, dist, np.minimum(large, N_BUCKETS - 1)).astype(np.int32)


def _bias_lookup(tab, dist):
    onehot = (jnp.asarray(_t5_bucket(dist))[..., None] == jnp.arange(N_BUCKETS)).astype(F32)
    return jnp.einsum("...b,bh->...h", onehot, tab, precision=HIGHEST)


def _prompt_bias(tab, dil):
    delta = np.arange(BLOCK)[:, None] + BLOCK - np.arange(2 * BLOCK)[None, :]
    return jnp.transpose(_bias_lookup(tab, np.maximum(delta, 0) * dil), (2, 0, 1))


def _sample_bias(tab, win, dil, t_new):
    cols = win + LANES
    back = win + np.arange(t_new)[:, None] - np.arange(cols)[None, :]
    bias = _bias_lookup(tab, np.maximum(back, 0))
    return jnp.transpose(bias, (2, 0, 1)).reshape(HEADS * t_new, cols)


def kernel(x_prompt, x_sample, c_prompt, c_sample, cache_kv_w128, cache_kv_w512, cache_kv_w2048, rel_bias, w_ada,
           b_ada, g_norm1, w_in, ln_v_g, ln_v_b, w_spatial, b_spatial, w_up_a, w_up_b, w_out, g_norm2,
           w_route_group, b_route_group, w_route_expert, b_route_expert, w_e_gate, w_e_up, w_e_down, g_final):
    assert w_ada.shape[0] == 1, "single layer"
    bp, s_len, _ = x_prompt.shape
    bs, t_new, _ = x_sample.shape
    n_p, n_s = bp * s_len, bs * t_new
    caches = (cache_kv_w128[0], cache_kv_w512[0], cache_kv_w2048[0])
    assert all(c.shape[1] == win for c, (win, _) in zip(caches, DILATED_GROUPS)), "cache holds one full window"

    mod = _ada(jnp.concatenate([c_prompt, c_sample], axis=0), w_ada[0], b_ada[0])
    mod_p = [(mod[:bp].reshape(bp, 1, 6 * D_MODEL), j) for j in range(6)]
    mod_s = [(jnp.repeat(mod[bp:], t_new, axis=0).reshape(1, n_s, 6 * D_MODEL), j) for j in range(6)]

    w_in_bf = w_in[0].astype(BF16)
    g1 = g_norm1[0].reshape(1, D_MODEL)
    lng, lnb = ln_v_g[0].reshape(1, D_B), ln_v_b[0].reshape(1, D_B)
    ws_p = w_spatial[0]
    bs_p = jnp.repeat(jnp.transpose(b_spatial[0]), SGU_DIM, axis=1)
    reps = n_s // t_new
    ws_s = jnp.tile(w_spatial[0][:, :t_new, :t_new], (1, reps, reps))
    bs_s = jnp.tile(bs_p[:t_new], (reps, 1))

    tpb = s_len // TM_STAGE1
    tail_p = [(bp, min(win, s_len)) for win, _ in DILATED_GROUPS]

    def tail_index_p(i, g, rows):
        first = tpb - tail_p[g][1] // rows
        return (i // tpb, jnp.maximum(i % tpb - first, 0), 0)

    q_s, kv0_s, kv1_s, kv2_s, ob_s, ga_s, gb_s, vn_s = _stage1(
        x_sample.reshape(n_s, D_MODEL), mod_s[0], mod_s[1], g1, w_in_bf, lng, lnb, ws_s, bs_s,
        tm=n_s, mod_index=lambda i: (0, 0, 0), stride_index=None, tail_shapes=[(1, n_s)] * N_DIL,
        tail_index=lambda i, g, rows: (0, 0, 0),
        period=t_new, by_stride=False, emit_vn=True, name="stage1_sample")

    tabs = [rel_bias[:, g * HEADS:(g + 1) * HEADS].astype(F32) for g in range(N_DIL)]
    sample_bias = [_sample_bias(tabs[g], win, dil, t_new) for g, (win, dil) in enumerate(DILATED_GROUPS)]
    caches_t = [jnp.transpose(c, (0, 2, 3, 4, 1)).reshape(bs, 2, D_A, c.shape[1]) for c in caches]
    outs = _stage1(
        x_prompt.reshape(n_p, D_MODEL), mod_p[0], mod_p[1], g1, w_in_bf, lng, lnb, ws_p, bs_p,
        tm=TM_STAGE1, mod_index=lambda i: (i // tpb, 0, 0),
        stride_index=(bp, s_len, lambda i: (i // tpb, 0, i % tpb, 0)), tail_shapes=tail_p, tail_index=tail_index_p,
        period=CHUNK, by_stride=True, emit_vn=False, name="stage1_prompt",
        cache=(q_s, (kv0_s, kv1_s, kv2_s), caches_t, sample_bias, t_new))
    q_g, k_g, v_g = outs[0:3], outs[3:6], outs[6:9]
    kv0_p, kv1_p, kv2_p, ob_p, ga_p, gb_p, cache0_t, cache1_t, cache2_t, oa_s = outs[9:]

    o_groups, lse_groups = [], []
    for g, (win, dil) in enumerate(DILATED_GROUPS):
        o, lse = _attn_prompt_group(q_g[g], k_g[g], v_g[g], _prompt_bias(tabs[g], dil), dil, ATTN_BLOCKS_PER_STEP)
        o_groups.append(o)
        lse_groups.append(lse)

    wua, wub, wo = w_up_a[0].astype(BF16), w_up_b[0].astype(BF16), w_out[0].astype(BF16)
    g2 = g_norm2[0].reshape(1, D_MODEL)
    wr = jnp.concatenate([w_route_expert[0].reshape(D_MODEL, N_EXPERTS), w_route_group[0]], axis=1)
    wr = jnp.pad(wr, ((0, 0), (0, LANES - wr.shape[1])))
    br = jnp.concatenate([b_route_expert[0].reshape(N_EXPERTS), b_route_group[0]])
    br = jnp.pad(br, (0, LANES - br.shape[0])).reshape(1, LANES)
    wr_hi = wr.astype(BF16)
    wr_lo = (wr - wr_hi.astype(F32)).astype(BF16)

    tpb2 = s_len // TM_STAGE2
    x1_p, h2_p, route_p, code_p, cnt_p = _stage2(
        o_groups + lse_groups, ob_p, ga_p, gb_p, x_prompt.reshape(n_p, D_MODEL),
        mod_p[2], mod_p[3], mod_p[4], wua, wub, wo, g2, wr_hi, wr_lo, br,
        tm=TM_STAGE2, mod_index=lambda i: (i // tpb2, 0, 0), stride_index=lambda i: (i // tpb2, 0, i % tpb2, 0),
        merged=False, name="stage2_prompt")
    x1_s, h2_s, route_s, code_s, cnt_s = _stage2(
        [oa_s], ob_s, ga_s, gb_s, x_sample.reshape(n_s, D_MODEL),
        mod_s[2], mod_s[3], mod_s[4], wua, wub, wo, g2, wr_hi, wr_lo, br,
        tm=n_s, mod_index=lambda i: (0, 0, 0), stride_index=None, merged=True, name="stage2_sample")

    cap = 2 * (n_p + n_s) + N_EXPERTS * TM_MOE
    n_tiles = cap // TM_MOE
    count_p = cnt_p[0, :N_EXPERTS]
    count = count_p + cnt_s[0, :N_EXPERTS]
    padded = ((count + TM_MOE - 1) // TM_MOE) * TM_MOE
    ends = jnp.cumsum(padded)
    offset = ends - padded
    n_valid = (ends[-1] // TM_MOE).astype(jnp.int32).reshape(1)
    tile_start = jnp.arange(n_tiles, dtype=jnp.int32) * TM_MOE
    tile_expert = jnp.minimum(jnp.sum(ends[None, :] <= tile_start[:, None], axis=1), N_EXPERTS - 1).astype(jnp.int32)

    def sorted_rows(code, earlier):
        row = jnp.pad(offset + earlier, (0, LANES - N_EXPERTS))[None, :] + jnp.right_shift(code, 2)
        return jnp.stack([jnp.sum(jnp.where((code & 3) == k, row, 0), axis=1) for k in (1, 2)], axis=1)

    pos_p = sorted_rows(code_p, jnp.zeros_like(count_p))
    pos_s = sorted_rows(code_s, count_p)
    x_sorted = _dispatch(count, offset, padded, n_valid, pos_p, pos_s, h2_p, h2_s, cap=cap, tm=TM_FINAL, tile=TM_MOE)

    y_sorted = _moe(tile_expert, n_valid, x_sorted, w_e_gate[0], w_e_up[0], w_e_down[0], TM_MOE)

    gf = g_final.reshape(1, D_MODEL)
    tpb3 = s_len // TM_FINAL
    y_p = _final(pos_p, x1_p, route_p, mod_p[5], gf, y_sorted,
                 tm=TM_FINAL, mod_index=lambda i: (i // tpb3, 0, 0), name="final_prompt")
    y_s = _final(pos_s, x1_s, route_s, mod_s[5], gf, y_sorted,
                 tm=n_s, mod_index=lambda i: (0, 0, 0), name="final_sample")

    kv_p = [a.reshape(1, bp, a.shape[1], 2, HEADS, HEAD_DIM) for a in (kv0_p, kv1_p, kv2_p)]
    kv_s = [jnp.transpose(c.reshape(bs, 2, HEADS, HEAD_DIM, c.shape[-1]), (0, 4, 1, 2, 3))[None]
            for c in (cache0_t, cache1_t, cache2_t)]
    return (y_p.reshape(bp, s_len, D_MODEL), y_s.reshape(bs, t_new, D_MODEL),
            kv_p[0], kv_p[1], kv_p[2], kv_s[0], kv_s[1], kv_s[2],
            vn_s.reshape(1, bs, t_new, D_B))
```

```python
import functools
import math

import numpy as np
import jax
import jax.numpy as jnp
from jax import lax
from jax.experimental import pallas as pl
from jax.experimental.pallas import tpu as pltpu

F32 = jnp.float32
BF16 = jnp.bfloat16
HIGHEST = lax.Precision.HIGHEST

D_MODEL = 1024
HEAD_DIM = 64
HEADS = 8
DILATED_GROUPS = ((128, 1), (512, 4), (2048, 16))
N_DIL = 3
D_A = HEADS * HEAD_DIM
D_QKV = N_DIL * D_A
BLOCK = 128
STEPS = 128
SCALE = HEAD_DIM ** -0.5
CHUNK = 128
D_B = 512
SGU_GROUPS = 4
SGU_DIM = D_B // SGU_GROUPS
N_BUCKETS = 32
MAX_DISTANCE = 2048
MOE_GROUPS = 4
EXPERTS_PER_GROUP = 8
N_EXPERTS = 32
D_EXPERT = 512
EPS = 1e-6
NEG_INF = -1e30
LOG2E = math.log2(math.e)
LN2 = math.log(2.0)
LANES = 128
ROW_TILE = 8
C_U = 3 * D_QKV
C_V = C_U + D_B
C_GA = C_V + D_B
C_GB = C_GA + D_MODEL
IN_COLS = C_GB + D_MODEL

TM_STAGE1 = 256
TM_STAGE2 = 512
TM_MOE = 512
ATTN_BLOCKS_PER_STEP = 8
TM_FINAL = 512
VMEM_LIMIT = 56 * 1024 * 1024


def _dot(a, b):
    return jnp.dot(a, b, preferred_element_type=F32)


def _dot_nt(a, b):
    return lax.dot_general(a, b, (((1,), (1,)), ((), ())), preferred_element_type=F32)


def _sigmoid(x):
    return 1.0 / (1.0 + jnp.exp(-x))


def _gelu(x):
    return 0.5 * x * (1.0 + lax.erf(x * (2.0 ** -0.5)))


def _params(n_grid):
    return pltpu.CompilerParams(dimension_semantics=("arbitrary",) * n_grid, vmem_limit_bytes=VMEM_LIMIT)


def _mod_spec(vectors_col, batch_index):
    vectors, col = vectors_col
    return pl.BlockSpec((1, vectors.shape[1], D_MODEL), lambda i: batch_index(i)[:2] + (col,))


def _row_copy(src, src_row, dst, dst_row, sem):
    return pltpu.make_async_copy(src.at[pl.ds(src_row, 1)], dst.at[pl.ds(dst_row, 1)], sem)


def _ada_body(c_ref, w_ref, b_ref, o_ref):
    c = c_ref[...]
    s = c * _sigmoid(c)
    o_ref[...] = jnp.dot(s, w_ref[...], precision=HIGHEST, preferred_element_type=F32) + b_ref[...]


def _ada(c_all, w_ada, b_ada):
    n, d = c_all.shape
    cols = w_ada.shape[1]
    tn = 1024
    return pl.pallas_call(
        _ada_body,
        out_shape=jax.ShapeDtypeStruct((n, cols), F32),
        grid=(cols // tn,),
        in_specs=[pl.BlockSpec((n, d), lambda j: (0, 0)),
                  pl.BlockSpec((d, tn), lambda j: (0, j)),
                  pl.BlockSpec((1, tn), lambda j: (0, j))],
        out_specs=pl.BlockSpec((n, tn), lambda j: (0, j)),
        compiler_params=_params(1),
        name="ada",
    )(c_all, w_ada, b_ada.reshape(1, cols))


def _stage1_body(x_ref, sh_ref, sc_ref, g1_ref, w_ref, lng_ref, lnb_ref, ws_ref, bs_ref, *rest,
                 period, by_stride, emit_vn, cache_t_new):
    rest = list(rest)
    cache_refs = [rest.pop(0) for _ in range(1 + 3 * N_DIL)] if cache_t_new else []
    if by_stride:
        qkv_refs = [[rest.pop(0) for _ in range(N_DIL)] for _ in range(3)]
    else:
        q_ref = rest.pop(0)
    kv_refs = [rest.pop(0) for _ in range(N_DIL)]
    ob_ref, ga_ref, gb_ref = rest.pop(0), rest.pop(0), rest.pop(0)
    vn_ref = rest.pop(0) if emit_vn else None
    if cache_t_new:
        cache_refs += [rest.pop(0) for _ in range(N_DIL + 1)]
    stage_ref = rest.pop(0) if by_stride else None
    if cache_t_new:
        _sample_cache_step(pl.program_id(0), cache_refs + rest, cache_t_new)

    x = x_ref[...]
    tm = x.shape[0]
    h = x * lax.rsqrt(jnp.mean(x * x, axis=-1, keepdims=True) + EPS) * g1_ref[...]
    h = h * (1.0 + sc_ref[0]) + sh_ref[0]
    hb = h.astype(BF16)

    q = _dot(hb, w_ref[:, 0:D_QKV]) * (SCALE * LOG2E if by_stride else SCALE)
    k = _dot(hb, w_ref[:, D_QKV:2 * D_QKV])
    v = _dot(hb, w_ref[:, 2 * D_QKV:3 * D_QKV])
    if by_stride:
        per_group = D_A // LANES
        for val, refs in zip((q, k, v), qkv_refs):
            for c in range(D_QKV // LANES):
                stage_ref[c] = val[:, c * LANES:(c + 1) * LANES]
            for g, (_, dil) in enumerate(DILATED_GROUPS):
                for r in range(dil):
                    for c in range(per_group):
                        piece = stage_ref[g * per_group + c, pl.ds(r, tm // dil, stride=dil), :]
                        refs[g][0, r, :, c * LANES:(c + 1) * LANES] = piece.astype(BF16)
    else:
        q_ref[...] = q
    for g in range(N_DIL):
        rows = kv_refs[g].shape[1]
        kv_refs[g][0, :, 0:D_A] = k[tm - rows:, g * D_A:(g + 1) * D_A]
        kv_refs[g][0, :, D_A:2 * D_A] = v[tm - rows:, g * D_A:(g + 1) * D_A]

    ga_ref[...] = _sigmoid(_dot(hb, w_ref[:, C_GA:C_GB])).astype(BF16)
    gb_ref[...] = _sigmoid(_dot(hb, w_ref[:, C_GB:IN_COLS])).astype(BF16)

    u = _gelu(_dot(hb, w_ref[:, C_U:C_V]))
    vb = _gelu(_dot(hb, w_ref[:, C_V:C_GA]))
    xc = vb - jnp.mean(vb, axis=-1, keepdims=True)
    vn = xc * lax.rsqrt(jnp.mean(xc * xc, axis=-1, keepdims=True) + EPS) * lng_ref[...] + lnb_ref[...]
    if emit_vn:
        vn_ref[...] = vn
    vnb = vn.astype(BF16)

    r = ws_ref.shape[1]
    ri = lax.broadcasted_iota(jnp.int32, (r, r), 0)
    ci = lax.broadcasted_iota(jnp.int32, (r, r), 1)
    keep = ci <= ri
    if period != r:
        sh = int(math.log2(period))
        keep = keep & (jnp.right_shift(ri, sh) == jnp.right_shift(ci, sh))
    for g in range(SGU_GROUPS):
        wg = jnp.where(keep, ws_ref[g], 0.0).astype(BF16)
        cs = slice(g * SGU_DIM, (g + 1) * SGU_DIM)
        for c in range(tm // r):
            rs = slice(c * r, (c + 1) * r)
            mixed = _dot(wg, vnb[rs, cs]) + bs_ref[:, cs]
            ob_ref[rs, cs] = (u[rs, cs] * mixed).astype(BF16)


def _stage1(x2d, sh, sc, g1, w_in_bf, lng, lnb, ws, bs, *, tm, mod_index, stride_index, tail_shapes, tail_index,
            period, by_stride, emit_vn, name, cache=None):
    n = x2d.shape[0]
    grid = (n // tm,)
    r = ws.shape[1]
    const2 = lambda i: (0, 0)
    row = lambda i: (i, 0)
    in_specs = [
        pl.BlockSpec((tm, D_MODEL), row),
        _mod_spec(sh, mod_index),
        _mod_spec(sc, mod_index),
        pl.BlockSpec((1, D_MODEL), const2),
        pl.BlockSpec((D_MODEL, IN_COLS), const2, pipeline_mode=pl.Buffered(1)),
        pl.BlockSpec((1, D_B), const2),
        pl.BlockSpec((1, D_B), const2),
        pl.BlockSpec((SGU_GROUPS, r, r), lambda i: (0, 0, 0)),
        pl.BlockSpec((r, D_B), const2),
    ]
    out_shape, out_specs, scratch = [], [], []
    if by_stride:
        n_batch, s_len = stride_index[0], stride_index[1]
        for _ in range(3):
            for _, dil in DILATED_GROUPS:
                out_shape.append(jax.ShapeDtypeStruct((n_batch, dil, s_len // dil, D_A), BF16))
                out_specs.append(pl.BlockSpec((1, dil, tm // dil, D_A), stride_index[2]))
        scratch.append(pltpu.VMEM((D_QKV // LANES, tm, LANES), F32))
    else:
        out_shape.append(jax.ShapeDtypeStruct((n, D_QKV), F32))
        out_specs.append(pl.BlockSpec((tm, D_QKV), row))
    for g in range(N_DIL):
        nb, keep = tail_shapes[g]
        rows = min(tm, keep)
        out_shape.append(jax.ShapeDtypeStruct((nb, keep, 2 * D_A), F32))
        out_specs.append(pl.BlockSpec((1, rows, 2 * D_A), functools.partial(tail_index, g=g, rows=rows)))
    out_shape += [jax.ShapeDtypeStruct((n, D_B), BF16),
                  jax.ShapeDtypeStruct((n, D_MODEL), BF16),
                  jax.ShapeDtypeStruct((n, D_MODEL), BF16)]
    out_specs += [pl.BlockSpec((tm, D_B), row), pl.BlockSpec((tm, D_MODEL), row), pl.BlockSpec((tm, D_MODEL), row)]
    if emit_vn:
        out_shape.append(jax.ShapeDtypeStruct((n, D_B), F32))
        out_specs.append(pl.BlockSpec((tm, D_B), row))
    args = [x2d, sh[0], sc[0], g1, w_in_bf, lng, lnb, ws, bs]
    cache_t_new = 0
    if cache is not None:
        c_args, c_in, c_shape, c_out, c_scratch, units = _sample_cache_specs(*cache)
        assert units == grid[0], "one sample-cache unit per projection tile"
        args += c_args
        in_specs += c_in
        out_shape += c_shape
        out_specs += c_out
        scratch += c_scratch
        cache_t_new = cache[-1]
    body = functools.partial(_stage1_body, period=period, by_stride=by_stride, emit_vn=emit_vn,
                             cache_t_new=cache_t_new)
    return pl.pallas_call(
        body, out_shape=out_shape, grid=grid, in_specs=in_specs, out_specs=out_specs, scratch_shapes=scratch,
        compiler_params=_params(1), name=name,
    )(*args)


def _attn_prompt_body(q_ref, kp_ref, kc_ref, vp_ref, vc_ref, bias_ref, o_ref, lse_ref):
    n = pl.program_id(2)
    n_sub = q_ref.shape[2] // BLOCK
    ri = lax.broadcasted_iota(jnp.int32, (BLOCK, 2 * BLOCK), 0)
    ci = lax.broadcasted_iota(jnp.int32, (BLOCK, 2 * BLOCK), 1)
    delta = ri + BLOCK - ci
    band = (delta >= 0) & (delta <= STEPS)
    first_key = jnp.where(n > 0, 0, BLOCK)
    low_q = lax.broadcasted_iota(jnp.int32, (BLOCK, LANES), 1) < HEAD_DIM
    low_v = lax.broadcasted_iota(jnp.int32, (2 * BLOCK, LANES), 1) < HEAD_DIM
    ones_lo = jnp.where(low_v, 1.0, 0.0).astype(BF16)
    ones_hi = jnp.where(low_v, 0.0, 1.0).astype(BF16)
    for sub in range(n_sub):
        rows = slice(sub * BLOCK, (sub + 1) * BLOCK)
        valid = band & (ci >= first_key) if sub == 0 else band
        for pair in range(HEADS // 2):
            cs = slice(pair * LANES, (pair + 1) * LANES)
            qp = q_ref[0, 0, rows, cs]
            if sub == 0:
                kp = jnp.concatenate([kp_ref[0, 0, :, cs], kc_ref[0, 0, 0:BLOCK, cs]], axis=0)
                vp = jnp.concatenate([vp_ref[0, 0, :, cs], vc_ref[0, 0, 0:BLOCK, cs]], axis=0)
            else:
                kp = kc_ref[0, 0, (sub - 1) * BLOCK:(sub + 1) * BLOCK, cs]
                vp = vc_ref[0, 0, (sub - 1) * BLOCK:(sub + 1) * BLOCK, cs]
            acc, tops = None, []
            for half in range(2):
                own_q = low_q if half == 0 else jnp.logical_not(low_q)
                own_v = low_v if half == 0 else jnp.logical_not(low_v)
                s = _dot_nt(jnp.where(own_q, qp, jnp.zeros_like(qp)), kp) + bias_ref[2 * pair + half]
                s = jnp.where(valid, s, NEG_INF)
                m = jnp.max(s, axis=-1, keepdims=True)
                p = jnp.exp2(s - m).astype(BF16)
                w = jnp.concatenate([jnp.where(own_v, vp, jnp.zeros_like(vp)), ones_lo if half == 0 else ones_hi],
                                    axis=1)
                r = _dot(p, w)
                acc = r if acc is None else acc + r
                tops.append(m)
            den = acc[:, LANES:]
            o_ref[0, 0, rows, cs] = (acc[:, :LANES] / den).astype(BF16)
            lse_ref[0, 0, rows, cs] = jnp.where(low_q, tops[0], tops[1]) * LN2 + jnp.log(den)


def _attn_prompt_group(q4, k4, v4, bias, dil, n_sub):
    b_sz, _, m_len, _ = q4.shape
    span = n_sub * BLOCK
    cur = pl.BlockSpec((1, 1, span, D_A), lambda b, r, n: (b, r, n, 0))
    prev = pl.BlockSpec((1, 1, BLOCK, D_A), lambda b, r, n: (b, r, jnp.maximum(n * n_sub - 1, 0), 0))
    return pl.pallas_call(
        _attn_prompt_body,
        out_shape=[jax.ShapeDtypeStruct((b_sz, dil, m_len, D_A), BF16),
                   jax.ShapeDtypeStruct((b_sz, dil, m_len, D_A), F32)],
        grid=(b_sz, dil, m_len // span),
        in_specs=[cur, prev, cur, prev, cur,
                  pl.BlockSpec((HEADS, BLOCK, 2 * BLOCK), lambda b, r, n: (0, 0, 0))],
        out_specs=[cur, cur],
        compiler_params=_params(3),
        name=f"attn_prompt_d{dil}",
    )(q4, k4, k4, v4, v4, bias)


def _sample_cache_step(unit, refs, t_new):
    refs = list(refs)
    q_ref = refs.pop(0)
    new_refs = [refs.pop(0) for _ in range(N_DIL)]
    cache_refs = [refs.pop(0) for _ in range(N_DIL)]
    bias_refs = [refs.pop(0) for _ in range(N_DIL)]
    out_refs = [refs.pop(0) for _ in range(N_DIL)]
    o_ref = refs.pop(0)
    p_refs = [refs.pop(0) for _ in range(N_DIL)]
    scale_ref = refs.pop(0)
    kv = lax.rem(unit, 2)
    rows = HEADS * t_new

    def allowed(g, col0, cols):
        win, dil = DILATED_GROUPS[g]
        tok = jnp.bitwise_and(lax.broadcasted_iota(jnp.int32, (rows, cols), 0), t_new - 1)
        col = lax.broadcasted_iota(jnp.int32, (rows, cols), 1) + col0
        back = win + tok - col
        return ((col < win + t_new) & (back >= 0) & (jnp.bitwise_and(back, dil - 1) == 0)
                & (back <= STEPS * dil))

    lane = lax.broadcasted_iota(jnp.int32, (D_A, LANES), 1)
    front = jnp.zeros((LANES - t_new, D_A), F32)
    for g in range(N_DIL):
        win = cache_refs[g].shape[-1]
        new = new_refs[g][0]
        half = jnp.where(kv == 0, new[:, 0:D_A], new[:, D_A:2 * D_A])
        tail = jnp.transpose(jnp.concatenate([front, half], axis=0))
        rolled = pltpu.roll(cache_refs[g][0, 0], win - t_new, axis=1)
        if win > LANES:
            out_refs[g][0, 0, :, 0:win - LANES] = rolled[:, 0:win - LANES]
        out_refs[g][0, 0, :, win - LANES:win] = jnp.where(lane >= LANES - t_new, tail, rolled[:, win - LANES:win])

    ri = lax.broadcasted_iota(jnp.int32, (rows, D_A), 0)
    ci = lax.broadcasted_iota(jnp.int32, (rows, D_A), 1)
    own_head = jnp.right_shift(ri, int(math.log2(t_new))) == jnp.right_shift(ci, int(math.log2(HEAD_DIM)))
    pad = jnp.zeros((LANES - t_new, D_A), F32)
    glane = lax.broadcasted_iota(jnp.int32, (rows, LANES), 1)

    @pl.when(kv == 0)
    def _():
        q = q_ref[...]
        stats = []
        for g in range(N_DIL):
            win = cache_refs[g].shape[-1]
            qg = q[:, g * D_A:(g + 1) * D_A]
            qblk = jnp.where(own_head, jnp.concatenate([qg] * HEADS, axis=0), 0.0).astype(BF16)
            s_c = _dot(qblk, cache_refs[g][0, 0].astype(BF16)) + bias_refs[g][:, 0:win]
            s_c = jnp.where(allowed(g, 0, win), s_c, NEG_INF)
            k_new = jnp.concatenate([new_refs[g][0][:, 0:D_A], pad], axis=0).astype(BF16)
            s_n = _dot_nt(qblk, k_new) + bias_refs[g][:, win:]
            s_n = jnp.where(allowed(g, win, LANES), s_n, NEG_INF)
            m = jnp.maximum(jnp.max(s_c, axis=-1, keepdims=True), jnp.max(s_n, axis=-1, keepdims=True))
            p_c = jnp.exp(s_c - m)
            p_n = jnp.exp(s_n - m)
            l = jnp.sum(p_c, axis=-1, keepdims=True) + jnp.sum(p_n, axis=-1, keepdims=True)
            p_refs[g][:, 0:win] = p_c.astype(BF16)
            p_refs[g][:, win:] = p_n.astype(BF16)
            stats.append((l, m + jnp.log(l)))
        top = jnp.maximum(jnp.maximum(stats[0][1], stats[1][1]), stats[2][1])
        es = [jnp.exp(lse - top) for _, lse in stats]
        den = es[0] + es[1] + es[2]
        scale = jnp.zeros((rows, LANES), F32)
        for g in range(N_DIL):
            scale = jnp.where(glane == g, es[g] / den / stats[g][0], scale)
        scale_ref[...] = scale

    @pl.when(kv == 1)
    def _():
        acc = jnp.zeros((rows, D_A), F32)
        for g in range(N_DIL):
            win = cache_refs[g].shape[-1]
            part = _dot_nt(p_refs[g][:, 0:win], cache_refs[g][0, 0].astype(BF16))
            v_new = jnp.concatenate([new_refs[g][0][:, D_A:2 * D_A], pad], axis=0).astype(BF16)
            part = part + _dot(p_refs[g][:, win:], v_new)
            acc = acc + scale_ref[:, g:g + 1] * part
        acc = jnp.where(own_head, acc, 0.0)
        out = acc[0:t_new]
        for h in range(1, HEADS):
            out = out + acc[h * t_new:(h + 1) * t_new]
        o_ref[...] = out


def _sample_cache_specs(q_s, kv_new, caches_t, biases, t_new):
    n = q_s.shape[0]
    nb = n // t_new
    blk4 = lambda i: (i // 2, lax.rem(i, 2), 0, 0)
    in_specs = [pl.BlockSpec((t_new, D_QKV), lambda i: (i // 2, 0))]
    in_specs += [pl.BlockSpec((1, t_new, 2 * D_A), lambda i: (i // 2, 0, 0))] * N_DIL
    in_specs += [pl.BlockSpec((1, 1, D_A, c.shape[-1]), blk4) for c in caches_t]
    in_specs += [pl.BlockSpec(a.shape, lambda i: (0, 0), pipeline_mode=pl.Buffered(1)) for a in biases]
    out_shape = [jax.ShapeDtypeStruct(c.shape, c.dtype) for c in caches_t]
    out_specs = [pl.BlockSpec((1, 1, D_A, c.shape[-1]), blk4) for c in caches_t]
    out_shape.append(jax.ShapeDtypeStruct((n, D_A), F32))
    out_specs.append(pl.BlockSpec((t_new, D_A), lambda i: (i // 2, 0)))
    rows = HEADS * t_new
    scratch = [pltpu.VMEM((rows, c.shape[-1] + LANES), BF16) for c in caches_t]
    scratch.append(pltpu.VMEM((rows, LANES), F32))
    args = [q_s, *[a.reshape(nb, t_new, 2 * D_A) for a in kv_new], *caches_t, *biases]
    return args, in_specs, out_shape, out_specs, scratch, 2 * nb


def _stage2_body(*refs, merged):
    refs = list(refs)
    if merged:
        oa_ref = refs.pop(0)
    else:
        o_refs = [refs.pop(0) for _ in range(N_DIL)]
        l_refs = [refs.pop(0) for _ in range(N_DIL)]
    (ob_ref, ga_ref, gb_ref, x_ref, gt1_ref, sh2_ref, sc2_ref, wua_ref, wub_ref, wo_ref, g2_ref,
     wrh_ref, wrl_ref, br_ref, x1_ref, h2_ref, route_ref, code_ref, cnt_ref, run_ref) = refs[:20]
    step = pl.program_id(0)
    tm = x_ref.shape[0]

    @pl.when(step == 0)
    def _():
        run_ref[...] = jnp.zeros_like(run_ref)

    if merged:
        oa = oa_ref[...].astype(BF16)
    else:
        o_stage, l_stage = refs[20], refs[21]

        def by_position(ref, stage, dil):
            if dil == 1:
                return ref[0, 0].astype(F32)
            slabs = ref.shape[-1] // LANES
            for r in range(dil):
                rows = ref[0, r].astype(F32)
                for c in range(slabs):
                    stage[c, pl.ds(r, tm // dil, stride=dil), :] = rows[:, c * LANES:(c + 1) * LANES]
            return jnp.concatenate([stage[c] for c in range(slabs)], axis=1)

        ls = [by_position(l_refs[g], l_stage, dil) for g, (_, dil) in enumerate(DILATED_GROUPS)]
        top = jnp.maximum(jnp.maximum(ls[0], ls[1]), ls[2])
        es = [jnp.exp(l - top) for l in ls]
        den = es[0] + es[1] + es[2]
        oa = None
        for g, (_, dil) in enumerate(DILATED_GROUPS):
            term = (es[g] / den) * by_position(o_refs[g], o_stage, dil)
            oa = term if oa is None else oa + term
        oa = oa.astype(BF16)

    ya = _dot(oa, wua_ref[...])
    yb = _dot(ob_ref[...], wub_ref[...])
    z = (ga_ref[...].astype(F32) * ya + gb_ref[...].astype(F32) * yb).astype(BF16)
    x1 = x_ref[...] + gt1_ref[0] * _dot(z, wo_ref[...])
    x1_ref[...] = x1
    h2 = x1 * lax.rsqrt(jnp.mean(x1 * x1, axis=-1, keepdims=True) + EPS) * g2_ref[...]
    h2 = h2 * (1.0 + sc2_ref[0]) + sh2_ref[0]
    h2_ref[...] = h2

    h_hi = h2.astype(BF16)
    h_lo = (h2 - h_hi.astype(F32)).astype(BF16)
    logit = (_dot(h_hi, wrh_ref[...]) + _dot(h_hi, wrl_ref[...]) + _dot(h_lo, wrh_ref[...])) + br_ref[...]
    lane_i = lax.broadcasted_iota(jnp.int32, (tm, LANES), 1)
    lane = lane_i.astype(F32)
    big = float(LANES)
    is_g = (lane_i >= N_EXPERTS) & (lane_i < N_EXPERTS + MOE_GROUPS)
    gl = jnp.where(is_g, logit, -jnp.inf)
    gmax = jnp.max(gl, axis=-1, keepdims=True)
    g_idx = jnp.min(jnp.where(gl == gmax, lane, big), axis=-1, keepdims=True) - N_EXPERTS
    g_w = 1.0 / jnp.sum(jnp.where(is_g, jnp.exp(logit - gmax), 0.0), axis=-1, keepdims=True)
    lo = g_idx * EXPERTS_PER_GROUP
    el = jnp.where((lane >= lo) & (lane < lo + EXPERTS_PER_GROUP), logit, -jnp.inf)
    t1 = jnp.max(el, axis=-1, keepdims=True)
    i1 = jnp.min(jnp.where(el == t1, lane, big), axis=-1, keepdims=True)
    el2 = jnp.where(lane == i1, -jnp.inf, el)
    t2 = jnp.max(el2, axis=-1, keepdims=True)
    i2 = jnp.min(jnp.where(el2 == t2, lane, big), axis=-1, keepdims=True)
    d = jnp.exp(t2 - t1)
    cw1 = g_w * (1.0 / (1.0 + d))
    cw2 = g_w * (d / (1.0 + d))

    oh1 = lane == i1
    oh2 = lane == i2
    picked = jnp.where(oh1, 1, jnp.where(oh2, 2, 0))
    hit = jnp.minimum(picked, 1)
    row_i = lax.broadcasted_iota(jnp.int32, (tm, LANES), 0)
    seen = hit
    shift = 1
    while shift < tm:
        seen = seen + jnp.where(row_i >= shift, pltpu.roll(seen, shift, axis=0), 0)
        shift *= 2
    run = run_ref[0:1, :]
    code_ref[...] = jnp.where(hit > 0, picked + 4 * (seen - hit + run), 0)
    run_new = jnp.broadcast_to(run + seen[tm - 1:tm, :], run_ref.shape)
    run_ref[...] = run_new
    cnt_ref[...] = run_new

    route_ref[...] = jnp.where(lane_i == 0, cw1, jnp.where(lane_i == 1, cw2, 0.0))


def _stage2(attn_in, ob, ga, gb, x2d, gt1, sh2, sc2, wua, wub, wo, g2, wr_hi, wr_lo, br, *, tm, mod_index, stride_index,
            merged, name):
    n = x2d.shape[0]
    row = lambda i: (i, 0)
    const2 = lambda i: (0, 0)
    scratch = [pltpu.VMEM((8, LANES), jnp.int32)]
    if merged:
        attn_specs = [pl.BlockSpec((tm, D_A), row)]
    else:
        attn_specs = [pl.BlockSpec((1, dil, tm // dil, D_A), stride_index) for _, dil in DILATED_GROUPS]
        attn_specs += [pl.BlockSpec((1, dil, tm // dil, D_A), stride_index) for _, dil in DILATED_GROUPS]
        scratch += [pltpu.VMEM((D_A // LANES, tm, LANES), F32), pltpu.VMEM((D_A // LANES, tm, LANES), F32)]
    in_specs = attn_specs + [
        pl.BlockSpec((tm, D_B), row), pl.BlockSpec((tm, D_MODEL), row), pl.BlockSpec((tm, D_MODEL), row),
        pl.BlockSpec((tm, D_MODEL), row),
        _mod_spec(gt1, mod_index), _mod_spec(sh2, mod_index), _mod_spec(sc2, mod_index),
        pl.BlockSpec((D_A, D_MODEL), const2), pl.BlockSpec((D_B, D_MODEL), const2),
        pl.BlockSpec((D_MODEL, D_MODEL), const2), pl.BlockSpec((1, D_MODEL), const2),
        pl.BlockSpec((D_MODEL, LANES), const2), pl.BlockSpec((D_MODEL, LANES), const2),
        pl.BlockSpec((1, LANES), const2),
    ]
    out_shape = [jax.ShapeDtypeStruct((n, D_MODEL), F32), jax.ShapeDtypeStruct((n, D_MODEL), F32),
                 jax.ShapeDtypeStruct((n, LANES), F32), jax.ShapeDtypeStruct((n, LANES), jnp.int32),
                 jax.ShapeDtypeStruct((8, LANES), jnp.int32)]
    out_specs = [pl.BlockSpec((tm, D_MODEL), row), pl.BlockSpec((tm, D_MODEL), row),
                 pl.BlockSpec((tm, LANES), row), pl.BlockSpec((tm, LANES), row), pl.BlockSpec((8, LANES), const2)]
    return pl.pallas_call(
        functools.partial(_stage2_body, merged=merged),
        out_shape=out_shape, grid=(n // tm,), in_specs=in_specs, out_specs=out_specs,
        scratch_shapes=scratch,
        compiler_params=_params(1), name=name,
    )(*attn_in, ob, ga, gb, x2d, gt1[0], sh2[0], sc2[0], wua, wub, wo, g2, wr_hi, wr_lo, br)


def _dispatch_body(cnt_ref, off_ref, pad_ref, nv_ref, posp_ref, poss_ref, hp_ref, hs_ref, xs_ref,
                   zero_ref, buf_ref, load_sem, scat_sem, zero_sem, *, tiles_p, n_tiles):
    step = pl.program_id(0)
    tile = zero_ref.shape[0]
    groups = buf_ref.shape[1]
    tm = groups * ROW_TILE
    groups_s = hs_ref.shape[0]
    n_s = groups_s * ROW_TILE
    slot = lax.rem(step, 3)
    ahead = lax.rem(step + 1, 3)

    def load_prompt(j, s):
        return pltpu.make_async_copy(hp_ref.at[pl.ds(j * groups, groups)], buf_ref.at[s], load_sem.at[s])

    def load_sample(s):
        return pltpu.make_async_copy(hs_ref, buf_ref.at[s, pl.ds(0, groups_s)], load_sem.at[s])

    def row_copy(s, group, j, dst_row):
        return pltpu.make_async_copy(buf_ref.at[s, group, pl.ds(j, 1)], xs_ref.at[pl.ds(dst_row, 1)], scat_sem.at[s])

    def scatter(rows, pos_ref, s):
        def issue(group, c):
            for j in range(ROW_TILE):
                t = group * ROW_TILE + j
                row_copy(s, group, j, pos_ref[0, 0, 2 * t]).start()
                row_copy(s, group, j, pos_ref[0, 0, 2 * t + 1]).start(priority=1)
            return c

        lax.fori_loop(0, rows // ROW_TILE, issue, 0)

    def drain(rows, s):
        def one(t, c):
            row_copy(s, 0, 0, 0).wait()
            row_copy(s, 0, 0, 0).wait()
            return c

        lax.fori_loop(0, rows, one, 0, unroll=8)

    @pl.when(step == 0)
    def _():
        zero_ref[...] = jnp.zeros_like(zero_ref)
        load_prompt(0, 0).start()
        bits = int(math.log2(tile))

        def pad_copies(e, wait):
            n = pad_ref[e] - cnt_ref[e]
            base = off_ref[e] + cnt_ref[e]
            end = off_ref[e] + pad_ref[e]

            def one_row(i, c):
                copy = _row_copy(zero_ref, 0, xs_ref, base + i, zero_sem)
                if wait:
                    copy.wait()
                else:
                    copy.start()
                return c

            lax.fori_loop(0, jnp.bitwise_and(n, 7), one_row, 0)
            for b in range(3, bits):
                size = 1 << b
                first = pl.multiple_of(end - jnp.bitwise_and(n, ~(2 * size - 1)) - size, 8)
                copy = pltpu.make_async_copy(zero_ref.at[pl.ds(0, size)], xs_ref.at[pl.ds(first, size)], zero_sem)

                @pl.when(jnp.bitwise_and(n, size) != 0)
                def _():
                    if wait:
                        copy.wait()
                    else:
                        copy.start()

        def start_pads(e, c):
            pad_copies(e, False)
            return c

        def wait_pads(e, c):
            pad_copies(e, True)
            return c

        def tile_copy(j):
            return pltpu.make_async_copy(zero_ref, xs_ref.at[pl.ds(j * tile, tile)], zero_sem)

        def start_tile(j, c):
            tile_copy(j).start()
            return c

        def wait_tile(j, c):
            tile_copy(j).wait()
            return c

        lax.fori_loop(0, N_EXPERTS, start_pads, 0)
        lax.fori_loop(nv_ref[0], n_tiles, start_tile, 0)
        lax.fori_loop(0, N_EXPERTS, wait_pads, 0)
        lax.fori_loop(nv_ref[0], n_tiles, wait_tile, 0)

    @pl.when(step >= 2)
    def _():
        drain(tm, ahead)

    @pl.when(step + 1 < tiles_p)
    def _():
        load_prompt(step + 1, ahead).start()

    @pl.when(step + 1 == tiles_p)
    def _():
        load_sample(ahead).start()

    @pl.when(step < tiles_p)
    def _():
        load_prompt(step, slot).wait()
        scatter(tm, posp_ref, slot)

    @pl.when(step == tiles_p)
    def _():
        load_sample(slot).wait()
        scatter(n_s, poss_ref, slot)
        drain(tm, lax.rem(step + 2, 3))
        drain(n_s, slot)


def _dispatch(count, offset, padded, n_valid, pos_p, pos_s, h2_p, h2_s, *, cap, tm, tile):
    n_p, n_s = h2_p.shape[0], h2_s.shape[0]
    tiles_p = n_p // tm
    assert tiles_p >= 2 and n_s <= tm
    last_p = lambda i, *_: (jnp.minimum(i, tiles_p - 1), 0, 0)
    in_specs = [pl.BlockSpec((1, 1, 2 * tm), last_p, memory_space=pltpu.SMEM),
                pl.BlockSpec((1, 1, 2 * n_s), lambda i, *_: (0, 0, 0), memory_space=pltpu.SMEM),
                pl.BlockSpec(memory_space=pl.ANY),
                pl.BlockSpec(memory_space=pl.ANY)]
    grid_spec = pltpu.PrefetchScalarGridSpec(
        num_scalar_prefetch=4, grid=(tiles_p + 1,), in_specs=in_specs,
        out_specs=pl.BlockSpec(memory_space=pl.ANY),
        scratch_shapes=[pltpu.VMEM((tile, D_MODEL), F32), pltpu.VMEM((3, tm // ROW_TILE, ROW_TILE, D_MODEL), F32),
                        pltpu.SemaphoreType.DMA((3,)), pltpu.SemaphoreType.DMA((3,)), pltpu.SemaphoreType.DMA(())],
    )
    by_tile = lambda a: a.reshape(a.shape[0] // ROW_TILE, ROW_TILE, D_MODEL)
    return pl.pallas_call(
        functools.partial(_dispatch_body, tiles_p=tiles_p, n_tiles=cap // tile),
        out_shape=jax.ShapeDtypeStruct((cap, D_MODEL), F32), grid_spec=grid_spec,
        compiler_params=pltpu.CompilerParams(dimension_semantics=("arbitrary",), vmem_limit_bytes=VMEM_LIMIT,
                                             has_side_effects=True),
        name="dispatch",
    )(count, offset, padded, n_valid, pos_p.reshape(tiles_p, 1, 2 * tm), pos_s.reshape(1, 1, 2 * n_s),
      by_tile(h2_p), by_tile(h2_s))


def _moe_body(te_ref, nv_ref, run_ref, nxt_ref, x_ref, wg_hbm, wu_hbm, wd_hbm, y_ref,
              wg_f, wu_f, wd_f, wg_s, wu_s, wd_s, sem):
    i = pl.program_id(0)
    active = i < nv_ref[0]
    fresh = (i == 0) | (te_ref[i] != te_ref[jnp.maximum(i - 1, 0)])
    slot = lax.rem(run_ref[i], 2)

    def fetch(e, s):
        return (pltpu.make_async_copy(wg_hbm.at[e], wg_f.at[s], sem.at[s]),
                pltpu.make_async_copy(wu_hbm.at[e], wu_f.at[s], sem.at[s]),
                pltpu.make_async_copy(wd_hbm.at[e], wd_f.at[s], sem.at[s]))

    @pl.when(active & (i == 0))
    def _():
        for copy in fetch(te_ref[0], 0):
            copy.start()

    @pl.when(active & fresh)
    def _():
        for copy in fetch(te_ref[i], slot):
            copy.wait()

        @pl.when(nxt_ref[i] >= 0)
        def _():
            for copy in fetch(nxt_ref[i], 1 - slot):
                copy.start()

        wg_s[...] = wg_f[slot].astype(BF16)
        wu_s[...] = wu_f[slot].astype(BF16)
        wd_s[...] = wd_f[slot].astype(BF16)

    @pl.when(active)
    def _():
        x = x_ref[...].astype(BF16)
        a = _dot(x, wg_s[...])
        b = _dot(x, wu_s[...])
        mid = (a * _sigmoid(a) * b).astype(BF16)
        y_ref[...] = _dot(mid, wd_s[...])

    @pl.when(jnp.logical_not(active))
    def _():
        y_ref[...] = jnp.zeros_like(y_ref)


def _moe(tile_expert, n_valid, x_sorted, w_eg, w_eu, w_ed, tm):
    cap = x_sorted.shape[0]
    n_tiles = cap // tm
    tile = jnp.arange(n_tiles, dtype=jnp.int32)
    live = tile < n_valid[0]
    starts = jnp.concatenate([jnp.zeros((1,), jnp.bool_), tile_expert[1:] != tile_expert[:-1]]) & live
    run = jnp.cumsum(starts.astype(jnp.int32))
    later = (tile_expert[None, :] > tile_expert[:, None]) & live[None, :]
    nxt = jnp.min(jnp.where(later, tile_expert[None, :], N_EXPERTS), axis=1)
    nxt = jnp.where(nxt < N_EXPERTS, nxt, -1).astype(jnp.int32)

    xmap = lambda i, te, nv, rn, nx: (jnp.maximum(jnp.minimum(i, nv[0] - 1), 0), 0)
    grid_spec = pltpu.PrefetchScalarGridSpec(
        num_scalar_prefetch=4,
        grid=(n_tiles,),
        in_specs=[pl.BlockSpec((tm, D_MODEL), xmap),
                  pl.BlockSpec(memory_space=pl.ANY), pl.BlockSpec(memory_space=pl.ANY),
                  pl.BlockSpec(memory_space=pl.ANY)],
        out_specs=pl.BlockSpec((tm, D_MODEL), lambda i, te, nv, rn, nx: (i, 0)),
        scratch_shapes=[pltpu.VMEM((2, D_MODEL, D_EXPERT), F32), pltpu.VMEM((2, D_MODEL, D_EXPERT), F32),
                        pltpu.VMEM((2, D_EXPERT, D_MODEL), F32),
                        pltpu.VMEM((D_MODEL, D_EXPERT), BF16), pltpu.VMEM((D_MODEL, D_EXPERT), BF16),
                        pltpu.VMEM((D_EXPERT, D_MODEL), BF16), pltpu.SemaphoreType.DMA((2,))],
    )
    return pl.pallas_call(
        _moe_body, out_shape=jax.ShapeDtypeStruct((cap, D_MODEL), F32), grid_spec=grid_spec,
        compiler_params=_params(1), name="moe",
    )(tile_expert, n_valid, run, nxt, x_sorted, w_eg, w_eu, w_ed)


def _final_body(pos_ref, next_ref, x1_ref, route_ref, gt2_ref, gf_ref, ys_ref, o_ref, y_buf, sem):
    step = pl.program_id(0)
    tm = x1_ref.shape[0]
    slot = lax.rem(step, 2)

    def row_copy(s, k, group, j, src_row):
        return pltpu.make_async_copy(ys_ref.at[pl.ds(src_row, 1)], y_buf.at[s, k, group, pl.ds(j, 1)], sem.at[s])

    def gather(p_ref, s):
        def issue(group, c):
            for j in range(ROW_TILE):
                t = group * ROW_TILE + j
                row_copy(s, 0, group, j, p_ref[0, 0, 2 * t]).start()
                row_copy(s, 1, group, j, p_ref[0, 0, 2 * t + 1]).start(priority=1)
            return c

        lax.fori_loop(0, tm // ROW_TILE, issue, 0)

    @pl.when(step == 0)
    def _():
        gather(pos_ref, 0)

    @pl.when(step + 1 < pl.num_programs(0))
    def _():
        gather(next_ref, 1 - slot)

    def drain(t, c):
        row_copy(slot, 0, 0, 0, 0).wait()
        row_copy(slot, 1, 0, 0, 0).wait()
        return c

    lax.fori_loop(0, tm, drain, 0, unroll=8)

    route = route_ref[...]
    moe = (route[:, 0:1] * y_buf[slot, 0].reshape(tm, D_MODEL)
           + route[:, 1:2] * y_buf[slot, 1].reshape(tm, D_MODEL))
    x = x1_ref[...] + gt2_ref[0] * moe
    o_ref[...] = x * lax.rsqrt(jnp.mean(x * x, axis=-1, keepdims=True) + EPS) * gf_ref[...]


def _final(pos, x1, route, gt2, gf, y_sorted, *, tm, mod_index, name):
    n = x1.shape[0]
    steps = n // tm
    row = lambda i: (i, 0)
    pos3 = pos.reshape(steps, 1, 2 * tm)
    return pl.pallas_call(
        _final_body,
        out_shape=jax.ShapeDtypeStruct((n, D_MODEL), F32),
        grid=(steps,),
        in_specs=[pl.BlockSpec((1, 1, 2 * tm), lambda i: (i, 0, 0), memory_space=pltpu.SMEM),
                  pl.BlockSpec((1, 1, 2 * tm), lambda i: (jnp.minimum(i + 1, steps - 1), 0, 0),
                               memory_space=pltpu.SMEM),
                  pl.BlockSpec((tm, D_MODEL), row), pl.BlockSpec((tm, LANES), row),
                  _mod_spec(gt2, mod_index), pl.BlockSpec((1, D_MODEL), lambda i: (0, 0)),
                  pl.BlockSpec(memory_space=pl.ANY)],
        out_specs=pl.BlockSpec((tm, D_MODEL), row),
        scratch_shapes=[pltpu.VMEM((2, 2, tm // ROW_TILE, ROW_TILE, D_MODEL), F32), pltpu.SemaphoreType.DMA((2,))],
        compiler_params=_params(1), name=name,
    )(pos3, pos3, x1, route, gt2[0], gf, y_sorted)


def _t5_bucket(dist):
    max_exact = N_BUCKETS // 2
    dist = np.asarray(dist)
    d = np.maximum(dist, 1).astype(np.float32)
    large = max_exact + (np.log(d / np.float32(max_exact)) / np.float32(math.log(MAX_DISTANCE / max_exact))
                         * np.float32(N_BUCKETS - max_exact)).astype(np.int32)
    return np.where(dist < max_exact, dist, np.minimum(large, N_BUCKETS - 1)).astype(np.int32)


def _bias_lookup(tab, dist):
    onehot = (jnp.asarray(_t5_bucket(dist))[..., None] == jnp.arange(N_BUCKETS)).astype(F32)
    return jnp.einsum("...b,bh->...h", onehot, tab, precision=HIGHEST)


def _prompt_bias(tab, dil):
    delta = np.arange(BLOCK)[:, None] + BLOCK - np.arange(2 * BLOCK)[None, :]
    return jnp.transpose(_bias_lookup(tab, np.maximum(delta, 0) * dil), (2, 0, 1))


def _sample_bias(tab, win, dil, t_new):
    cols = win + LANES
    back = win + np.arange(t_new)[:, None] - np.arange(cols)[None, :]
    bias = _bias_lookup(tab, np.maximum(back, 0))
    return jnp.transpose(bias, (2, 0, 1)).reshape(HEADS * t_new, cols)


def kernel(x_prompt, x_sample, c_prompt, c_sample, cache_kv_w128, cache_kv_w512, cache_kv_w2048, rel_bias, w_ada,
           b_ada, g_norm1, w_in, ln_v_g, ln_v_b, w_spatial, b_spatial, w_up_a, w_up_b, w_out, g_norm2,
           w_route_group, b_route_group, w_route_expert, b_route_expert, w_e_gate, w_e_up, w_e_down, g_final):
    assert w_ada.shape[0] == 1, "single layer"
    bp, s_len, _ = x_prompt.shape
    bs, t_new, _ = x_sample.shape
    n_p, n_s = bp * s_len, bs * t_new
    caches = (cache_kv_w128[0], cache_kv_w512[0], cache_kv_w2048[0])
    assert all(c.shape[1] == win for c, (win, _) in zip(caches, DILATED_GROUPS)), "cache holds one full window"

    mod = _ada(jnp.concatenate([c_prompt, c_sample], axis=0), w_ada[0], b_ada[0])
    mod_p = [(mod[:bp].reshape(bp, 1, 6 * D_MODEL), j) for j in range(6)]
    mod_s = [(jnp.repeat(mod[bp:], t_new, axis=0).reshape(1, n_s, 6 * D_MODEL), j) for j in range(6)]

    w_in_bf = w_in[0].astype(BF16)
    g1 = g_norm1[0].reshape(1, D_MODEL)
    lng, lnb = ln_v_g[0].reshape(1, D_B), ln_v_b[0].reshape(1, D_B)
    ws_p = w_spatial[0]
    bs_p = jnp.repeat(jnp.transpose(b_spatial[0]), SGU_DIM, axis=1)
    reps = n_s // t_new
    ws_s = jnp.tile(w_spatial[0][:, :t_new, :t_new], (1, reps, reps))
    bs_s = jnp.tile(bs_p[:t_new], (reps, 1))

    tpb = s_len // TM_STAGE1
    tail_p = [(bp, min(win, s_len)) for win, _ in DILATED_GROUPS]

    def tail_index_p(i, g, rows):
        first = tpb - tail_p[g][1] // rows
        return (i // tpb, jnp.maximum(i % tpb - first, 0), 0)

    q_s, kv0_s, kv1_s, kv2_s, ob_s, ga_s, gb_s, vn_s = _stage1(
        x_sample.reshape(n_s, D_MODEL), mod_s[0], mod_s[1], g1, w_in_bf, lng, lnb, ws_s, bs_s,
        tm=n_s, mod_index=lambda i: (0, 0, 0), stride_index=None, tail_shapes=[(1, n_s)] * N_DIL,
        tail_index=lambda i, g, rows: (0, 0, 0),
        period=t_new, by_stride=False, emit_vn=True, name="stage1_sample")

    tabs = [rel_bias[:, g * HEADS:(g + 1) * HEADS].astype(F32) for g in range(N_DIL)]
    sample_bias = [_sample_bias(tabs[g], win, dil, t_new) for g, (win, dil) in enumerate(DILATED_GROUPS)]
    caches_t = [jnp.transpose(c, (0, 2, 3, 4, 1)).reshape(bs, 2, D_A, c.shape[1]) for c in caches]
    outs = _stage1(
        x_prompt.reshape(n_p, D_MODEL), mod_p[0], mod_p[1], g1, w_in_bf, lng, lnb, ws_p, bs_p,
        tm=TM_STAGE1, mod_index=lambda i: (i // tpb, 0, 0),
        stride_index=(bp, s_len, lambda i: (i // tpb, 0, i % tpb, 0)), tail_shapes=tail_p, tail_index=tail_index_p,
        period=CHUNK, by_stride=True, emit_vn=False, name="stage1_prompt",
        cache=(q_s, (kv0_s, kv1_s, kv2_s), caches_t, sample_bias, t_new))
    q_g, k_g, v_g = outs[0:3], outs[3:6], outs[6:9]
    kv0_p, kv1_p, kv2_p, ob_p, ga_p, gb_p, cache0_t, cache1_t, cache2_t, oa_s = outs[9:]

    o_groups, lse_groups = [], []
    for g, (win, dil) in enumerate(DILATED_GROUPS):
        o, lse = _attn_prompt_group(q_g[g], k_g[g], v_g[g], _prompt_bias(tabs[g], dil) * LOG2E, dil,
                                    min(ATTN_BLOCKS_PER_STEP, s_len // dil // BLOCK))
        o_groups.append(o)
        lse_groups.append(lse)

    wua, wub, wo = w_up_a[0].astype(BF16), w_up_b[0].astype(BF16), w_out[0].astype(BF16)
    g2 = g_norm2[0].reshape(1, D_MODEL)
    wr = jnp.concatenate([w_route_expert[0].reshape(D_MODEL, N_EXPERTS), w_route_group[0]], axis=1)
    wr = jnp.pad(wr, ((0, 0), (0, LANES - wr.shape[1])))
    br = jnp.concatenate([b_route_expert[0].reshape(N_EXPERTS), b_route_group[0]])
    br = jnp.pad(br, (0, LANES - br.shape[0])).reshape(1, LANES)
    wr_hi = wr.astype(BF16)
    wr_lo = (wr - wr_hi.astype(F32)).astype(BF16)

    tpb2 = s_len // TM_STAGE2
    x1_p, h2_p, route_p, code_p, cnt_p = _stage2(
        o_groups + lse_groups, ob_p, ga_p, gb_p, x_prompt.reshape(n_p, D_MODEL),
        mod_p[2], mod_p[3], mod_p[4], wua, wub, wo, g2, wr_hi, wr_lo, br,
        tm=TM_STAGE2, mod_index=lambda i: (i // tpb2, 0, 0), stride_index=lambda i: (i // tpb2, 0, i % tpb2, 0),
        merged=False, name="stage2_prompt")
    x1_s, h2_s, route_s, code_s, cnt_s = _stage2(
        [oa_s], ob_s, ga_s, gb_s, x_sample.reshape(n_s, D_MODEL),
        mod_s[2], mod_s[3], mod_s[4], wua, wub, wo, g2, wr_hi, wr_lo, br,
        tm=n_s, mod_index=lambda i: (0, 0, 0), stride_index=None, merged=True, name="stage2_sample")

    cap = 2 * (n_p + n_s) + N_EXPERTS * TM_MOE
    n_tiles = cap // TM_MOE
    count_p = cnt_p[0, :N_EXPERTS]
    count = count_p + cnt_s[0, :N_EXPERTS]
    padded = ((count + TM_MOE - 1) // TM_MOE) * TM_MOE
    ends = jnp.cumsum(padded)
    offset = ends - padded
    n_valid = (ends[-1] // TM_MOE).astype(jnp.int32).reshape(1)
    tile_start = jnp.arange(n_tiles, dtype=jnp.int32) * TM_MOE
    tile_expert = jnp.minimum(jnp.sum(ends[None, :] <= tile_start[:, None], axis=1), N_EXPERTS - 1).astype(jnp.int32)

    def sorted_rows(code, earlier):
        row = jnp.pad(offset + earlier, (0, LANES - N_EXPERTS))[None, :] + jnp.right_shift(code, 2)
        return jnp.stack([jnp.sum(jnp.where((code & 3) == k, row, 0), axis=1) for k in (1, 2)], axis=1)

    pos_p = sorted_rows(code_p, jnp.zeros_like(count_p))
    pos_s = sorted_rows(code_s, count_p)
    x_sorted = _dispatch(count, offset, padded, n_valid, pos_p, pos_s, h2_p, h2_s, cap=cap, tm=TM_FINAL, tile=TM_MOE)

    y_sorted = _moe(tile_expert, n_valid, x_sorted, w_e_gate[0], w_e_up[0], w_e_down[0], TM_MOE)

    gf = g_final.reshape(1, D_MODEL)
    tpb3 = s_len // TM_FINAL
    y_p = _final(pos_p, x1_p, route_p, mod_p[5], gf, y_sorted,
                 tm=TM_FINAL, mod_index=lambda i: (i // tpb3, 0, 0), name="final_prompt")
    y_s = _final(pos_s, x1_s, route_s, mod_s[5], gf, y_sorted,
                 tm=n_s, mod_index=lambda i: (0, 0, 0), name="final_sample")

    kv_p = [a.reshape(1, bp, a.shape[1], 2, HEADS, HEAD_DIM) for a in (kv0_p, kv1_p, kv2_p)]
    kv_s = [jnp.transpose(c.reshape(bs, 2, HEADS, HEAD_DIM, c.shape[-1]), (0, 4, 1, 2, 3))[None]
            for c in (cache0_t, cache1_t, cache2_t)]
    return (y_p.reshape(bp, s_len, D_MODEL), y_s.reshape(bs, t_new, D_MODEL),
            kv_p[0], kv_p[1], kv_p[2], kv_s[0], kv_s[1], kv_s[2],
            vn_s.reshape(1, bs, t_new, D_B))
```

```python
import functools
import math

import numpy as np
import jax
import jax.numpy as jnp
from jax import lax
from jax.experimental import pallas as pl
from jax.experimental.pallas import tpu as pltpu

F32 = jnp.float32
BF16 = jnp.bfloat16
HIGHEST = lax.Precision.HIGHEST

D_MODEL = 1024
HEAD_DIM = 64
HEADS = 8
DILATED_GROUPS = ((128, 1), (512, 4), (2048, 16))
N_DIL = 3
D_A = HEADS * HEAD_DIM
D_QKV = N_DIL * D_A
BLOCK = 128
STEPS = 128
SCALE = HEAD_DIM ** -0.5
CHUNK = 128
D_B = 512
SGU_GROUPS = 4
SGU_DIM = D_B // SGU_GROUPS
N_BUCKETS = 32
MAX_DISTANCE = 2048
MOE_GROUPS = 4
EXPERTS_PER_GROUP = 8
N_EXPERTS = 32
D_EXPERT = 512
EPS = 1e-6
NEG_INF = -1e30
LOG2E = math.log2(math.e)
LN2 = math.log(2.0)
LANES = 128
ROW_TILE = 8
C_U = 3 * D_QKV
C_V = C_U + D_B
C_GA = C_V + D_B
C_GB = C_GA + D_MODEL
IN_COLS = C_GB + D_MODEL

TM_STAGE1 = 256
TM_STAGE2 = 512
TM_MOE = 512
ATTN_BLOCKS_PER_STEP = 8
TM_FINAL = 512
VMEM_LIMIT = 56 * 1024 * 1024


def _dot(a, b):
    return jnp.dot(a, b, preferred_element_type=F32)


def _dot_nt(a, b):
    return lax.dot_general(a, b, (((1,), (1,)), ((), ())), preferred_element_type=F32)


def _sigmoid(x):
    return 1.0 / (1.0 + jnp.exp(-x))


def _gelu(x):
    return 0.5 * x * (1.0 + lax.erf(x * (2.0 ** -0.5)))


def _params(n_grid):
    return pltpu.CompilerParams(dimension_semantics=("arbitrary",) * n_grid, vmem_limit_bytes=VMEM_LIMIT)


def _mod_spec(vectors_col, batch_index):
    vectors, col = vectors_col
    return pl.BlockSpec((1, vectors.shape[1], D_MODEL), lambda i: batch_index(i)[:2] + (col,))


def _row_copy(src, src_row, dst, dst_row, sem):
    return pltpu.make_async_copy(src.at[pl.ds(src_row, 1)], dst.at[pl.ds(dst_row, 1)], sem)


def _ada_body(c_ref, w_ref, b_ref, o_ref):
    c = c_ref[...]
    s = c * _sigmoid(c)
    o_ref[...] = jnp.dot(s, w_ref[...], precision=HIGHEST, preferred_element_type=F32) + b_ref[...]


def _ada(c_all, w_ada, b_ada):
    n, d = c_all.shape
    cols = w_ada.shape[1]
    tn = 1024
    return pl.pallas_call(
        _ada_body,
        out_shape=jax.ShapeDtypeStruct((n, cols), F32),
        grid=(cols // tn,),
        in_specs=[pl.BlockSpec((n, d), lambda j: (0, 0)),
                  pl.BlockSpec((d, tn), lambda j: (0, j)),
                  pl.BlockSpec((1, tn), lambda j: (0, j))],
        out_specs=pl.BlockSpec((n, tn), lambda j: (0, j)),
        compiler_params=_params(1),
        name="ada",
    )(c_all, w_ada, b_ada.reshape(1, cols))


def _stage1_body(x_ref, sh_ref, sc_ref, g1_ref, w_ref, lng_ref, lnb_ref, ws_ref, bs_ref, *rest,
                 period, by_stride, emit_vn, cache_t_new):
    rest = list(rest)
    cache_refs = [rest.pop(0) for _ in range(1 + 3 * N_DIL)] if cache_t_new else []
    if by_stride:
        qkv_refs = [[rest.pop(0) for _ in range(N_DIL)] for _ in range(3)]
    else:
        q_ref = rest.pop(0)
    kv_refs = [rest.pop(0) for _ in range(N_DIL)]
    ob_ref, ga_ref, gb_ref = rest.pop(0), rest.pop(0), rest.pop(0)
    vn_ref = rest.pop(0) if emit_vn else None
    if cache_t_new:
        cache_refs += [rest.pop(0) for _ in range(N_DIL + 1)]
    stage_ref = rest.pop(0) if by_stride else None
    if cache_t_new:
        _sample_cache_step(pl.program_id(0), cache_refs + rest, cache_t_new)

    x = x_ref[...]
    tm = x.shape[0]
    h = x * lax.rsqrt(jnp.mean(x * x, axis=-1, keepdims=True) + EPS) * g1_ref[...]
    h = h * (1.0 + sc_ref[0]) + sh_ref[0]
    hb = h.astype(BF16)

    q = _dot(hb, w_ref[:, 0:D_QKV]) * (SCALE * LOG2E if by_stride else SCALE)
    k = _dot(hb, w_ref[:, D_QKV:2 * D_QKV])
    v = _dot(hb, w_ref[:, 2 * D_QKV:3 * D_QKV])
    if by_stride:
        per_group = D_A // LANES
        for val, refs in zip((q, k, v), qkv_refs):
            for c in range(D_QKV // LANES):
                stage_ref[c] = val[:, c * LANES:(c + 1) * LANES]
            for g, (_, dil) in enumerate(DILATED_GROUPS):
                for r in range(dil):
                    for c in range(per_group):
                        piece = stage_ref[g * per_group + c, pl.ds(r, tm // dil, stride=dil), :]
                        refs[g][0, r, :, c * LANES:(c + 1) * LANES] = piece.astype(BF16)
    else:
        q_ref[...] = q
    for g in range(N_DIL):
        rows = kv_refs[g].shape[1]
        kv_refs[g][0, :, 0:D_A] = k[tm - rows:, g * D_A:(g + 1) * D_A]
        kv_refs[g][0, :, D_A:2 * D_A] = v[tm - rows:, g * D_A:(g + 1) * D_A]

    ga_ref[...] = _sigmoid(_dot(hb, w_ref[:, C_GA:C_GB])).astype(BF16)
    gb_ref[...] = _sigmoid(_dot(hb, w_ref[:, C_GB:IN_COLS])).astype(BF16)

    u = _gelu(_dot(hb, w_ref[:, C_U:C_V]))
    vb = _gelu(_dot(hb, w_ref[:, C_V:C_GA]))
    xc = vb - jnp.mean(vb, axis=-1, keepdims=True)
    vn = xc * lax.rsqrt(jnp.mean(xc * xc, axis=-1, keepdims=True) + EPS) * lng_ref[...] + lnb_ref[...]
    if emit_vn:
        vn_ref[...] = vn
    vnb = vn.astype(BF16)

    r = ws_ref.shape[1]
    ri = lax.broadcasted_iota(jnp.int32, (r, r), 0)
    ci = lax.broadcasted_iota(jnp.int32, (r, r), 1)
    keep = ci <= ri
    if period != r:
        sh = int(math.log2(period))
        keep = keep & (jnp.right_shift(ri, sh) == jnp.right_shift(ci, sh))
    for g in range(SGU_GROUPS):
        wg = jnp.where(keep, ws_ref[g], 0.0).astype(BF16)
        cs = slice(g * SGU_DIM, (g + 1) * SGU_DIM)
        for c in range(tm // r):
            rs = slice(c * r, (c + 1) * r)
            mixed = _dot(wg, vnb[rs, cs]) + bs_ref[:, cs]
            ob_ref[rs, cs] = (u[rs, cs] * mixed).astype(BF16)


def _stage1(x2d, sh, sc, g1, w_in_bf, lng, lnb, ws, bs, *, tm, mod_index, stride_index, tail_shapes, tail_index,
            period, by_stride, emit_vn, name, cache=None):
    n = x2d.shape[0]
    grid = (n // tm,)
    r = ws.shape[1]
    const2 = lambda i: (0, 0)
    row = lambda i: (i, 0)
    in_specs = [
        pl.BlockSpec((tm, D_MODEL), row),
        _mod_spec(sh, mod_index),
        _mod_spec(sc, mod_index),
        pl.BlockSpec((1, D_MODEL), const2),
        pl.BlockSpec((D_MODEL, IN_COLS), const2, pipeline_mode=pl.Buffered(1)),
        pl.BlockSpec((1, D_B), const2),
        pl.BlockSpec((1, D_B), const2),
        pl.BlockSpec((SGU_GROUPS, r, r), lambda i: (0, 0, 0)),
        pl.BlockSpec((r, D_B), const2),
    ]
    out_shape, out_specs, scratch = [], [], []
    if by_stride:
        n_batch, s_len = stride_index[0], stride_index[1]
        for _ in range(3):
            for _, dil in DILATED_GROUPS:
                out_shape.append(jax.ShapeDtypeStruct((n_batch, dil, s_len // dil, D_A), BF16))
                out_specs.append(pl.BlockSpec((1, dil, tm // dil, D_A), stride_index[2]))
        scratch.append(pltpu.VMEM((D_QKV // LANES, tm, LANES), F32))
    else:
        out_shape.append(jax.ShapeDtypeStruct((n, D_QKV), F32))
        out_specs.append(pl.BlockSpec((tm, D_QKV), row))
    for g in range(N_DIL):
        nb, keep = tail_shapes[g]
        rows = min(tm, keep)
        out_shape.append(jax.ShapeDtypeStruct((nb, keep, 2 * D_A), F32))
        out_specs.append(pl.BlockSpec((1, rows, 2 * D_A), functools.partial(tail_index, g=g, rows=rows)))
    out_shape += [jax.ShapeDtypeStruct((n, D_B), BF16),
                  jax.ShapeDtypeStruct((n, D_MODEL), BF16),
                  jax.ShapeDtypeStruct((n, D_MODEL), BF16)]
    out_specs += [pl.BlockSpec((tm, D_B), row), pl.BlockSpec((tm, D_MODEL), row), pl.BlockSpec((tm, D_MODEL), row)]
    if emit_vn:
        out_shape.append(jax.ShapeDtypeStruct((n, D_B), F32))
        out_specs.append(pl.BlockSpec((tm, D_B), row))
    args = [x2d, sh[0], sc[0], g1, w_in_bf, lng, lnb, ws, bs]
    cache_t_new = 0
    if cache is not None:
        c_args, c_in, c_shape, c_out, c_scratch, units = _sample_cache_specs(*cache)
        assert units == grid[0], "one sample-cache unit per projection tile"
        args += c_args
        in_specs += c_in
        out_shape += c_shape
        out_specs += c_out
        scratch += c_scratch
        cache_t_new = cache[-1]
    body = functools.partial(_stage1_body, period=period, by_stride=by_stride, emit_vn=emit_vn,
                             cache_t_new=cache_t_new)
    return pl.pallas_call(
        body, out_shape=out_shape, grid=grid, in_specs=in_specs, out_specs=out_specs, scratch_shapes=scratch,
        compiler_params=_params(1), name=name,
    )(*args)


def _attn_prompt_body(q_ref, kp_ref, kc_ref, vp_ref, vc_ref, bias_ref, o_ref, lse_ref):
    n = pl.program_id(2)
    n_sub = q_ref.shape[2] // BLOCK
    ri = lax.broadcasted_iota(jnp.int32, (BLOCK, 2 * BLOCK), 0)
    ci = lax.broadcasted_iota(jnp.int32, (BLOCK, 2 * BLOCK), 1)
    delta = ri + BLOCK - ci
    band = (delta >= 0) & (delta <= STEPS)
    first_key = jnp.where(n > 0, 0, BLOCK)
    low_q = lax.broadcasted_iota(jnp.int32, (BLOCK, LANES), 1) < HEAD_DIM
    low_v = lax.broadcasted_iota(jnp.int32, (2 * BLOCK, LANES), 1) < HEAD_DIM
    ones_lo = jnp.where(low_v, 1.0, 0.0).astype(BF16)
    ones_hi = jnp.where(low_v, 0.0, 1.0).astype(BF16)
    for sub in range(n_sub):
        rows = slice(sub * BLOCK, (sub + 1) * BLOCK)
        valid = band & (ci >= first_key) if sub == 0 else band
        for pair in range(HEADS // 2):
            cs = slice(pair * LANES, (pair + 1) * LANES)
            qp = q_ref[0, 0, rows, cs]
            if sub == 0:
                kp = jnp.concatenate([kp_ref[0, 0, :, cs], kc_ref[0, 0, 0:BLOCK, cs]], axis=0)
                vp = jnp.concatenate([vp_ref[0, 0, :, cs], vc_ref[0, 0, 0:BLOCK, cs]], axis=0)
            else:
                kp = kc_ref[0, 0, (sub - 1) * BLOCK:(sub + 1) * BLOCK, cs]
                vp = vc_ref[0, 0, (sub - 1) * BLOCK:(sub + 1) * BLOCK, cs]
            acc, tops = None, []
            for half in range(2):
                own_q = low_q if half == 0 else jnp.logical_not(low_q)
                own_v = low_v if half == 0 else jnp.logical_not(low_v)
                s = _dot_nt(jnp.where(own_q, qp, jnp.zeros_like(qp)), kp) + bias_ref[2 * pair + half]
                s = jnp.where(valid, s, NEG_INF)
                m = jnp.max(s, axis=-1, keepdims=True)
                p = jnp.exp2(s - m).astype(BF16)
                w = jnp.concatenate([jnp.where(own_v, vp, jnp.zeros_like(vp)), ones_lo if half == 0 else ones_hi],
                                    axis=1)
                r = _dot(p, w)
                acc = r if acc is None else acc + r
                tops.append(m)
            den = acc[:, LANES:]
            o_ref[0, 0, rows, cs] = (acc[:, :LANES] / den).astype(BF16)
            lse_ref[0, 0, rows, cs] = jnp.where(low_q, tops[0], tops[1]) * LN2 + jnp.log(den)


def _attn_prompt_group(q4, k4, v4, bias, dil, n_sub):
    b_sz, _, m_len, _ = q4.shape
    span = n_sub * BLOCK
    cur = pl.BlockSpec((1, 1, span, D_A), lambda b, r, n: (b, r, n, 0))
    prev = pl.BlockSpec((1, 1, BLOCK, D_A), lambda b, r, n: (b, r, jnp.maximum(n * n_sub - 1, 0), 0))
    return pl.pallas_call(
        _attn_prompt_body,
        out_shape=[jax.ShapeDtypeStruct((b_sz, dil, m_len, D_A), BF16),
                   jax.ShapeDtypeStruct((b_sz, dil, m_len, D_A), F32)],
        grid=(b_sz, dil, m_len // span),
        in_specs=[cur, prev, cur, prev, cur,
                  pl.BlockSpec((HEADS, BLOCK, 2 * BLOCK), lambda b, r, n: (0, 0, 0))],
        out_specs=[cur, cur],
        compiler_params=_params(3),
        name=f"attn_prompt_d{dil}",
    )(q4, k4, k4, v4, v4, bias)


def _sample_cache_step(unit, refs, t_new):
    refs = list(refs)
    q_ref = refs.pop(0)
    new_refs = [refs.pop(0) for _ in range(N_DIL)]
    cache_refs = [refs.pop(0) for _ in range(N_DIL)]
    bias_refs = [refs.pop(0) for _ in range(N_DIL)]
    out_refs = [refs.pop(0) for _ in range(N_DIL)]
    o_ref = refs.pop(0)
    p_refs = [refs.pop(0) for _ in range(N_DIL)]
    scale_ref = refs.pop(0)
    kv = lax.rem(unit, 2)
    rows = HEADS * t_new

    def allowed(g, col0, cols):
        win, dil = DILATED_GROUPS[g]
        tok = jnp.bitwise_and(lax.broadcasted_iota(jnp.int32, (rows, cols), 0), t_new - 1)
        col = lax.broadcasted_iota(jnp.int32, (rows, cols), 1) + col0
        back = win + tok - col
        return ((col < win + t_new) & (back >= 0) & (jnp.bitwise_and(back, dil - 1) == 0)
                & (back <= STEPS * dil))

    lane = lax.broadcasted_iota(jnp.int32, (D_A, LANES), 1)
    front = jnp.zeros((LANES - t_new, D_A), F32)
    for g in range(N_DIL):
        win = cache_refs[g].shape[-1]
        new = new_refs[g][0]
        half = jnp.where(kv == 0, new[:, 0:D_A], new[:, D_A:2 * D_A])
        tail = jnp.transpose(jnp.concatenate([front, half], axis=0))
        rolled = pltpu.roll(cache_refs[g][0, 0], win - t_new, axis=1)
        if win > LANES:
            out_refs[g][0, 0, :, 0:win - LANES] = rolled[:, 0:win - LANES]
        out_refs[g][0, 0, :, win - LANES:win] = jnp.where(lane >= LANES - t_new, tail, rolled[:, win - LANES:win])

    ri = lax.broadcasted_iota(jnp.int32, (rows, D_A), 0)
    ci = lax.broadcasted_iota(jnp.int32, (rows, D_A), 1)
    own_head = jnp.right_shift(ri, int(math.log2(t_new))) == jnp.right_shift(ci, int(math.log2(HEAD_DIM)))
    pad = jnp.zeros((LANES - t_new, D_A), F32)
    glane = lax.broadcasted_iota(jnp.int32, (rows, LANES), 1)

    @pl.when(kv == 0)
    def _():
        q = q_ref[...]
        stats = []
        for g in range(N_DIL):
            win = cache_refs[g].shape[-1]
            qg = q[:, g * D_A:(g + 1) * D_A]
            qblk = jnp.where(own_head, jnp.concatenate([qg] * HEADS, axis=0), 0.0).astype(BF16)
            s_c = _dot(qblk, cache_refs[g][0, 0].astype(BF16)) + bias_refs[g][:, 0:win]
            s_c = jnp.where(allowed(g, 0, win), s_c, NEG_INF)
            k_new = jnp.concatenate([new_refs[g][0][:, 0:D_A], pad], axis=0).astype(BF16)
            s_n = _dot_nt(qblk, k_new) + bias_refs[g][:, win:]
            s_n = jnp.where(allowed(g, win, LANES), s_n, NEG_INF)
            m = jnp.maximum(jnp.max(s_c, axis=-1, keepdims=True), jnp.max(s_n, axis=-1, keepdims=True))
            p_c = jnp.exp(s_c - m)
            p_n = jnp.exp(s_n - m)
            l = jnp.sum(p_c, axis=-1, keepdims=True) + jnp.sum(p_n, axis=-1, keepdims=True)
            p_refs[g][:, 0:win] = p_c.astype(BF16)
            p_refs[g][:, win:] = p_n.astype(BF16)
            stats.append((l, m + jnp.log(l)))
        top = jnp.maximum(jnp.maximum(stats[0][1], stats[1][1]), stats[2][1])
        es = [jnp.exp(lse - top) for _, lse in stats]
        den = es[0] + es[1] + es[2]
        scale = jnp.zeros((rows, LANES), F32)
        for g in range(N_DIL):
            scale = jnp.where(glane == g, es[g] / den / stats[g][0], scale)
        scale_ref[...] = scale

    @pl.when(kv == 1)
    def _():
        acc = jnp.zeros((rows, D_A), F32)
        for g in range(N_DIL):
            win = cache_refs[g].shape[-1]
            part = _dot_nt(p_refs[g][:, 0:win], cache_refs[g][0, 0].astype(BF16))
            v_new = jnp.concatenate([new_refs[g][0][:, D_A:2 * D_A], pad], axis=0).astype(BF16)
            part = part + _dot(p_refs[g][:, win:], v_new)
            acc = acc + scale_ref[:, g:g + 1] * part
        acc = jnp.where(own_head, acc, 0.0)
        out = acc[0:t_new]
        for h in range(1, HEADS):
            out = out + acc[h * t_new:(h + 1) * t_new]
        o_ref[...] = out


def _sample_cache_specs(q_s, kv_new, caches_t, biases, t_new):
    n = q_s.shape[0]
    nb = n // t_new
    blk4 = lambda i: (i // 2, lax.rem(i, 2), 0, 0)
    in_specs = [pl.BlockSpec((t_new, D_QKV), lambda i: (i // 2, 0))]
    in_specs += [pl.BlockSpec((1, t_new, 2 * D_A), lambda i: (i // 2, 0, 0))] * N_DIL
    in_specs += [pl.BlockSpec((1, 1, D_A, c.shape[-1]), blk4) for c in caches_t]
    in_specs += [pl.BlockSpec(a.shape, lambda i: (0, 0), pipeline_mode=pl.Buffered(1)) for a in biases]
    out_shape = [jax.ShapeDtypeStruct(c.shape, c.dtype) for c in caches_t]
    out_specs = [pl.BlockSpec((1, 1, D_A, c.shape[-1]), blk4) for c in caches_t]
    out_shape.append(jax.ShapeDtypeStruct((n, D_A), F32))
    out_specs.append(pl.BlockSpec((t_new, D_A), lambda i: (i // 2, 0)))
    rows = HEADS * t_new
    scratch = [pltpu.VMEM((rows, c.shape[-1] + LANES), BF16) for c in caches_t]
    scratch.append(pltpu.VMEM((rows, LANES), F32))
    args = [q_s, *[a.reshape(nb, t_new, 2 * D_A) for a in kv_new], *caches_t, *biases]
    return args, in_specs, out_shape, out_specs, scratch, 2 * nb


def _stage2_body(*refs, merged):
    refs = list(refs)
    if merged:
        oa_ref = refs.pop(0)
    else:
        o_refs = [refs.pop(0) for _ in range(N_DIL)]
        l_refs = [refs.pop(0) for _ in range(N_DIL)]
    (ob_ref, ga_ref, gb_ref, x_ref, gt1_ref, sh2_ref, sc2_ref, wua_ref, wub_ref, wo_ref, g2_ref,
     wr_ref, br_ref, x1_ref, h2_ref, route_ref, code_ref, cnt_ref, run_ref) = refs[:19]
    step = pl.program_id(0)
    tm = x_ref.shape[0]

    @pl.when(step == 0)
    def _():
        run_ref[...] = jnp.zeros_like(run_ref)

    if merged:
        oa = oa_ref[...].astype(BF16)
    else:
        o_stage, l_stage = refs[19], refs[20]

        def by_position(ref, stage, dil):
            if dil == 1:
                return ref[0, 0].astype(F32)
            slabs = ref.shape[-1] // LANES
            for r in range(dil):
                rows = ref[0, r].astype(F32)
                for c in range(slabs):
                    stage[c, pl.ds(r, tm // dil, stride=dil), :] = rows[:, c * LANES:(c + 1) * LANES]
            return jnp.concatenate([stage[c] for c in range(slabs)], axis=1)

        ls = [by_position(l_refs[g], l_stage, dil) for g, (_, dil) in enumerate(DILATED_GROUPS)]
        top = jnp.maximum(jnp.maximum(ls[0], ls[1]), ls[2])
        es = [jnp.exp(l - top) for l in ls]
        den = es[0] + es[1] + es[2]
        oa = None
        for g, (_, dil) in enumerate(DILATED_GROUPS):
            term = es[g] * by_position(o_refs[g], o_stage, dil)
            oa = term if oa is None else oa + term
        oa = (oa / den).astype(BF16)

    ya = _dot(oa, wua_ref[...])
    yb = _dot(ob_ref[...], wub_ref[...])
    z = (ga_ref[...].astype(F32) * ya + gb_ref[...].astype(F32) * yb).astype(BF16)
    x1 = x_ref[...] + gt1_ref[0] * _dot(z, wo_ref[...])
    x1_ref[...] = x1
    h2 = x1 * lax.rsqrt(jnp.mean(x1 * x1, axis=-1, keepdims=True) + EPS) * g2_ref[...]
    h2 = h2 * (1.0 + sc2_ref[0]) + sh2_ref[0]
    h2_ref[...] = h2

    h_hi = h2.astype(BF16)
    h_lo = (h2 - h_hi.astype(F32)).astype(BF16)
    both = _dot(h_hi, wr_ref[...])
    logit = (both[:, 0:LANES] + both[:, LANES:] + _dot(h_lo, wr_ref[:, 0:LANES])) + br_ref[...]
    lane_i = lax.broadcasted_iota(jnp.int32, (tm, LANES), 1)
    lane = lane_i.astype(F32)
    big = float(LANES)
    is_g = (lane_i >= N_EXPERTS) & (lane_i < N_EXPERTS + MOE_GROUPS)
    gl = jnp.where(is_g, logit, -jnp.inf)
    gmax = jnp.max(gl, axis=-1, keepdims=True)
    g_idx = jnp.min(jnp.where(gl == gmax, lane, big), axis=-1, keepdims=True) - N_EXPERTS
    g_w = 1.0 / jnp.sum(jnp.where(is_g, jnp.exp(logit - gmax), 0.0), axis=-1, keepdims=True)
    lo = g_idx * EXPERTS_PER_GROUP
    el = jnp.where((lane >= lo) & (lane < lo + EXPERTS_PER_GROUP), logit, -jnp.inf)
    t1 = jnp.max(el, axis=-1, keepdims=True)
    i1 = jnp.min(jnp.where(el == t1, lane, big), axis=-1, keepdims=True)
    el2 = jnp.where(lane == i1, -jnp.inf, el)
    t2 = jnp.max(el2, axis=-1, keepdims=True)
    i2 = jnp.min(jnp.where(el2 == t2, lane, big), axis=-1, keepdims=True)
    d = jnp.exp(t2 - t1)
    cw1 = g_w * (1.0 / (1.0 + d))
    cw2 = g_w * (d / (1.0 + d))

    oh1 = lane == i1
    oh2 = lane == i2
    picked = jnp.where(oh1, 1, jnp.where(oh2, 2, 0))
    hit = jnp.minimum(picked, 1)
    row_i = lax.broadcasted_iota(jnp.int32, (tm, LANES), 0)
    seen = hit
    shift = 1
    while shift < tm:
        seen = seen + jnp.where(row_i >= shift, pltpu.roll(seen, shift, axis=0), 0)
        shift *= 2
    run = run_ref[0:1, :]
    code_ref[...] = jnp.where(hit > 0, picked + 4 * (seen - hit + run), 0)
    run_new = jnp.broadcast_to(run + seen[tm - 1:tm, :], run_ref.shape)
    run_ref[...] = run_new
    cnt_ref[...] = run_new

    route_ref[...] = jnp.where(lane_i == 0, cw1, jnp.where(lane_i == 1, cw2, 0.0))


def _stage2(attn_in, ob, ga, gb, x2d, gt1, sh2, sc2, wua, wub, wo, g2, wr_pair, br, *, tm, mod_index, stride_index,
            merged, name):
    n = x2d.shape[0]
    row = lambda i: (i, 0)
    const2 = lambda i: (0, 0)
    scratch = [pltpu.VMEM((8, LANES), jnp.int32)]
    if merged:
        attn_specs = [pl.BlockSpec((tm, D_A), row)]
    else:
        attn_specs = [pl.BlockSpec((1, dil, tm // dil, D_A), stride_index) for _, dil in DILATED_GROUPS]
        attn_specs += [pl.BlockSpec((1, dil, tm // dil, D_A), stride_index) for _, dil in DILATED_GROUPS]
        scratch += [pltpu.VMEM((D_A // LANES, tm, LANES), F32), pltpu.VMEM((D_A // LANES, tm, LANES), F32)]
    in_specs = attn_specs + [
        pl.BlockSpec((tm, D_B), row), pl.BlockSpec((tm, D_MODEL), row), pl.BlockSpec((tm, D_MODEL), row),
        pl.BlockSpec((tm, D_MODEL), row),
        _mod_spec(gt1, mod_index), _mod_spec(sh2, mod_index), _mod_spec(sc2, mod_index),
        pl.BlockSpec((D_A, D_MODEL), const2), pl.BlockSpec((D_B, D_MODEL), const2),
        pl.BlockSpec((D_MODEL, D_MODEL), const2), pl.BlockSpec((1, D_MODEL), const2),
        pl.BlockSpec((D_MODEL, 2 * LANES), const2),
        pl.BlockSpec((1, LANES), const2),
    ]
    out_shape = [jax.ShapeDtypeStruct((n, D_MODEL), F32), jax.ShapeDtypeStruct((n, D_MODEL), F32),
                 jax.ShapeDtypeStruct((n, LANES), F32), jax.ShapeDtypeStruct((n, LANES), jnp.int32),
                 jax.ShapeDtypeStruct((8, LANES), jnp.int32)]
    out_specs = [pl.BlockSpec((tm, D_MODEL), row), pl.BlockSpec((tm, D_MODEL), row),
                 pl.BlockSpec((tm, LANES), row), pl.BlockSpec((tm, LANES), row), pl.BlockSpec((8, LANES), const2)]
    return pl.pallas_call(
        functools.partial(_stage2_body, merged=merged),
        out_shape=out_shape, grid=(n // tm,), in_specs=in_specs, out_specs=out_specs,
        scratch_shapes=scratch,
        compiler_params=_params(1), name=name,
    )(*attn_in, ob, ga, gb, x2d, gt1[0], sh2[0], sc2[0], wua, wub, wo, g2, wr_pair, br)


def _dispatch_body(cnt_ref, off_ref, pad_ref, nv_ref, posp_ref, poss_ref, hp_ref, hs_ref, xs_ref,
                   zero_ref, buf_ref, load_sem, scat_sem, zero_sem, *, tiles_p, n_tiles):
    step = pl.program_id(0)
    tile = zero_ref.shape[0]
    groups = buf_ref.shape[1]
    tm = groups * ROW_TILE
    groups_s = hs_ref.shape[0]
    n_s = groups_s * ROW_TILE
    slot = lax.rem(step, 3)
    ahead = lax.rem(step + 1, 3)

    def load_prompt(j, s):
        return pltpu.make_async_copy(hp_ref.at[pl.ds(j * groups, groups)], buf_ref.at[s], load_sem.at[s])

    def load_sample(s):
        return pltpu.make_async_copy(hs_ref, buf_ref.at[s, pl.ds(0, groups_s)], load_sem.at[s])

    def row_copy(s, group, j, dst_row):
        return pltpu.make_async_copy(buf_ref.at[s, group, pl.ds(j, 1)], xs_ref.at[pl.ds(dst_row, 1)], scat_sem.at[s])

    def scatter(rows, pos_ref, s):
        def issue(group, c):
            for j in range(ROW_TILE):
                t = group * ROW_TILE + j
                row_copy(s, group, j, pos_ref[0, 0, 2 * t]).start()
                row_copy(s, group, j, pos_ref[0, 0, 2 * t + 1]).start(priority=1)
            return c

        lax.fori_loop(0, rows // ROW_TILE, issue, 0)

    def drain(rows, s):
        def one(t, c):
            row_copy(s, 0, 0, 0).wait()
            row_copy(s, 0, 0, 0).wait()
            return c

        lax.fori_loop(0, rows, one, 0, unroll=8)

    @pl.when(step == 0)
    def _():
        zero_ref[...] = jnp.zeros_like(zero_ref)
        load_prompt(0, 0).start()
        bits = int(math.log2(tile))

        def pad_copies(e, wait):
            n = pad_ref[e] - cnt_ref[e]
            base = off_ref[e] + cnt_ref[e]
            end = off_ref[e] + pad_ref[e]

            def one_row(i, c):
                copy = _row_copy(zero_ref, 0, xs_ref, base + i, zero_sem)
                if wait:
                    copy.wait()
                else:
                    copy.start()
                return c

            lax.fori_loop(0, jnp.bitwise_and(n, 7), one_row, 0)
            for b in range(3, bits):
                size = 1 << b
                first = pl.multiple_of(end - jnp.bitwise_and(n, ~(2 * size - 1)) - size, 8)
                copy = pltpu.make_async_copy(zero_ref.at[pl.ds(0, size)], xs_ref.at[pl.ds(first, size)], zero_sem)

                @pl.when(jnp.bitwise_and(n, size) != 0)
                def _():
                    if wait:
                        copy.wait()
                    else:
                        copy.start()

        def start_pads(e, c):
            pad_copies(e, False)
            return c

        def wait_pads(e, c):
            pad_copies(e, True)
            return c

        def tile_copy(j):
            return pltpu.make_async_copy(zero_ref, xs_ref.at[pl.ds(j * tile, tile)], zero_sem)

        def start_tile(j, c):
            tile_copy(j).start()
            return c

        def wait_tile(j, c):
            tile_copy(j).wait()
            return c

        lax.fori_loop(0, N_EXPERTS, start_pads, 0)
        lax.fori_loop(nv_ref[0], n_tiles, start_tile, 0)
        lax.fori_loop(0, N_EXPERTS, wait_pads, 0)
        lax.fori_loop(nv_ref[0], n_tiles, wait_tile, 0)

    @pl.when(step >= 2)
    def _():
        drain(tm, ahead)

    @pl.when(step + 1 < tiles_p)
    def _():
        load_prompt(step + 1, ahead).start()

    @pl.when(step + 1 == tiles_p)
    def _():
        load_sample(ahead).start()

    @pl.when(step < tiles_p)
    def _():
        load_prompt(step, slot).wait()
        scatter(tm, posp_ref, slot)

    @pl.when(step == tiles_p)
    def _():
        load_sample(slot).wait()
        scatter(n_s, poss_ref, slot)
        drain(tm, lax.rem(step + 2, 3))
        drain(n_s, slot)


def _dispatch(count, offset, padded, n_valid, pos_p, pos_s, h2_p, h2_s, *, cap, tm, tile):
    n_p, n_s = h2_p.shape[0], h2_s.shape[0]
    tiles_p = n_p // tm
    assert tiles_p >= 2 and n_s <= tm
    last_p = lambda i, *_: (jnp.minimum(i, tiles_p - 1), 0, 0)
    in_specs = [pl.BlockSpec((1, 1, 2 * tm), last_p, memory_space=pltpu.SMEM),
                pl.BlockSpec((1, 1, 2 * n_s), lambda i, *_: (0, 0, 0), memory_space=pltpu.SMEM),
                pl.BlockSpec(memory_space=pl.ANY),
                pl.BlockSpec(memory_space=pl.ANY)]
    grid_spec = pltpu.PrefetchScalarGridSpec(
        num_scalar_prefetch=4, grid=(tiles_p + 1,), in_specs=in_specs,
        out_specs=pl.BlockSpec(memory_space=pl.ANY),
        scratch_shapes=[pltpu.VMEM((tile, D_MODEL), F32), pltpu.VMEM((3, tm // ROW_TILE, ROW_TILE, D_MODEL), F32),
                        pltpu.SemaphoreType.DMA((3,)), pltpu.SemaphoreType.DMA((3,)), pltpu.SemaphoreType.DMA(())],
    )
    by_tile = lambda a: a.reshape(a.shape[0] // ROW_TILE, ROW_TILE, D_MODEL)
    return pl.pallas_call(
        functools.partial(_dispatch_body, tiles_p=tiles_p, n_tiles=cap // tile),
        out_shape=jax.ShapeDtypeStruct((cap, D_MODEL), F32), grid_spec=grid_spec,
        compiler_params=pltpu.CompilerParams(dimension_semantics=("arbitrary",), vmem_limit_bytes=VMEM_LIMIT,
                                             has_side_effects=True),
        name="dispatch",
    )(count, offset, padded, n_valid, pos_p.reshape(tiles_p, 1, 2 * tm), pos_s.reshape(1, 1, 2 * n_s),
      by_tile(h2_p), by_tile(h2_s))


def _moe_body(te_ref, nv_ref, run_ref, nxt_ref, x_ref, wg_hbm, wu_hbm, wd_hbm, y_ref,
              wg_f, wu_f, wd_f, wg_s, wu_s, wd_s, sem):
    i = pl.program_id(0)
    active = i < nv_ref[0]
    fresh = (i == 0) | (te_ref[i] != te_ref[jnp.maximum(i - 1, 0)])
    slot = lax.rem(run_ref[i], 2)

    def fetch(e, s):
        return (pltpu.make_async_copy(wg_hbm.at[e], wg_f.at[s], sem.at[s]),
                pltpu.make_async_copy(wu_hbm.at[e], wu_f.at[s], sem.at[s]),
                pltpu.make_async_copy(wd_hbm.at[e], wd_f.at[s], sem.at[s]))

    @pl.when(active & (i == 0))
    def _():
        for copy in fetch(te_ref[0], 0):
            copy.start()

    @pl.when(active & fresh)
    def _():
        for copy in fetch(te_ref[i], slot):
            copy.wait()

        @pl.when(nxt_ref[i] >= 0)
        def _():
            for copy in fetch(nxt_ref[i], 1 - slot):
                copy.start()

        wg_s[...] = wg_f[slot].astype(BF16)
        wu_s[...] = wu_f[slot].astype(BF16)
        wd_s[...] = wd_f[slot].astype(BF16)

    @pl.when(active)
    def _():
        x = x_ref[...].astype(BF16)
        a = _dot(x, wg_s[...])
        b = _dot(x, wu_s[...])
        mid = (a * _sigmoid(a) * b).astype(BF16)
        y_ref[...] = _dot(mid, wd_s[...])

    @pl.when(jnp.logical_not(active))
    def _():
        y_ref[...] = jnp.zeros_like(y_ref)


def _moe(tile_expert, n_valid, x_sorted, w_eg, w_eu, w_ed, tm):
    cap = x_sorted.shape[0]
    n_tiles = cap // tm
    tile = jnp.arange(n_tiles, dtype=jnp.int32)
    live = tile < n_valid[0]
    starts = jnp.concatenate([jnp.zeros((1,), jnp.bool_), tile_expert[1:] != tile_expert[:-1]]) & live
    run = jnp.cumsum(starts.astype(jnp.int32))
    later = (tile_expert[None, :] > tile_expert[:, None]) & live[None, :]
    nxt = jnp.min(jnp.where(later, tile_expert[None, :], N_EXPERTS), axis=1)
    nxt = jnp.where(nxt < N_EXPERTS, nxt, -1).astype(jnp.int32)

    xmap = lambda i, te, nv, rn, nx: (jnp.maximum(jnp.minimum(i, nv[0] - 1), 0), 0)
    grid_spec = pltpu.PrefetchScalarGridSpec(
        num_scalar_prefetch=4,
        grid=(n_tiles,),
        in_specs=[pl.BlockSpec((tm, D_MODEL), xmap),
                  pl.BlockSpec(memory_space=pl.ANY), pl.BlockSpec(memory_space=pl.ANY),
                  pl.BlockSpec(memory_space=pl.ANY)],
        out_specs=pl.BlockSpec((tm, D_MODEL), lambda i, te, nv, rn, nx: (i, 0)),
        scratch_shapes=[pltpu.VMEM((2, D_MODEL, D_EXPERT), F32), pltpu.VMEM((2, D_MODEL, D_EXPERT), F32),
                        pltpu.VMEM((2, D_EXPERT, D_MODEL), F32),
                        pltpu.VMEM((D_MODEL, D_EXPERT), BF16), pltpu.VMEM((D_MODEL, D_EXPERT), BF16),
                        pltpu.VMEM((D_EXPERT, D_MODEL), BF16), pltpu.SemaphoreType.DMA((2,))],
    )
    return pl.pallas_call(
        _moe_body, out_shape=jax.ShapeDtypeStruct((cap, D_MODEL), F32), grid_spec=grid_spec,
        compiler_params=_params(1), name="moe",
    )(tile_expert, n_valid, run, nxt, x_sorted, w_eg, w_eu, w_ed)


def _final_body(pos_ref, next_ref, x1_ref, route_ref, gt2_ref, gf_ref, ys_ref, o_ref, y_buf, sem):
    step = pl.program_id(0)
    tm = x1_ref.shape[0]
    slot = lax.rem(step, 2)

    def row_copy(s, k, group, j, src_row):
        return pltpu.make_async_copy(ys_ref.at[pl.ds(src_row, 1)], y_buf.at[s, k, group, pl.ds(j, 1)], sem.at[s])

    def gather(p_ref, s):
        def issue(group, c):
            for j in range(ROW_TILE):
                t = group * ROW_TILE + j
                row_copy(s, 0, group, j, p_ref[0, 0, 2 * t]).start()
                row_copy(s, 1, group, j, p_ref[0, 0, 2 * t + 1]).start(priority=1)
            return c

        lax.fori_loop(0, tm // ROW_TILE, issue, 0)

    @pl.when(step == 0)
    def _():
        gather(pos_ref, 0)

    @pl.when(step + 1 < pl.num_programs(0))
    def _():
        gather(next_ref, 1 - slot)

    def drain(t, c):
        row_copy(slot, 0, 0, 0, 0).wait()
        row_copy(slot, 1, 0, 0, 0).wait()
        return c

    lax.fori_loop(0, tm, drain, 0, unroll=8)

    route = route_ref[...]
    moe = (route[:, 0:1] * y_buf[slot, 0].reshape(tm, D_MODEL)
           + route[:, 1:2] * y_buf[slot, 1].reshape(tm, D_MODEL))
    x = x1_ref[...] + gt2_ref[0] * moe
    o_ref[...] = x * lax.rsqrt(jnp.mean(x * x, axis=-1, keepdims=True) + EPS) * gf_ref[...]


def _final(pos, x1, route, gt2, gf, y_sorted, *, tm, mod_index, name):
    n = x1.shape[0]
    steps = n // tm
    row = lambda i: (i, 0)
    pos3 = pos.reshape(steps, 1, 2 * tm)
    return pl.pallas_call(
        _final_body,
        out_shape=jax.ShapeDtypeStruct((n, D_MODEL), F32),
        grid=(steps,),
        in_specs=[pl.BlockSpec((1, 1, 2 * tm), lambda i: (i, 0, 0), memory_space=pltpu.SMEM),
                  pl.BlockSpec((1, 1, 2 * tm), lambda i: (jnp.minimum(i + 1, steps - 1), 0, 0),
                               memory_space=pltpu.SMEM),
                  pl.BlockSpec((tm, D_MODEL), row), pl.BlockSpec((tm, LANES), row),
                  _mod_spec(gt2, mod_index), pl.BlockSpec((1, D_MODEL), lambda i: (0, 0)),
                  pl.BlockSpec(memory_space=pl.ANY)],
        out_specs=pl.BlockSpec((tm, D_MODEL), row),
        scratch_shapes=[pltpu.VMEM((2, 2, tm // ROW_TILE, ROW_TILE, D_MODEL), F32), pltpu.SemaphoreType.DMA((2,))],
        compiler_params=_params(1), name=name,
    )(pos3, pos3, x1, route, gt2[0], gf, y_sorted)


def _t5_bucket(dist):
    max_exact = N_BUCKETS // 2
    dist = np.asarray(dist)
    d = np.maximum(dist, 1).astype(np.float32)
    large = max_exact + (np.log(d / np.float32(max_exact)) / np.float32(math.log(MAX_DISTANCE / max_exact))
                         * np.float32(N_BUCKETS - max_exact)).astype(np.int32)
    return np.where(dist < max_exact, dist, np.minimum(large, N_BUCKETS - 1)).astype(np.int32)


def _bias_lookup(tab, dist):
    onehot = (jnp.asarray(_t5_bucket(dist))[..., None] == jnp.arange(N_BUCKETS)).astype(F32)
    return jnp.einsum("...b,bh->...h", onehot, tab, precision=HIGHEST)


def _prompt_bias(tab, dil):
    delta = np.arange(BLOCK)[:, None] + BLOCK - np.arange(2 * BLOCK)[None, :]
    return jnp.transpose(_bias_lookup(tab, np.maximum(delta, 0) * dil), (2, 0, 1))


def _sample_bias(tab, win, dil, t_new):
    cols = win + LANES
    back = win + np.arange(t_new)[:, None] - np.arange(cols)[None, :]
    bias = _bias_lookup(tab, np.maximum(back, 0))
    return jnp.transpose(bias, (2, 0, 1)).reshape(HEADS * t_new, cols)


def kernel(x_prompt, x_sample, c_prompt, c_sample, cache_kv_w128, cache_kv_w512, cache_kv_w2048, rel_bias, w_ada,
           b_ada, g_norm1, w_in, ln_v_g, ln_v_b, w_spatial, b_spatial, w_up_a, w_up_b, w_out, g_norm2,
           w_route_group, b_route_group, w_route_expert, b_route_expert, w_e_gate, w_e_up, w_e_down, g_final):
    assert w_ada.shape[0] == 1, "single layer"
    bp, s_len, _ = x_prompt.shape
    bs, t_new, _ = x_sample.shape
    n_p, n_s = bp * s_len, bs * t_new
    caches = (cache_kv_w128[0], cache_kv_w512[0], cache_kv_w2048[0])
    assert all(c.shape[1] == win for c, (win, _) in zip(caches, DILATED_GROUPS)), "cache holds one full window"

    mod = _ada(jnp.concatenate([c_prompt, c_sample], axis=0), w_ada[0], b_ada[0])
    mod_p = [(mod[:bp].reshape(bp, 1, 6 * D_MODEL), j) for j in range(6)]
    mod_s = [(jnp.repeat(mod[bp:], t_new, axis=0).reshape(1, n_s, 6 * D_MODEL), j) for j in range(6)]

    w_in_bf = w_in[0].astype(BF16)
    g1 = g_norm1[0].reshape(1, D_MODEL)
    lng, lnb = ln_v_g[0].reshape(1, D_B), ln_v_b[0].reshape(1, D_B)
    ws_p = w_spatial[0]
    bs_p = jnp.repeat(jnp.transpose(b_spatial[0]), SGU_DIM, axis=1)
    reps = n_s // t_new
    corner = w_spatial[0][:, :t_new, :t_new]
    ws_s = (jnp.eye(reps, dtype=F32)[None, :, None, :, None] * corner[:, None, :, None, :]).reshape(
        SGU_GROUPS, n_s, n_s)
    bs_s = jnp.broadcast_to(bs_p[None, :t_new], (reps, t_new, D_B)).reshape(n_s, D_B)

    tpb = s_len // TM_STAGE1
    tail_p = [(bp, min(win, s_len)) for win, _ in DILATED_GROUPS]

    def tail_index_p(i, g, rows):
        first = tpb - tail_p[g][1] // rows
        return (i // tpb, jnp.maximum(i % tpb - first, 0), 0)

    q_s, kv0_s, kv1_s, kv2_s, ob_s, ga_s, gb_s, vn_s = _stage1(
        x_sample.reshape(n_s, D_MODEL), mod_s[0], mod_s[1], g1, w_in_bf, lng, lnb, ws_s, bs_s,
        tm=n_s, mod_index=lambda i: (0, 0, 0), stride_index=None, tail_shapes=[(1, n_s)] * N_DIL,
        tail_index=lambda i, g, rows: (0, 0, 0),
        period=t_new, by_stride=False, emit_vn=True, name="stage1_sample")

    tabs = [rel_bias[:, g * HEADS:(g + 1) * HEADS].astype(F32) for g in range(N_DIL)]
    sample_bias = [_sample_bias(tabs[g], win, dil, t_new) for g, (win, dil) in enumerate(DILATED_GROUPS)]
    caches_t = [jnp.transpose(c, (0, 2, 3, 4, 1)).reshape(bs, 2, D_A, c.shape[1]) for c in caches]
    outs = _stage1(
        x_prompt.reshape(n_p, D_MODEL), mod_p[0], mod_p[1], g1, w_in_bf, lng, lnb, ws_p, bs_p,
        tm=TM_STAGE1, mod_index=lambda i: (i // tpb, 0, 0),
        stride_index=(bp, s_len, lambda i: (i // tpb, 0, i % tpb, 0)), tail_shapes=tail_p, tail_index=tail_index_p,
        period=CHUNK, by_stride=True, emit_vn=False, name="stage1_prompt",
        cache=(q_s, (kv0_s, kv1_s, kv2_s), caches_t, sample_bias, t_new))
    q_g, k_g, v_g = outs[0:3], outs[3:6], outs[6:9]
    kv0_p, kv1_p, kv2_p, ob_p, ga_p, gb_p, cache0_t, cache1_t, cache2_t, oa_s = outs[9:]

    o_groups, lse_groups = [], []
    for g, (win, dil) in enumerate(DILATED_GROUPS):
        o, lse = _attn_prompt_group(q_g[g], k_g[g], v_g[g], _prompt_bias(tabs[g], dil) * LOG2E, dil,
                                    min(ATTN_BLOCKS_PER_STEP, s_len // dil // BLOCK))
        o_groups.append(o)
        lse_groups.append(lse)

    wua, wub, wo = w_up_a[0].astype(BF16), w_up_b[0].astype(BF16), w_out[0].astype(BF16)
    g2 = g_norm2[0].reshape(1, D_MODEL)
    wr = jnp.concatenate([w_route_expert[0].reshape(D_MODEL, N_EXPERTS), w_route_group[0]], axis=1)
    wr = jnp.pad(wr, ((0, 0), (0, LANES - wr.shape[1])))
    br = jnp.concatenate([b_route_expert[0].reshape(N_EXPERTS), b_route_group[0]])
    br = jnp.pad(br, (0, LANES - br.shape[0])).reshape(1, LANES)
    wr_hi = wr.astype(BF16)
    wr_pair = jnp.concatenate([wr_hi, (wr - wr_hi.astype(F32)).astype(BF16)], axis=1)

    tpb2 = s_len // TM_STAGE2
    x1_p, h2_p, route_p, code_p, cnt_p = _stage2(
        o_groups + lse_groups, ob_p, ga_p, gb_p, x_prompt.reshape(n_p, D_MODEL),
        mod_p[2], mod_p[3], mod_p[4], wua, wub, wo, g2, wr_pair, br,
        tm=TM_STAGE2, mod_index=lambda i: (i // tpb2, 0, 0), stride_index=lambda i: (i // tpb2, 0, i % tpb2, 0),
        merged=False, name="stage2_prompt")
    x1_s, h2_s, route_s, code_s, cnt_s = _stage2(
        [oa_s], ob_s, ga_s, gb_s, x_sample.reshape(n_s, D_MODEL),
        mod_s[2], mod_s[3], mod_s[4], wua, wub, wo, g2, wr_pair, br,
        tm=n_s, mod_index=lambda i: (0, 0, 0), stride_index=None, merged=True, name="stage2_sample")

    cap = 2 * (n_p + n_s) + N_EXPERTS * TM_MOE
    n_tiles = cap // TM_MOE
    count_p = cnt_p[0, :N_EXPERTS]
    count = count_p + cnt_s[0, :N_EXPERTS]
    padded = ((count + TM_MOE - 1) // TM_MOE) * TM_MOE
    ends = jnp.cumsum(padded)
    offset = ends - padded
    n_valid = (ends[-1] // TM_MOE).astype(jnp.int32).reshape(1)
    tile_start = jnp.arange(n_tiles, dtype=jnp.int32) * TM_MOE
    tile_expert = jnp.minimum(jnp.sum(ends[None, :] <= tile_start[:, None], axis=1), N_EXPERTS - 1).astype(jnp.int32)

    def sorted_rows(code, earlier):
        row = jnp.pad(offset + earlier, (0, LANES - N_EXPERTS))[None, :] + jnp.right_shift(code, 2)
        return jnp.stack([jnp.sum(jnp.where((code & 3) == k, row, 0), axis=1) for k in (1, 2)], axis=1)

    pos_p = sorted_rows(code_p, jnp.zeros_like(count_p))
    pos_s = sorted_rows(code_s, count_p)
    x_sorted = _dispatch(count, offset, padded, n_valid, pos_p, pos_s, h2_p, h2_s, cap=cap, tm=TM_FINAL, tile=TM_MOE)

    y_sorted = _moe(tile_expert, n_valid, x_sorted, w_e_gate[0], w_e_up[0], w_e_down[0], TM_MOE)

    gf = g_final.reshape(1, D_MODEL)
    tpb3 = s_len // TM_FINAL
    y_p = _final(pos_p, x1_p, route_p, mod_p[5], gf, y_sorted,
                 tm=TM_FINAL, mod_index=lambda i: (i // tpb3, 0, 0), name="final_prompt")
    y_s = _final(pos_s, x1_s, route_s, mod_s[5], gf, y_sorted,
                 tm=n_s, mod_index=lambda i: (0, 0, 0), name="final_sample")

    kv_p = [a.reshape(1, bp, a.shape[1], 2, HEADS, HEAD_DIM) for a in (kv0_p, kv1_p, kv2_p)]
    kv_s = [jnp.transpose(c.reshape(bs, 2, HEADS, HEAD_DIM, c.shape[-1]), (0, 4, 1, 2, 3))[None]
            for c in (cache0_t, cache1_t, cache2_t)]
    return (y_p.reshape(bp, s_len, D_MODEL), y_s.reshape(bs, t_new, D_MODEL),
            kv_p[0], kv_p[1], kv_p[2], kv_s[0], kv_s[1], kv_s[2],
            vn_s.reshape(1, bs, t_new, D_B))
```

```python
import functools
import math

import numpy as np
import jax
import jax.numpy as jnp
from jax import lax
from jax.experimental import pallas as pl
from jax.experimental.pallas import tpu as pltpu

F32 = jnp.float32
BF16 = jnp.bfloat16
HIGHEST = lax.Precision.HIGHEST

D_MODEL = 1024
HEAD_DIM = 64
HEADS = 8
DILATED_GROUPS = ((128, 1), (512, 4), (2048, 16))
N_DIL = 3
D_A = HEADS * HEAD_DIM
D_QKV = N_DIL * D_A
BLOCK = 128
STEPS = 128
SCALE = HEAD_DIM ** -0.5
CHUNK = 128
D_B = 512
SGU_GROUPS = 4
SGU_DIM = D_B // SGU_GROUPS
N_BUCKETS = 32
MAX_DISTANCE = 2048
MOE_GROUPS = 4
EXPERTS_PER_GROUP = 8
N_EXPERTS = 32
D_EXPERT = 512
EPS = 1e-6
NEG_INF = -1e30
LOG2E = math.log2(math.e)
LN2 = math.log(2.0)
LANES = 128
ROW_TILE = 8
C_U = 3 * D_QKV
C_V = C_U + D_B
C_GA = C_V + D_B
C_GB = C_GA + D_MODEL
IN_COLS = C_GB + D_MODEL

TM_STAGE1 = 256
TM_STAGE2 = 512
TM_MOE = 512
ATTN_BLOCKS_PER_STEP = 8
TM_FINAL = 512
VMEM_LIMIT = 56 * 1024 * 1024


def _dot(a, b):
    return jnp.dot(a, b, preferred_element_type=F32)


def _dot_nt(a, b):
    return lax.dot_general(a, b, (((1,), (1,)), ((), ())), preferred_element_type=F32)


def _sigmoid(x):
    return 1.0 / (1.0 + jnp.exp(-x))


def _gelu(x):
    return 0.5 * x * (1.0 + lax.erf(x * (2.0 ** -0.5)))


def _params(n_grid):
    return pltpu.CompilerParams(dimension_semantics=("arbitrary",) * n_grid, vmem_limit_bytes=VMEM_LIMIT)


def _mod_spec(vectors_col, batch_index):
    vectors, col = vectors_col
    return pl.BlockSpec((1, vectors.shape[1], D_MODEL), lambda i: batch_index(i)[:2] + (col,))


def _row_copy(src, src_row, dst, dst_row, sem):
    return pltpu.make_async_copy(src.at[pl.ds(src_row, 1)], dst.at[pl.ds(dst_row, 1)], sem)


def _ada_body(c_ref, w_ref, b_ref, o_ref):
    c = c_ref[...]
    s = c * _sigmoid(c)
    o_ref[...] = jnp.dot(s, w_ref[...], precision=HIGHEST, preferred_element_type=F32) + b_ref[...]


def _ada(c_all, w_ada, b_ada):
    n, d = c_all.shape
    cols = w_ada.shape[1]
    tn = 1024
    return pl.pallas_call(
        _ada_body,
        out_shape=jax.ShapeDtypeStruct((n, cols), F32),
        grid=(cols // tn,),
        in_specs=[pl.BlockSpec((n, d), lambda j: (0, 0)),
                  pl.BlockSpec((d, tn), lambda j: (0, j)),
                  pl.BlockSpec((1, tn), lambda j: (0, j))],
        out_specs=pl.BlockSpec((n, tn), lambda j: (0, j)),
        compiler_params=_params(1),
        name="ada",
    )(c_all, w_ada, b_ada.reshape(1, cols))


def _stage1_body(x_ref, sh_ref, sc_ref, g1_ref, w_ref, lng_ref, lnb_ref, ws_ref, bs_ref, *rest,
                 period, by_stride, emit_vn, cache_t_new):
    rest = list(rest)
    cache_refs = [rest.pop(0) for _ in range(1 + 3 * N_DIL)] if cache_t_new else []
    if by_stride:
        qkv_refs = [[rest.pop(0) for _ in range(N_DIL)] for _ in range(3)]
    else:
        q_ref = rest.pop(0)
    kv_refs = [rest.pop(0) for _ in range(N_DIL)]
    ob_ref, ga_ref, gb_ref = rest.pop(0), rest.pop(0), rest.pop(0)
    vn_ref = rest.pop(0) if emit_vn else None
    if cache_t_new:
        cache_refs += [rest.pop(0) for _ in range(N_DIL + 1)]
    stage_ref = rest.pop(0) if by_stride else None
    if cache_t_new:
        _sample_cache_step(pl.program_id(0), cache_refs + rest, cache_t_new)

    x = x_ref[...]
    tm = x.shape[0]
    h = x * lax.rsqrt(jnp.mean(x * x, axis=-1, keepdims=True) + EPS) * g1_ref[...]
    h = h * (1.0 + sc_ref[0]) + sh_ref[0]
    hb = h.astype(BF16)

    q = _dot(hb, w_ref[:, 0:D_QKV]) * (SCALE * LOG2E if by_stride else SCALE)
    k = _dot(hb, w_ref[:, D_QKV:2 * D_QKV])
    v = _dot(hb, w_ref[:, 2 * D_QKV:3 * D_QKV])
    if by_stride:
        per_group = D_A // LANES
        for val, refs in zip((q, k, v), qkv_refs):
            for c in range(D_QKV // LANES):
                stage_ref[c] = val[:, c * LANES:(c + 1) * LANES]
            for g, (_, dil) in enumerate(DILATED_GROUPS):
                for r in range(dil):
                    for c in range(per_group):
                        piece = stage_ref[g * per_group + c, pl.ds(r, tm // dil, stride=dil), :]
                        refs[g][0, r, :, c * LANES:(c + 1) * LANES] = piece.astype(BF16)
    else:
        q_ref[...] = q
    for g in range(N_DIL):
        rows = kv_refs[g].shape[1]
        kv_refs[g][0, :, 0:D_A] = k[tm - rows:, g * D_A:(g + 1) * D_A]
        kv_refs[g][0, :, D_A:2 * D_A] = v[tm - rows:, g * D_A:(g + 1) * D_A]

    ga_ref[...] = _sigmoid(_dot(hb, w_ref[:, C_GA:C_GB])).astype(BF16)
    gb_ref[...] = _sigmoid(_dot(hb, w_ref[:, C_GB:IN_COLS])).astype(BF16)

    u = _gelu(_dot(hb, w_ref[:, C_U:C_V]))
    vb = _gelu(_dot(hb, w_ref[:, C_V:C_GA]))
    xc = vb - jnp.mean(vb, axis=-1, keepdims=True)
    vn = xc * lax.rsqrt(jnp.mean(xc * xc, axis=-1, keepdims=True) + EPS) * lng_ref[...] + lnb_ref[...]
    if emit_vn:
        vn_ref[...] = vn
    vnb = vn.astype(BF16)

    r = ws_ref.shape[1]
    ri = lax.broadcasted_iota(jnp.int32, (r, r), 0)
    ci = lax.broadcasted_iota(jnp.int32, (r, r), 1)
    keep = ci <= ri
    if period != r:
        sh = int(math.log2(period))
        keep = keep & (jnp.right_shift(ri, sh) == jnp.right_shift(ci, sh))
    for g in range(SGU_GROUPS):
        wg = jnp.where(keep, ws_ref[g], 0.0).astype(BF16)
        cs = slice(g * SGU_DIM, (g + 1) * SGU_DIM)
        for c in range(tm // r):
            rs = slice(c * r, (c + 1) * r)
            mixed = _dot(wg, vnb[rs, cs]) + bs_ref[:, cs]
            ob_ref[rs, cs] = (u[rs, cs] * mixed).astype(BF16)


def _stage1(x2d, sh, sc, g1, w_in_bf, lng, lnb, ws, bs, *, tm, mod_index, stride_index, tail_shapes, tail_index,
            period, by_stride, emit_vn, name, cache=None):
    n = x2d.shape[0]
    grid = (n // tm,)
    r = ws.shape[1]
    const2 = lambda i: (0, 0)
    row = lambda i: (i, 0)
    in_specs = [
        pl.BlockSpec((tm, D_MODEL), row),
        _mod_spec(sh, mod_index),
        _mod_spec(sc, mod_index),
        pl.BlockSpec((1, D_MODEL), const2),
        pl.BlockSpec((D_MODEL, IN_COLS), const2, pipeline_mode=pl.Buffered(1)),
        pl.BlockSpec((1, D_B), const2),
        pl.BlockSpec((1, D_B), const2),
        pl.BlockSpec((SGU_GROUPS, r, r), lambda i: (0, 0, 0)),
        pl.BlockSpec((r, D_B), const2),
    ]
    out_shape, out_specs, scratch = [], [], []
    if by_stride:
        n_batch, s_len = stride_index[0], stride_index[1]
        for _ in range(3):
            for _, dil in DILATED_GROUPS:
                out_shape.append(jax.ShapeDtypeStruct((n_batch, dil, s_len // dil, D_A), BF16))
                out_specs.append(pl.BlockSpec((1, dil, tm // dil, D_A), stride_index[2]))
        scratch.append(pltpu.VMEM((D_QKV // LANES, tm, LANES), F32))
    else:
        out_shape.append(jax.ShapeDtypeStruct((n, D_QKV), F32))
        out_specs.append(pl.BlockSpec((tm, D_QKV), row))
    for g in range(N_DIL):
        nb, keep = tail_shapes[g]
        rows = min(tm, keep)
        out_shape.append(jax.ShapeDtypeStruct((nb, keep, 2 * D_A), F32))
        out_specs.append(pl.BlockSpec((1, rows, 2 * D_A), functools.partial(tail_index, g=g, rows=rows)))
    out_shape += [jax.ShapeDtypeStruct((n, D_B), BF16),
                  jax.ShapeDtypeStruct((n, D_MODEL), BF16),
                  jax.ShapeDtypeStruct((n, D_MODEL), BF16)]
    out_specs += [pl.BlockSpec((tm, D_B), row), pl.BlockSpec((tm, D_MODEL), row), pl.BlockSpec((tm, D_MODEL), row)]
    if emit_vn:
        out_shape.append(jax.ShapeDtypeStruct((n, D_B), F32))
        out_specs.append(pl.BlockSpec((tm, D_B), row))
    args = [x2d, sh[0], sc[0], g1, w_in_bf, lng, lnb, ws, bs]
    cache_t_new = 0
    if cache is not None:
        c_args, c_in, c_shape, c_out, c_scratch, units = _sample_cache_specs(*cache)
        assert units == grid[0], "one sample-cache unit per projection tile"
        args += c_args
        in_specs += c_in
        out_shape += c_shape
        out_specs += c_out
        scratch += c_scratch
        cache_t_new = cache[-1]
    body = functools.partial(_stage1_body, period=period, by_stride=by_stride, emit_vn=emit_vn,
                             cache_t_new=cache_t_new)
    return pl.pallas_call(
        body, out_shape=out_shape, grid=grid, in_specs=in_specs, out_specs=out_specs, scratch_shapes=scratch,
        compiler_params=_params(1), name=name,
    )(*args)


def _attn_prompt_body(q_ref, kp_ref, kc_ref, vp_ref, vc_ref, bias_ref, o_ref, lse_ref):
    n = pl.program_id(2)
    n_sub = q_ref.shape[2] // BLOCK
    ri = lax.broadcasted_iota(jnp.int32, (BLOCK, 2 * BLOCK), 0)
    ci = lax.broadcasted_iota(jnp.int32, (BLOCK, 2 * BLOCK), 1)
    delta = ri + BLOCK - ci
    band = (delta >= 0) & (delta <= STEPS)
    first_key = jnp.where(n > 0, 0, BLOCK)
    low_q = lax.broadcasted_iota(jnp.int32, (BLOCK, LANES), 1) < HEAD_DIM
    low_v = lax.broadcasted_iota(jnp.int32, (2 * BLOCK, LANES), 1) < HEAD_DIM
    ones_lo = jnp.where(low_v, 1.0, 0.0).astype(BF16)
    ones_hi = jnp.where(low_v, 0.0, 1.0).astype(BF16)
    for sub in range(n_sub):
        rows = slice(sub * BLOCK, (sub + 1) * BLOCK)
        valid = band & (ci >= first_key) if sub == 0 else band
        for pair in range(HEADS // 2):
            cs = slice(pair * LANES, (pair + 1) * LANES)
            qp = q_ref[0, 0, rows, cs]
            if sub == 0:
                kp = jnp.concatenate([kp_ref[0, 0, :, cs], kc_ref[0, 0, 0:BLOCK, cs]], axis=0)
                vp = jnp.concatenate([vp_ref[0, 0, :, cs], vc_ref[0, 0, 0:BLOCK, cs]], axis=0)
            else:
                kp = kc_ref[0, 0, (sub - 1) * BLOCK:(sub + 1) * BLOCK, cs]
                vp = vc_ref[0, 0, (sub - 1) * BLOCK:(sub + 1) * BLOCK, cs]
            acc, tops = None, []
            for half in range(2):
                own_q = low_q if half == 0 else jnp.logical_not(low_q)
                own_v = low_v if half == 0 else jnp.logical_not(low_v)
                s = _dot_nt(jnp.where(own_q, qp, jnp.zeros_like(qp)), kp) + bias_ref[2 * pair + half]
                s = jnp.where(valid, s, NEG_INF)
                m = jnp.max(s, axis=-1, keepdims=True)
                p = jnp.exp2(s - m).astype(BF16)
                w = jnp.concatenate([jnp.where(own_v, vp, jnp.zeros_like(vp)), ones_lo if half == 0 else ones_hi],
                                    axis=1)
                r = _dot(p, w)
                acc = r if acc is None else acc + r
                tops.append(m)
            den = acc[:, LANES:]
            o_ref[0, 0, rows, cs] = (acc[:, :LANES] / den).astype(BF16)
            lse_ref[0, 0, rows, cs] = jnp.where(low_q, tops[0], tops[1]) * LN2 + jnp.log(den)


def _attn_prompt_group(q4, k4, v4, bias, dil, n_sub):
    b_sz, _, m_len, _ = q4.shape
    span = n_sub * BLOCK
    cur = pl.BlockSpec((1, 1, span, D_A), lambda b, r, n: (b, r, n, 0))
    prev = pl.BlockSpec((1, 1, BLOCK, D_A), lambda b, r, n: (b, r, jnp.maximum(n * n_sub - 1, 0), 0))
    return pl.pallas_call(
        _attn_prompt_body,
        out_shape=[jax.ShapeDtypeStruct((b_sz, dil, m_len, D_A), BF16),
                   jax.ShapeDtypeStruct((b_sz, dil, m_len, D_A), F32)],
        grid=(b_sz, dil, m_len // span),
        in_specs=[cur, prev, cur, prev, cur,
                  pl.BlockSpec((HEADS, BLOCK, 2 * BLOCK), lambda b, r, n: (0, 0, 0))],
        out_specs=[cur, cur],
        compiler_params=_params(3),
        name=f"attn_prompt_d{dil}",
    )(q4, k4, k4, v4, v4, bias)


def _sample_cache_step(unit, refs, t_new):
    refs = list(refs)
    q_ref = refs.pop(0)
    new_refs = [refs.pop(0) for _ in range(N_DIL)]
    cache_refs = [refs.pop(0) for _ in range(N_DIL)]
    bias_refs = [refs.pop(0) for _ in range(N_DIL)]
    out_refs = [refs.pop(0) for _ in range(N_DIL)]
    o_ref = refs.pop(0)
    p_refs = [refs.pop(0) for _ in range(N_DIL)]
    scale_ref = refs.pop(0)
    kv = lax.rem(unit, 2)
    rows = HEADS * t_new

    def allowed(g, col0, cols):
        win, dil = DILATED_GROUPS[g]
        tok = jnp.bitwise_and(lax.broadcasted_iota(jnp.int32, (rows, cols), 0), t_new - 1)
        col = lax.broadcasted_iota(jnp.int32, (rows, cols), 1) + col0
        back = win + tok - col
        return ((col < win + t_new) & (back >= 0) & (jnp.bitwise_and(back, dil - 1) == 0)
                & (back <= STEPS * dil))

    lane = lax.broadcasted_iota(jnp.int32, (D_A, LANES), 1)
    front = jnp.zeros((LANES - t_new, D_A), F32)
    for g in range(N_DIL):
        win = cache_refs[g].shape[-1]
        new = new_refs[g][0]
        half = jnp.where(kv == 0, new[:, 0:D_A], new[:, D_A:2 * D_A])
        tail = jnp.transpose(jnp.concatenate([front, half], axis=0))
        rolled = pltpu.roll(cache_refs[g][0, 0], win - t_new, axis=1)
        if win > LANES:
            out_refs[g][0, 0, :, 0:win - LANES] = rolled[:, 0:win - LANES]
        out_refs[g][0, 0, :, win - LANES:win] = jnp.where(lane >= LANES - t_new, tail, rolled[:, win - LANES:win])

    ri = lax.broadcasted_iota(jnp.int32, (rows, D_A), 0)
    ci = lax.broadcasted_iota(jnp.int32, (rows, D_A), 1)
    own_head = jnp.right_shift(ri, int(math.log2(t_new))) == jnp.right_shift(ci, int(math.log2(HEAD_DIM)))
    pad = jnp.zeros((LANES - t_new, D_A), F32)
    glane = lax.broadcasted_iota(jnp.int32, (rows, LANES), 1)

    @pl.when(kv == 0)
    def _():
        q = q_ref[...]
        stats = []
        for g in range(N_DIL):
            win = cache_refs[g].shape[-1]
            qg = q[:, g * D_A:(g + 1) * D_A]
            qblk = jnp.where(own_head, jnp.concatenate([qg] * HEADS, axis=0), 0.0).astype(BF16)
            s_c = _dot(qblk, cache_refs[g][0, 0].astype(BF16)) + bias_refs[g][:, 0:win]
            s_c = jnp.where(allowed(g, 0, win), s_c, NEG_INF)
            k_new = jnp.concatenate([new_refs[g][0][:, 0:D_A], pad], axis=0).astype(BF16)
            s_n = _dot_nt(qblk, k_new) + bias_refs[g][:, win:]
            s_n = jnp.where(allowed(g, win, LANES), s_n, NEG_INF)
            m = jnp.maximum(jnp.max(s_c, axis=-1, keepdims=True), jnp.max(s_n, axis=-1, keepdims=True))
            p_c = jnp.exp(s_c - m)
            p_n = jnp.exp(s_n - m)
            l = jnp.sum(p_c, axis=-1, keepdims=True) + jnp.sum(p_n, axis=-1, keepdims=True)
            p_refs[g][:, 0:win] = p_c.astype(BF16)
            p_refs[g][:, win:] = p_n.astype(BF16)
            stats.append((l, m + jnp.log(l)))
        top = jnp.maximum(jnp.maximum(stats[0][1], stats[1][1]), stats[2][1])
        es = [jnp.exp(lse - top) for _, lse in stats]
        den = es[0] + es[1] + es[2]
        scale = jnp.zeros((rows, LANES), F32)
        for g in range(N_DIL):
            scale = jnp.where(glane == g, es[g] / den / stats[g][0], scale)
        scale_ref[...] = scale

    @pl.when(kv == 1)
    def _():
        acc = jnp.zeros((rows, D_A), F32)
        for g in range(N_DIL):
            win = cache_refs[g].shape[-1]
            part = _dot_nt(p_refs[g][:, 0:win], cache_refs[g][0, 0].astype(BF16))
            v_new = jnp.concatenate([new_refs[g][0][:, D_A:2 * D_A], pad], axis=0).astype(BF16)
            part = part + _dot(p_refs[g][:, win:], v_new)
            acc = acc + scale_ref[:, g:g + 1] * part
        acc = jnp.where(own_head, acc, 0.0)
        out = acc[0:t_new]
        for h in range(1, HEADS):
            out = out + acc[h * t_new:(h + 1) * t_new]
        o_ref[...] = out


def _sample_cache_specs(q_s, kv_new, caches_t, biases, t_new):
    n = q_s.shape[0]
    nb = n // t_new
    blk4 = lambda i: (i // 2, lax.rem(i, 2), 0, 0)
    in_specs = [pl.BlockSpec((t_new, D_QKV), lambda i: (i // 2, 0))]
    in_specs += [pl.BlockSpec((1, t_new, 2 * D_A), lambda i: (i // 2, 0, 0))] * N_DIL
    in_specs += [pl.BlockSpec((1, 1, D_A, c.shape[-1]), blk4) for c in caches_t]
    in_specs += [pl.BlockSpec(a.shape, lambda i: (0, 0), pipeline_mode=pl.Buffered(1)) for a in biases]
    out_shape = [jax.ShapeDtypeStruct(c.shape, c.dtype) for c in caches_t]
    out_specs = [pl.BlockSpec((1, 1, D_A, c.shape[-1]), blk4) for c in caches_t]
    out_shape.append(jax.ShapeDtypeStruct((n, D_A), F32))
    out_specs.append(pl.BlockSpec((t_new, D_A), lambda i: (i // 2, 0)))
    rows = HEADS * t_new
    scratch = [pltpu.VMEM((rows, c.shape[-1] + LANES), BF16) for c in caches_t]
    scratch.append(pltpu.VMEM((rows, LANES), F32))
    args = [q_s, *[a.reshape(nb, t_new, 2 * D_A) for a in kv_new], *caches_t, *biases]
    return args, in_specs, out_shape, out_specs, scratch, 2 * nb


def _stage2_body(*refs, merged):
    refs = list(refs)
    if merged:
        oa_ref = refs.pop(0)
    else:
        o_refs = [refs.pop(0) for _ in range(N_DIL)]
        l_refs = [refs.pop(0) for _ in range(N_DIL)]
    (ob_ref, ga_ref, gb_ref, x_ref, gt1_ref, sh2_ref, sc2_ref, wua_ref, wub_ref, wo_ref, g2_ref,
     wr_ref, br_ref, x1_ref, h2_ref, route_ref, code_ref, cnt_ref, run_ref) = refs[:19]
    step = pl.program_id(0)
    tm = x_ref.shape[0]

    @pl.when(step == 0)
    def _():
        run_ref[...] = jnp.zeros_like(run_ref)

    if merged:
        oa = oa_ref[...].astype(BF16)
    else:
        o_stage, l_stage = refs[19], refs[20]

        def by_position(ref, stage, dil):
            if dil == 1:
                return ref[0, 0].astype(F32)
            slabs = ref.shape[-1] // LANES
            for r in range(dil):
                rows = ref[0, r].astype(F32)
                for c in range(slabs):
                    stage[c, pl.ds(r, tm // dil, stride=dil), :] = rows[:, c * LANES:(c + 1) * LANES]
            return jnp.concatenate([stage[c] for c in range(slabs)], axis=1)

        ls = [by_position(l_refs[g], l_stage, dil) for g, (_, dil) in enumerate(DILATED_GROUPS)]
        top = jnp.maximum(jnp.maximum(ls[0], ls[1]), ls[2])
        es = [jnp.exp(l - top) for l in ls]
        den = es[0] + es[1] + es[2]
        oa = None
        for g, (_, dil) in enumerate(DILATED_GROUPS):
            term = es[g] * by_position(o_refs[g], o_stage, dil)
            oa = term if oa is None else oa + term
        oa = (oa / den).astype(BF16)

    ya = _dot(oa, wua_ref[...])
    yb = _dot(ob_ref[...], wub_ref[...])
    z = (ga_ref[...].astype(F32) * ya + gb_ref[...].astype(F32) * yb).astype(BF16)
    x1 = x_ref[...] + gt1_ref[0] * _dot(z, wo_ref[...])
    x1_ref[...] = x1
    h2 = x1 * lax.rsqrt(jnp.mean(x1 * x1, axis=-1, keepdims=True) + EPS) * g2_ref[...]
    h2 = h2 * (1.0 + sc2_ref[0]) + sh2_ref[0]
    h2_ref[...] = h2

    h_hi = h2.astype(BF16)
    h_lo = (h2 - h_hi.astype(F32)).astype(BF16)
    both = _dot(h_hi, wr_ref[...])
    logit = (both[:, 0:LANES] + both[:, LANES:] + _dot(h_lo, wr_ref[:, 0:LANES])) + br_ref[...]
    lane_i = lax.broadcasted_iota(jnp.int32, (tm, LANES), 1)
    lane = lane_i.astype(F32)
    big = float(LANES)
    is_g = (lane_i >= N_EXPERTS) & (lane_i < N_EXPERTS + MOE_GROUPS)
    gl = jnp.where(is_g, logit, -jnp.inf)
    gmax = jnp.max(gl, axis=-1, keepdims=True)
    g_idx = jnp.min(jnp.where(gl == gmax, lane, big), axis=-1, keepdims=True) - N_EXPERTS
    g_w = 1.0 / jnp.sum(jnp.where(is_g, jnp.exp(logit - gmax), 0.0), axis=-1, keepdims=True)
    lo = g_idx * EXPERTS_PER_GROUP
    el = jnp.where((lane >= lo) & (lane < lo + EXPERTS_PER_GROUP), logit, -jnp.inf)
    t1 = jnp.max(el, axis=-1, keepdims=True)
    i1 = jnp.min(jnp.where(el == t1, lane, big), axis=-1, keepdims=True)
    el2 = jnp.where(lane == i1, -jnp.inf, el)
    t2 = jnp.max(el2, axis=-1, keepdims=True)
    i2 = jnp.min(jnp.where(el2 == t2, lane, big), axis=-1, keepdims=True)
    d = jnp.exp(t2 - t1)
    cw1 = g_w * (1.0 / (1.0 + d))
    cw2 = g_w * (d / (1.0 + d))

    oh1 = lane == i1
    oh2 = lane == i2
    picked = jnp.where(oh1, 1, jnp.where(oh2, 2, 0))
    hit = jnp.minimum(picked, 1)
    row_i = lax.broadcasted_iota(jnp.int32, (tm, LANES), 0)
    seen = hit
    shift = 1
    while shift < tm:
        seen = seen + jnp.where(row_i >= shift, pltpu.roll(seen, shift, axis=0), 0)
        shift *= 2
    run = run_ref[0:1, :]
    code_ref[...] = jnp.where(hit > 0, picked + 4 * (seen - hit + run), 0)
    run_new = jnp.broadcast_to(run + seen[tm - 1:tm, :], run_ref.shape)
    run_ref[...] = run_new
    cnt_ref[...] = run_new

    route_ref[...] = jnp.where(lane_i == 0, cw1, jnp.where(lane_i == 1, cw2, 0.0))


def _stage2(attn_in, ob, ga, gb, x2d, gt1, sh2, sc2, wua, wub, wo, g2, wr_pair, br, *, tm, mod_index, stride_index,
            merged, name):
    n = x2d.shape[0]
    row = lambda i: (i, 0)
    const2 = lambda i: (0, 0)
    scratch = [pltpu.VMEM((8, LANES), jnp.int32)]
    if merged:
        attn_specs = [pl.BlockSpec((tm, D_A), row)]
    else:
        attn_specs = [pl.BlockSpec((1, dil, tm // dil, D_A), stride_index) for _, dil in DILATED_GROUPS]
        attn_specs += [pl.BlockSpec((1, dil, tm // dil, D_A), stride_index) for _, dil in DILATED_GROUPS]
        scratch += [pltpu.VMEM((D_A // LANES, tm, LANES), F32), pltpu.VMEM((D_A // LANES, tm, LANES), F32)]
    in_specs = attn_specs + [
        pl.BlockSpec((tm, D_B), row), pl.BlockSpec((tm, D_MODEL), row), pl.BlockSpec((tm, D_MODEL), row),
        pl.BlockSpec((tm, D_MODEL), row),
        _mod_spec(gt1, mod_index), _mod_spec(sh2, mod_index), _mod_spec(sc2, mod_index),
        pl.BlockSpec((D_A, D_MODEL), const2), pl.BlockSpec((D_B, D_MODEL), const2),
        pl.BlockSpec((D_MODEL, D_MODEL), const2), pl.BlockSpec((1, D_MODEL), const2),
        pl.BlockSpec((D_MODEL, 2 * LANES), const2),
        pl.BlockSpec((1, LANES), const2),
    ]
    out_shape = [jax.ShapeDtypeStruct((n, D_MODEL), F32), jax.ShapeDtypeStruct((n, D_MODEL), F32),
                 jax.ShapeDtypeStruct((n, LANES), F32), jax.ShapeDtypeStruct((n, LANES), jnp.int32),
                 jax.ShapeDtypeStruct((8, LANES), jnp.int32)]
    out_specs = [pl.BlockSpec((tm, D_MODEL), row), pl.BlockSpec((tm, D_MODEL), row),
                 pl.BlockSpec((tm, LANES), row), pl.BlockSpec((tm, LANES), row), pl.BlockSpec((8, LANES), const2)]
    return pl.pallas_call(
        functools.partial(_stage2_body, merged=merged),
        out_shape=out_shape, grid=(n // tm,), in_specs=in_specs, out_specs=out_specs,
        scratch_shapes=scratch,
        compiler_params=_params(1), name=name,
    )(*attn_in, ob, ga, gb, x2d, gt1[0], sh2[0], sc2[0], wua, wub, wo, g2, wr_pair, br)


def _dispatch_body(cnt_ref, off_ref, pad_ref, nv_ref, posp_ref, poss_ref, hp_ref, hs_ref, xs_ref,
                   zero_ref, buf_ref, load_sem, scat_sem, zero_sem, *, tiles_p, n_tiles):
    step = pl.program_id(0)
    tile = zero_ref.shape[0]
    groups = buf_ref.shape[1]
    tm = groups * ROW_TILE
    groups_s = hs_ref.shape[0]
    n_s = groups_s * ROW_TILE
    slot = lax.rem(step, 3)
    ahead = lax.rem(step + 1, 3)

    def load_prompt(j, s):
        return pltpu.make_async_copy(hp_ref.at[pl.ds(j * groups, groups)], buf_ref.at[s], load_sem.at[s])

    def load_sample(s):
        return pltpu.make_async_copy(hs_ref, buf_ref.at[s, pl.ds(0, groups_s)], load_sem.at[s])

    def row_copy(s, group, j, dst_row):
        return pltpu.make_async_copy(buf_ref.at[s, group, pl.ds(j, 1)], xs_ref.at[pl.ds(dst_row, 1)], scat_sem.at[s])

    def scatter(rows, pos_ref, s):
        def issue(group, c):
            for j in range(ROW_TILE):
                t = group * ROW_TILE + j
                row_copy(s, group, j, pos_ref[0, 0, 2 * t]).start()
                row_copy(s, group, j, pos_ref[0, 0, 2 * t + 1]).start(priority=1)
            return c

        lax.fori_loop(0, rows // ROW_TILE, issue, 0)

    def drain(rows, s):
        def one(t, c):
            row_copy(s, 0, 0, 0).wait()
            row_copy(s, 0, 0, 0).wait()
            return c

        lax.fori_loop(0, rows, one, 0, unroll=8)

    @pl.when(step == 0)
    def _():
        zero_ref[...] = jnp.zeros_like(zero_ref)
        load_prompt(0, 0).start()
        bits = int(math.log2(tile))

        def pad_copies(e, wait):
            n = pad_ref[e] - cnt_ref[e]
            base = off_ref[e] + cnt_ref[e]
            end = off_ref[e] + pad_ref[e]

            def one_row(i, c):
                copy = _row_copy(zero_ref, 0, xs_ref, base + i, zero_sem)
                if wait:
                    copy.wait()
                else:
                    copy.start()
                return c

            lax.fori_loop(0, jnp.bitwise_and(n, 7), one_row, 0)
            for b in range(3, bits):
                size = 1 << b
                first = pl.multiple_of(end - jnp.bitwise_and(n, ~(2 * size - 1)) - size, 8)
                copy = pltpu.make_async_copy(zero_ref.at[pl.ds(0, size)], xs_ref.at[pl.ds(first, size)], zero_sem)

                @pl.when(jnp.bitwise_and(n, size) != 0)
                def _():
                    if wait:
                        copy.wait()
                    else:
                        copy.start()

        def start_pads(e, c):
            pad_copies(e, False)
            return c

        def wait_pads(e, c):
            pad_copies(e, True)
            return c

        def tile_copy(j):
            return pltpu.make_async_copy(zero_ref, xs_ref.at[pl.ds(j * tile, tile)], zero_sem)

        def start_tile(j, c):
            tile_copy(j).start()
            return c

        def wait_tile(j, c):
            tile_copy(j).wait()
            return c

        lax.fori_loop(0, N_EXPERTS, start_pads, 0)
        lax.fori_loop(nv_ref[0], n_tiles, start_tile, 0)
        lax.fori_loop(0, N_EXPERTS, wait_pads, 0)
        lax.fori_loop(nv_ref[0], n_tiles, wait_tile, 0)

    @pl.when(step >= 2)
    def _():
        drain(tm, ahead)

    @pl.when(step + 1 < tiles_p)
    def _():
        load_prompt(step + 1, ahead).start()

    @pl.when(step + 1 == tiles_p)
    def _():
        load_sample(ahead).start()

    @pl.when(step < tiles_p)
    def _():
        load_prompt(step, slot).wait()
        scatter(tm, posp_ref, slot)

    @pl.when(step == tiles_p)
    def _():
        load_sample(slot).wait()
        scatter(n_s, poss_ref, slot)
        drain(tm, lax.rem(step + 2, 3))
        drain(n_s, slot)


def _dispatch(count, offset, padded, n_valid, pos_p, pos_s, h2_p, h2_s, *, cap, tm, tile):
    n_p, n_s = h2_p.shape[0], h2_s.shape[0]
    tiles_p = n_p // tm
    assert tiles_p >= 2 and n_s <= tm
    last_p = lambda i, *_: (jnp.minimum(i, tiles_p - 1), 0, 0)
    in_specs = [pl.BlockSpec((1, 1, 2 * tm), last_p, memory_space=pltpu.SMEM),
                pl.BlockSpec((1, 1, 2 * n_s), lambda i, *_: (0, 0, 0), memory_space=pltpu.SMEM),
                pl.BlockSpec(memory_space=pl.ANY),
                pl.BlockSpec(memory_space=pl.ANY)]
    grid_spec = pltpu.PrefetchScalarGridSpec(
        num_scalar_prefetch=4, grid=(tiles_p + 1,), in_specs=in_specs,
        out_specs=pl.BlockSpec(memory_space=pl.ANY),
        scratch_shapes=[pltpu.VMEM((tile, D_MODEL), F32), pltpu.VMEM((3, tm // ROW_TILE, ROW_TILE, D_MODEL), F32),
                        pltpu.SemaphoreType.DMA((3,)), pltpu.SemaphoreType.DMA((3,)), pltpu.SemaphoreType.DMA(())],
    )
    by_tile = lambda a: a.reshape(a.shape[0] // ROW_TILE, ROW_TILE, D_MODEL)
    return pl.pallas_call(
        functools.partial(_dispatch_body, tiles_p=tiles_p, n_tiles=cap // tile),
        out_shape=jax.ShapeDtypeStruct((cap, D_MODEL), F32), grid_spec=grid_spec,
        compiler_params=pltpu.CompilerParams(dimension_semantics=("arbitrary",), vmem_limit_bytes=VMEM_LIMIT,
                                             has_side_effects=True),
        name="dispatch",
    )(count, offset, padded, n_valid, pos_p.reshape(tiles_p, 1, 2 * tm), pos_s.reshape(1, 1, 2 * n_s),
      by_tile(h2_p), by_tile(h2_s))


def _moe_body(te_ref, nv_ref, run_ref, nxt_ref, x_ref, wg_hbm, wu_hbm, wd_hbm, y_ref,
              wg_f, wu_f, wd_f, wg_s, wu_s, wd_s, sem):
    i = pl.program_id(0)
    active = i < nv_ref[0]
    fresh = (i == 0) | (te_ref[i] != te_ref[jnp.maximum(i - 1, 0)])
    slot = lax.rem(run_ref[i], 2)

    def fetch(e, s):
        return (pltpu.make_async_copy(wg_hbm.at[e], wg_f.at[s], sem.at[s]),
                pltpu.make_async_copy(wu_hbm.at[e], wu_f.at[s], sem.at[s]),
                pltpu.make_async_copy(wd_hbm.at[e], wd_f.at[s], sem.at[s]))

    @pl.when(active & (i == 0))
    def _():
        for copy in fetch(te_ref[0], 0):
            copy.start()

    @pl.when(active & fresh)
    def _():
        for copy in fetch(te_ref[i], slot):
            copy.wait()

        @pl.when(nxt_ref[i] >= 0)
        def _():
            for copy in fetch(nxt_ref[i], 1 - slot):
                copy.start()

        wg_s[...] = wg_f[slot].astype(BF16)
        wu_s[...] = wu_f[slot].astype(BF16)
        wd_s[...] = wd_f[slot].astype(BF16)

    @pl.when(active)
    def _():
        x = x_ref[...].astype(BF16)
        a = _dot(x, wg_s[...])
        b = _dot(x, wu_s[...])
        mid = (a * _sigmoid(a) * b).astype(BF16)
        y_ref[...] = _dot(mid, wd_s[...])

    @pl.when(jnp.logical_not(active))
    def _():
        y_ref[...] = jnp.zeros_like(y_ref)


def _moe(tile_expert, n_valid, x_sorted, w_eg, w_eu, w_ed, tm):
    cap = x_sorted.shape[0]
    n_tiles = cap // tm
    tile = jnp.arange(n_tiles, dtype=jnp.int32)
    live = tile < n_valid[0]
    starts = jnp.concatenate([jnp.zeros((1,), jnp.bool_), tile_expert[1:] != tile_expert[:-1]]) & live
    run = jnp.cumsum(starts.astype(jnp.int32))
    later = (tile_expert[None, :] > tile_expert[:, None]) & live[None, :]
    nxt = jnp.min(jnp.where(later, tile_expert[None, :], N_EXPERTS), axis=1)
    nxt = jnp.where(nxt < N_EXPERTS, nxt, -1).astype(jnp.int32)

    xmap = lambda i, te, nv, rn, nx: (jnp.maximum(jnp.minimum(i, nv[0] - 1), 0), 0)
    grid_spec = pltpu.PrefetchScalarGridSpec(
        num_scalar_prefetch=4,
        grid=(n_tiles,),
        in_specs=[pl.BlockSpec((tm, D_MODEL), xmap),
                  pl.BlockSpec(memory_space=pl.ANY), pl.BlockSpec(memory_space=pl.ANY),
                  pl.BlockSpec(memory_space=pl.ANY)],
        out_specs=pl.BlockSpec((tm, D_MODEL), lambda i, te, nv, rn, nx: (i, 0)),
        scratch_shapes=[pltpu.VMEM((2, D_MODEL, D_EXPERT), F32), pltpu.VMEM((2, D_MODEL, D_EXPERT), F32),
                        pltpu.VMEM((2, D_EXPERT, D_MODEL), F32),
                        pltpu.VMEM((D_MODEL, D_EXPERT), BF16), pltpu.VMEM((D_MODEL, D_EXPERT), BF16),
                        pltpu.VMEM((D_EXPERT, D_MODEL), BF16), pltpu.SemaphoreType.DMA((2,))],
    )
    return pl.pallas_call(
        _moe_body, out_shape=jax.ShapeDtypeStruct((cap, D_MODEL), F32), grid_spec=grid_spec,
        compiler_params=_params(1), name="moe",
    )(tile_expert, n_valid, run, nxt, x_sorted, w_eg, w_eu, w_ed)


def _final_body(pos_ref, next_ref, x1_ref, route_ref, gt2_ref, gf_ref, ys_ref, o_ref, y_buf, sem):
    step = pl.program_id(0)
    last = pl.num_programs(0) - 1
    tm = x1_ref.shape[0]
    groups = tm // ROW_TILE
    slot = lax.rem(step, 2)

    def row_copy(s, k, group, j, src_row):
        return pltpu.make_async_copy(ys_ref.at[pl.ds(src_row, 1)], y_buf.at[s, k, group, pl.ds(j, 1)], sem.at[s])

    batch = 8
    n_rows = batch * ROW_TILE

    def issue(p_ref, s, chunk):
        for g in range(batch):
            group = chunk * batch + g
            for j in range(ROW_TILE):
                t = group * ROW_TILE + j
                row_copy(s, 0, group, j, p_ref[0, 0, 2 * t]).start()
                row_copy(s, 1, group, j, p_ref[0, 0, 2 * t + 1]).start(priority=1)

    def combine(chunk):
        rows = pl.ds(pl.multiple_of(chunk * n_rows, n_rows), n_rows)
        tiles = pl.ds(chunk * batch, batch)
        route = route_ref[rows, :]
        moe = (route[:, 0:1] * y_buf[slot, 0, tiles].reshape(n_rows, D_MODEL)
               + route[:, 1:2] * y_buf[slot, 1, tiles].reshape(n_rows, D_MODEL))
        gate = gt2_ref[0] if gt2_ref.shape[1] == 1 else gt2_ref[0, rows, :]
        x = x1_ref[rows, :] + gate * moe
        o_ref[rows, :] = x * lax.rsqrt(jnp.mean(x * x, axis=-1, keepdims=True) + EPS) * gf_ref[...]

    groups = groups // batch

    @pl.when(step == 0)
    def _():
        def first(group, c):
            issue(pos_ref, 0, group)
            return c

        lax.fori_loop(0, groups, first, 0)

    def drain(t, c):
        row_copy(slot, 0, 0, 0, 0).wait()
        row_copy(slot, 1, 0, 0, 0).wait()
        return c

    lax.fori_loop(0, tm, drain, 0, unroll=8)

    @pl.when(step < last)
    def _():
        def both(group, c):
            issue(next_ref, 1 - slot, group)
            combine(group)
            return c

        lax.fori_loop(0, groups, both, 0)

    @pl.when(step == last)
    def _():
        def only(group, c):
            combine(group)
            return c

        lax.fori_loop(0, groups, only, 0)


def _final(pos, x1, route, gt2, gf, y_sorted, *, tm, mod_index, name):
    n = x1.shape[0]
    steps = n // tm
    row = lambda i: (i, 0)
    pos3 = pos.reshape(steps, 1, 2 * tm)
    return pl.pallas_call(
        _final_body,
        out_shape=jax.ShapeDtypeStruct((n, D_MODEL), F32),
        grid=(steps,),
        in_specs=[pl.BlockSpec((1, 1, 2 * tm), lambda i: (i, 0, 0), memory_space=pltpu.SMEM),
                  pl.BlockSpec((1, 1, 2 * tm), lambda i: (jnp.minimum(i + 1, steps - 1), 0, 0),
                               memory_space=pltpu.SMEM),
                  pl.BlockSpec((tm, D_MODEL), row), pl.BlockSpec((tm, LANES), row),
                  _mod_spec(gt2, mod_index), pl.BlockSpec((1, D_MODEL), lambda i: (0, 0)),
                  pl.BlockSpec(memory_space=pl.ANY)],
        out_specs=pl.BlockSpec((tm, D_MODEL), row),
        scratch_shapes=[pltpu.VMEM((2, 2, tm // ROW_TILE, ROW_TILE, D_MODEL), F32), pltpu.SemaphoreType.DMA((2,))],
        compiler_params=_params(1), name=name,
    )(pos3, pos3, x1, route, gt2[0], gf, y_sorted)


def _t5_bucket(dist):
    max_exact = N_BUCKETS // 2
    dist = np.asarray(dist)
    d = np.maximum(dist, 1).astype(np.float32)
    large = max_exact + (np.log(d / np.float32(max_exact)) / np.float32(math.log(MAX_DISTANCE / max_exact))
                         * np.float32(N_BUCKETS - max_exact)).astype(np.int32)
    return np.where(dist < max_exact, dist, np.minimum(large, N_BUCKETS - 1)).astype(np.int32)


def _bias_lookup(tab, dist):
    onehot = (jnp.asarray(_t5_bucket(dist))[..., None] == jnp.arange(N_BUCKETS)).astype(F32)
    return jnp.einsum("...b,bh->...h", onehot, tab, precision=HIGHEST)


def _prompt_bias(tab, dil):
    delta = np.arange(BLOCK)[:, None] + BLOCK - np.arange(2 * BLOCK)[None, :]
    return jnp.transpose(_bias_lookup(tab, np.maximum(delta, 0) * dil), (2, 0, 1))


def _sample_bias(tab, win, dil, t_new):
    cols = win + LANES
    back = win + np.arange(t_new)[:, None] - np.arange(cols)[None, :]
    bias = _bias_lookup(tab, np.maximum(back, 0))
    return jnp.transpose(bias, (2, 0, 1)).reshape(HEADS * t_new, cols)


def kernel(x_prompt, x_sample, c_prompt, c_sample, cache_kv_w128, cache_kv_w512, cache_kv_w2048, rel_bias, w_ada,
           b_ada, g_norm1, w_in, ln_v_g, ln_v_b, w_spatial, b_spatial, w_up_a, w_up_b, w_out, g_norm2,
           w_route_group, b_route_group, w_route_expert, b_route_expert, w_e_gate, w_e_up, w_e_down, g_final):
    assert w_ada.shape[0] == 1, "single layer"
    bp, s_len, _ = x_prompt.shape
    bs, t_new, _ = x_sample.shape
    n_p, n_s = bp * s_len, bs * t_new
    caches = (cache_kv_w128[0], cache_kv_w512[0], cache_kv_w2048[0])
    assert all(c.shape[1] == win for c, (win, _) in zip(caches, DILATED_GROUPS)), "cache holds one full window"

    mod = _ada(jnp.concatenate([c_prompt, c_sample], axis=0), w_ada[0], b_ada[0])
    mod_p = [(mod[:bp].reshape(bp, 1, 6 * D_MODEL), j) for j in range(6)]
    mod_s = [(jnp.repeat(mod[bp:], t_new, axis=0).reshape(1, n_s, 6 * D_MODEL), j) for j in range(6)]

    w_in_bf = w_in[0].astype(BF16)
    g1 = g_norm1[0].reshape(1, D_MODEL)
    lng, lnb = ln_v_g[0].reshape(1, D_B), ln_v_b[0].reshape(1, D_B)
    ws_p = w_spatial[0]
    bs_p = jnp.repeat(jnp.transpose(b_spatial[0]), SGU_DIM, axis=1)
    reps = n_s // t_new
    corner = w_spatial[0][:, :t_new, :t_new]
    ws_s = (jnp.eye(reps, dtype=F32)[None, :, None, :, None] * corner[:, None, :, None, :]).reshape(
        SGU_GROUPS, n_s, n_s)
    bs_s = jnp.broadcast_to(bs_p[None, :t_new], (reps, t_new, D_B)).reshape(n_s, D_B)

    tpb = s_len // TM_STAGE1
    tail_p = [(bp, min(win, s_len)) for win, _ in DILATED_GROUPS]

    def tail_index_p(i, g, rows):
        first = tpb - tail_p[g][1] // rows
        return (i // tpb, jnp.maximum(i % tpb - first, 0), 0)

    q_s, kv0_s, kv1_s, kv2_s, ob_s, ga_s, gb_s, vn_s = _stage1(
        x_sample.reshape(n_s, D_MODEL), mod_s[0], mod_s[1], g1, w_in_bf, lng, lnb, ws_s, bs_s,
        tm=n_s, mod_index=lambda i: (0, 0, 0), stride_index=None, tail_shapes=[(1, n_s)] * N_DIL,
        tail_index=lambda i, g, rows: (0, 0, 0),
        period=t_new, by_stride=False, emit_vn=True, name="stage1_sample")

    tabs = [rel_bias[:, g * HEADS:(g + 1) * HEADS].astype(F32) for g in range(N_DIL)]
    sample_bias = [_sample_bias(tabs[g], win, dil, t_new) for g, (win, dil) in enumerate(DILATED_GROUPS)]
    caches_t = [jnp.transpose(c, (0, 2, 3, 4, 1)).reshape(bs, 2, D_A, c.shape[1]) for c in caches]
    outs = _stage1(
        x_prompt.reshape(n_p, D_MODEL), mod_p[0], mod_p[1], g1, w_in_bf, lng, lnb, ws_p, bs_p,
        tm=TM_STAGE1, mod_index=lambda i: (i // tpb, 0, 0),
        stride_index=(bp, s_len, lambda i: (i // tpb, 0, i % tpb, 0)), tail_shapes=tail_p, tail_index=tail_index_p,
        period=CHUNK, by_stride=True, emit_vn=False, name="stage1_prompt",
        cache=(q_s, (kv0_s, kv1_s, kv2_s), caches_t, sample_bias, t_new))
    q_g, k_g, v_g = outs[0:3], outs[3:6], outs[6:9]
    kv0_p, kv1_p, kv2_p, ob_p, ga_p, gb_p, cache0_t, cache1_t, cache2_t, oa_s = outs[9:]

    o_groups, lse_groups = [], []
    for g, (win, dil) in enumerate(DILATED_GROUPS):
        o, lse = _attn_prompt_group(q_g[g], k_g[g], v_g[g], _prompt_bias(tabs[g], dil) * LOG2E, dil,
                                    min(ATTN_BLOCKS_PER_STEP, s_len // dil // BLOCK))
        o_groups.append(o)
        lse_groups.append(lse)

    wua, wub, wo = w_up_a[0].astype(BF16), w_up_b[0].astype(BF16), w_out[0].astype(BF16)
    g2 = g_norm2[0].reshape(1, D_MODEL)
    wr = jnp.concatenate([w_route_expert[0].reshape(D_MODEL, N_EXPERTS), w_route_group[0]], axis=1)
    wr = jnp.pad(wr, ((0, 0), (0, LANES - wr.shape[1])))
    br = jnp.concatenate([b_route_expert[0].reshape(N_EXPERTS), b_route_group[0]])
    br = jnp.pad(br, (0, LANES - br.shape[0])).reshape(1, LANES)
    wr_hi = wr.astype(BF16)
    wr_pair = jnp.concatenate([wr_hi, (wr - wr_hi.astype(F32)).astype(BF16)], axis=1)

    tpb2 = s_len // TM_STAGE2
    x1_p, h2_p, route_p, code_p, cnt_p = _stage2(
        o_groups + lse_groups, ob_p, ga_p, gb_p, x_prompt.reshape(n_p, D_MODEL),
        mod_p[2], mod_p[3], mod_p[4], wua, wub, wo, g2, wr_pair, br,
        tm=TM_STAGE2, mod_index=lambda i: (i // tpb2, 0, 0), stride_index=lambda i: (i // tpb2, 0, i % tpb2, 0),
        merged=False, name="stage2_prompt")
    x1_s, h2_s, route_s, code_s, cnt_s = _stage2(
        [oa_s], ob_s, ga_s, gb_s, x_sample.reshape(n_s, D_MODEL),
        mod_s[2], mod_s[3], mod_s[4], wua, wub, wo, g2, wr_pair, br,
        tm=n_s, mod_index=lambda i: (0, 0, 0), stride_index=None, merged=True, name="stage2_sample")

    cap = 2 * (n_p + n_s) + N_EXPERTS * TM_MOE
    n_tiles = cap // TM_MOE
    count_p = cnt_p[0, :N_EXPERTS]
    count = count_p + cnt_s[0, :N_EXPERTS]
    padded = ((count + TM_MOE - 1) // TM_MOE) * TM_MOE
    ends = jnp.cumsum(padded)
    offset = ends - padded
    n_valid = (ends[-1] // TM_MOE).astype(jnp.int32).reshape(1)
    tile_start = jnp.arange(n_tiles, dtype=jnp.int32) * TM_MOE
    tile_expert = jnp.minimum(jnp.sum(ends[None, :] <= tile_start[:, None], axis=1), N_EXPERTS - 1).astype(jnp.int32)

    def sorted_rows(code, earlier):
        row = jnp.pad(offset + earlier, (0, LANES - N_EXPERTS))[None, :] + jnp.right_shift(code, 2)
        return jnp.stack([jnp.sum(jnp.where((code & 3) == k, row, 0), axis=1) for k in (1, 2)], axis=1)

    pos_p = sorted_rows(code_p, jnp.zeros_like(count_p))
    pos_s = sorted_rows(code_s, count_p)
    x_sorted = _dispatch(count, offset, padded, n_valid, pos_p, pos_s, h2_p, h2_s, cap=cap, tm=TM_FINAL, tile=TM_MOE)

    y_sorted = _moe(tile_expert, n_valid, x_sorted, w_e_gate[0], w_e_up[0], w_e_down[0], TM_MOE)

    gf = g_final.reshape(1, D_MODEL)
    tpb3 = s_len // TM_FINAL
    y_p = _final(pos_p, x1_p, route_p, mod_p[5], gf, y_sorted,
                 tm=TM_FINAL, mod_index=lambda i: (i // tpb3, 0, 0), name="final_prompt")
    y_s = _final(pos_s, x1_s, route_s, mod_s[5], gf, y_sorted,
                 tm=n_s, mod_index=lambda i: (0, 0, 0), name="final_sample")

    kv_p = [a.reshape(1, bp, a.shape[1], 2, HEADS, HEAD_DIM) for a in (kv0_p, kv1_p, kv2_p)]
    kv_s = [jnp.transpose(c.reshape(bs, 2, HEADS, HEAD_DIM, c.shape[-1]), (0, 4, 1, 2, 3))[None]
            for c in (cache0_t, cache1_t, cache2_t)]
    return (y_p.reshape(bp, s_len, D_MODEL), y_s.reshape(bs, t_new, D_MODEL),
            kv_p[0], kv_p[1], kv_p[2], kv_s[0], kv_s[1], kv_s[2],
            vn_s.reshape(1, bs, t_new, D_B))
```

```python
import functools
import math

import numpy as np
import jax
import jax.numpy as jnp
from jax import lax
from jax.experimental import pallas as pl
from jax.experimental.pallas import tpu as pltpu

F32 = jnp.float32
BF16 = jnp.bfloat16
HIGHEST = lax.Precision.HIGHEST

D_MODEL = 1024
HEAD_DIM = 64
HEADS = 8
DILATED_GROUPS = ((128, 1), (512, 4), (2048, 16))
N_DIL = 3
D_A = HEADS * HEAD_DIM
D_QKV = N_DIL * D_A
BLOCK = 128
STEPS = 128
SCALE = HEAD_DIM ** -0.5
CHUNK = 128
D_B = 512
SGU_GROUPS = 4
SGU_DIM = D_B // SGU_GROUPS
N_BUCKETS = 32
MAX_DISTANCE = 2048
MOE_GROUPS = 4
EXPERTS_PER_GROUP = 8
N_EXPERTS = 32
D_EXPERT = 512
EPS = 1e-6
NEG_INF = -1e30
LOG2E = math.log2(math.e)
LN2 = math.log(2.0)
LANES = 128
ROW_TILE = 8
C_U = 3 * D_QKV
C_V = C_U + D_B
C_GA = C_V + D_B
C_GB = C_GA + D_MODEL
IN_COLS = C_GB + D_MODEL

TM_STAGE1 = 256
TM_STAGE2 = 512
TM_MOE = 512
ATTN_BLOCKS_PER_STEP = 8
TM_FINAL = 512
VMEM_BYTES = 64 * 1024 * 1024
VMEM_LIMIT = VMEM_BYTES - 8 * 1024 * 1024


def _dot(a, b):
    return jnp.dot(a, b, preferred_element_type=F32)


def _dot_nt(a, b):
    return lax.dot_general(a, b, (((1,), (1,)), ((), ())), preferred_element_type=F32)


def _sigmoid(x):
    return 1.0 / (1.0 + jnp.exp(-x))


def _gelu(x):
    return 0.5 * x * (1.0 + lax.erf(x * (2.0 ** -0.5)))


def _params(n_grid):
    return pltpu.CompilerParams(dimension_semantics=("arbitrary",) * n_grid, vmem_limit_bytes=VMEM_LIMIT)


def _mod_spec(vectors_col, batch_index):
    vectors, col = vectors_col
    return pl.BlockSpec((1, vectors.shape[1], D_MODEL), lambda i: batch_index(i)[:2] + (col,))


def _row_copy(src, src_row, dst, dst_row, sem):
    return pltpu.make_async_copy(src.at[pl.ds(src_row, 1)], dst.at[pl.ds(dst_row, 1)], sem)


def _ada_body(c_ref, w_ref, b_ref, o_ref):
    c = c_ref[...]
    s = c * _sigmoid(c)
    o_ref[...] = jnp.dot(s, w_ref[...], precision=HIGHEST, preferred_element_type=F32) + b_ref[...]


def _ada(c_all, w_ada, b_ada):
    n, d = c_all.shape
    cols = w_ada.shape[1]
    tn = 1024
    return pl.pallas_call(
        _ada_body,
        out_shape=jax.ShapeDtypeStruct((n, cols), F32),
        grid=(cols // tn,),
        in_specs=[pl.BlockSpec((n, d), lambda j: (0, 0)),
                  pl.BlockSpec((d, tn), lambda j: (0, j)),
                  pl.BlockSpec((1, tn), lambda j: (0, j))],
        out_specs=pl.BlockSpec((n, tn), lambda j: (0, j)),
        compiler_params=_params(1),
        name="ada",
    )(c_all, w_ada, b_ada.reshape(1, cols))


def _stage1_body(x_ref, sh_ref, sc_ref, g1_ref, w_ref, lng_ref, lnb_ref, ws_ref, bs_ref, *rest,
                 period, by_stride, emit_vn, cache_t_new):
    rest = list(rest)
    cache_refs = [rest.pop(0) for _ in range(1 + 3 * N_DIL)] if cache_t_new else []
    if by_stride:
        qkv_refs = [[rest.pop(0) for _ in range(N_DIL)] for _ in range(3)]
    else:
        q_ref = rest.pop(0)
    kv_refs = [rest.pop(0) for _ in range(N_DIL)]
    ob_ref, ga_ref, gb_ref = rest.pop(0), rest.pop(0), rest.pop(0)
    vn_ref = rest.pop(0) if emit_vn else None
    if cache_t_new:
        cache_refs += [rest.pop(0) for _ in range(N_DIL + 1)]
    stage_ref = rest.pop(0) if by_stride else None
    if cache_t_new:
        _sample_cache_step(pl.program_id(0), cache_refs + rest, cache_t_new)

    x = x_ref[...]
    tm = x.shape[0]
    h = x * lax.rsqrt(jnp.mean(x * x, axis=-1, keepdims=True) + EPS) * g1_ref[...]
    h = h * (1.0 + sc_ref[0]) + sh_ref[0]
    hb = h.astype(BF16)

    q = _dot(hb, w_ref[:, 0:D_QKV]) * (SCALE * LOG2E if by_stride else SCALE)
    k = _dot(hb, w_ref[:, D_QKV:2 * D_QKV])
    v = _dot(hb, w_ref[:, 2 * D_QKV:3 * D_QKV])
    if by_stride:
        per_group = D_A // LANES
        for val, refs in zip((q, k, v), qkv_refs):
            for c in range(D_QKV // LANES):
                stage_ref[c] = val[:, c * LANES:(c + 1) * LANES]
            for g, (_, dil) in enumerate(DILATED_GROUPS):
                for r in range(dil):
                    for c in range(per_group):
                        piece = stage_ref[g * per_group + c, pl.ds(r, tm // dil, stride=dil), :]
                        refs[g][0, r, :, c * LANES:(c + 1) * LANES] = piece.astype(BF16)
    else:
        q_ref[...] = q
    for g in range(N_DIL):
        rows = kv_refs[g].shape[1]
        kv_refs[g][0, :, 0:D_A] = k[tm - rows:, g * D_A:(g + 1) * D_A]
        kv_refs[g][0, :, D_A:2 * D_A] = v[tm - rows:, g * D_A:(g + 1) * D_A]

    ga_ref[...] = _sigmoid(_dot(hb, w_ref[:, C_GA:C_GB])).astype(BF16)
    gb_ref[...] = _sigmoid(_dot(hb, w_ref[:, C_GB:IN_COLS])).astype(BF16)

    u = _gelu(_dot(hb, w_ref[:, C_U:C_V]))
    vb = _gelu(_dot(hb, w_ref[:, C_V:C_GA]))
    xc = vb - jnp.mean(vb, axis=-1, keepdims=True)
    vn = xc * lax.rsqrt(jnp.mean(xc * xc, axis=-1, keepdims=True) + EPS) * lng_ref[...] + lnb_ref[...]
    if emit_vn:
        vn_ref[...] = vn
    vnb = vn.astype(BF16)

    r = ws_ref.shape[1]
    ri = lax.broadcasted_iota(jnp.int32, (r, r), 0)
    ci = lax.broadcasted_iota(jnp.int32, (r, r), 1)
    keep = ci <= ri
    if period != r:
        sh = int(math.log2(period))
        keep = keep & (jnp.right_shift(ri, sh) == jnp.right_shift(ci, sh))
    for g in range(SGU_GROUPS):
        wg = jnp.where(keep, ws_ref[g], 0.0).astype(BF16)
        cs = slice(g * SGU_DIM, (g + 1) * SGU_DIM)
        for c in range(tm // r):
            rs = slice(c * r, (c + 1) * r)
            mixed = _dot(wg, vnb[rs, cs]) + bs_ref[:, cs]
            ob_ref[rs, cs] = (u[rs, cs] * mixed).astype(BF16)


def _stage1(x2d, sh, sc, g1, w_in_bf, lng, lnb, ws, bs, *, tm, mod_index, stride_index, tail_shapes, tail_index,
            period, by_stride, emit_vn, name, cache=None):
    n = x2d.shape[0]
    grid = (n // tm,)
    r = ws.shape[1]
    const2 = lambda i: (0, 0)
    row = lambda i: (i, 0)
    in_specs = [
        pl.BlockSpec((tm, D_MODEL), row),
        _mod_spec(sh, mod_index),
        _mod_spec(sc, mod_index),
        pl.BlockSpec((1, D_MODEL), const2),
        pl.BlockSpec((D_MODEL, IN_COLS), const2, pipeline_mode=pl.Buffered(1)),
        pl.BlockSpec((1, D_B), const2),
        pl.BlockSpec((1, D_B), const2),
        pl.BlockSpec((SGU_GROUPS, r, r), lambda i: (0, 0, 0)),
        pl.BlockSpec((r, D_B), const2),
    ]
    out_shape, out_specs, scratch = [], [], []
    if by_stride:
        n_batch, s_len = stride_index[0], stride_index[1]
        for _ in range(3):
            for _, dil in DILATED_GROUPS:
                out_shape.append(jax.ShapeDtypeStruct((n_batch, dil, s_len // dil, D_A), BF16))
                out_specs.append(pl.BlockSpec((1, dil, tm // dil, D_A), stride_index[2]))
        scratch.append(pltpu.VMEM((D_QKV // LANES, tm, LANES), F32))
    else:
        out_shape.append(jax.ShapeDtypeStruct((n, D_QKV), F32))
        out_specs.append(pl.BlockSpec((tm, D_QKV), row))
    for g in range(N_DIL):
        nb, keep = tail_shapes[g]
        rows = min(tm, keep)
        out_shape.append(jax.ShapeDtypeStruct((nb, keep, 2 * D_A), F32))
        out_specs.append(pl.BlockSpec((1, rows, 2 * D_A), functools.partial(tail_index, g=g, rows=rows)))
    out_shape += [jax.ShapeDtypeStruct((n, D_B), BF16),
                  jax.ShapeDtypeStruct((n, D_MODEL), BF16),
                  jax.ShapeDtypeStruct((n, D_MODEL), BF16)]
    out_specs += [pl.BlockSpec((tm, D_B), row), pl.BlockSpec((tm, D_MODEL), row), pl.BlockSpec((tm, D_MODEL), row)]
    if emit_vn:
        out_shape.append(jax.ShapeDtypeStruct((n, D_B), F32))
        out_specs.append(pl.BlockSpec((tm, D_B), row))
    args = [x2d, sh[0], sc[0], g1, w_in_bf, lng, lnb, ws, bs]
    cache_t_new = 0
    if cache is not None:
        c_args, c_in, c_shape, c_out, c_scratch, units = _sample_cache_specs(*cache)
        assert units == grid[0], "one sample-cache unit per projection tile"
        args += c_args
        in_specs += c_in
        out_shape += c_shape
        out_specs += c_out
        scratch += c_scratch
        cache_t_new = cache[-1]
    body = functools.partial(_stage1_body, period=period, by_stride=by_stride, emit_vn=emit_vn,
                             cache_t_new=cache_t_new)
    return pl.pallas_call(
        body, out_shape=out_shape, grid=grid, in_specs=in_specs, out_specs=out_specs, scratch_shapes=scratch,
        compiler_params=_params(1), name=name,
    )(*args)


def _attn_prompt_body(q_ref, kp_ref, kc_ref, vp_ref, vc_ref, bias_ref, o_ref, lse_ref):
    n = pl.program_id(2)
    n_sub = q_ref.shape[2] // BLOCK
    ri = lax.broadcasted_iota(jnp.int32, (BLOCK, 2 * BLOCK), 0)
    ci = lax.broadcasted_iota(jnp.int32, (BLOCK, 2 * BLOCK), 1)
    delta = ri + BLOCK - ci
    band = (delta >= 0) & (delta <= STEPS)
    first_key = jnp.where(n > 0, 0, BLOCK)
    low_q = lax.broadcasted_iota(jnp.int32, (BLOCK, LANES), 1) < HEAD_DIM
    low_v = lax.broadcasted_iota(jnp.int32, (2 * BLOCK, LANES), 1) < HEAD_DIM
    ones_lo = jnp.where(low_v, 1.0, 0.0).astype(BF16)
    ones_hi = jnp.where(low_v, 0.0, 1.0).astype(BF16)
    for sub in range(n_sub):
        rows = slice(sub * BLOCK, (sub + 1) * BLOCK)
        valid = band & (ci >= first_key) if sub == 0 else band
        for pair in range(HEADS // 2):
            cs = slice(pair * LANES, (pair + 1) * LANES)
            qp = q_ref[0, 0, rows, cs]
            if sub == 0:
                kp = jnp.concatenate([kp_ref[0, 0, :, cs], kc_ref[0, 0, 0:BLOCK, cs]], axis=0)
                vp = jnp.concatenate([vp_ref[0, 0, :, cs], vc_ref[0, 0, 0:BLOCK, cs]], axis=0)
            else:
                kp = kc_ref[0, 0, (sub - 1) * BLOCK:(sub + 1) * BLOCK, cs]
                vp = vc_ref[0, 0, (sub - 1) * BLOCK:(sub + 1) * BLOCK, cs]
            acc, tops = None, []
            for half in range(2):
                own_q = low_q if half == 0 else jnp.logical_not(low_q)
                own_v = low_v if half == 0 else jnp.logical_not(low_v)
                s = _dot_nt(jnp.where(own_q, qp, jnp.zeros_like(qp)), kp) + bias_ref[2 * pair + half]
                s = jnp.where(valid, s, NEG_INF)
                m = jnp.max(s, axis=-1, keepdims=True)
                p = jnp.exp2(s - m).astype(BF16)
                w = jnp.concatenate([jnp.where(own_v, vp, jnp.zeros_like(vp)), ones_lo if half == 0 else ones_hi],
                                    axis=1)
                r = _dot(p, w)
                acc = r if acc is None else acc + r
                tops.append(m)
            den = acc[:, LANES:]
            o_ref[0, 0, rows, cs] = (acc[:, :LANES] / den).astype(BF16)
            lse_ref[0, 0, rows, cs] = jnp.where(low_q, tops[0], tops[1]) * LN2 + jnp.log(den)


def _attn_prompt_group(q4, k4, v4, bias, dil, n_sub):
    b_sz, _, m_len, _ = q4.shape
    span = n_sub * BLOCK
    cur = pl.BlockSpec((1, 1, span, D_A), lambda b, r, n: (b, r, n, 0))
    prev = pl.BlockSpec((1, 1, BLOCK, D_A), lambda b, r, n: (b, r, jnp.maximum(n * n_sub - 1, 0), 0))
    return pl.pallas_call(
        _attn_prompt_body,
        out_shape=[jax.ShapeDtypeStruct((b_sz, dil, m_len, D_A), BF16),
                   jax.ShapeDtypeStruct((b_sz, dil, m_len, D_A), F32)],
        grid=(b_sz, dil, m_len // span),
        in_specs=[cur, prev, cur, prev, cur,
                  pl.BlockSpec((HEADS, BLOCK, 2 * BLOCK), lambda b, r, n: (0, 0, 0))],
        out_specs=[cur, cur],
        compiler_params=_params(3),
        name=f"attn_prompt_d{dil}",
    )(q4, k4, k4, v4, v4, bias)


def _sample_cache_step(unit, refs, t_new):
    refs = list(refs)
    q_ref = refs.pop(0)
    new_refs = [refs.pop(0) for _ in range(N_DIL)]
    cache_refs = [refs.pop(0) for _ in range(N_DIL)]
    bias_refs = [refs.pop(0) for _ in range(N_DIL)]
    out_refs = [refs.pop(0) for _ in range(N_DIL)]
    o_ref = refs.pop(0)
    p_refs = [refs.pop(0) for _ in range(N_DIL)]
    scale_ref = refs.pop(0)
    kv = lax.rem(unit, 2)
    rows = HEADS * t_new

    def allowed(g, col0, cols):
        win, dil = DILATED_GROUPS[g]
        tok = jnp.bitwise_and(lax.broadcasted_iota(jnp.int32, (rows, cols), 0), t_new - 1)
        col = lax.broadcasted_iota(jnp.int32, (rows, cols), 1) + col0
        back = win + tok - col
        return ((col < win + t_new) & (back >= 0) & (jnp.bitwise_and(back, dil - 1) == 0)
                & (back <= STEPS * dil))

    lane = lax.broadcasted_iota(jnp.int32, (D_A, LANES), 1)
    front = jnp.zeros((LANES - t_new, D_A), F32)
    for g in range(N_DIL):
        win = cache_refs[g].shape[-1]
        new = new_refs[g][0]
        half = jnp.where(kv == 0, new[:, 0:D_A], new[:, D_A:2 * D_A])
        tail = jnp.transpose(jnp.concatenate([front, half], axis=0))
        rolled = pltpu.roll(cache_refs[g][0, 0], win - t_new, axis=1)
        if win > LANES:
            out_refs[g][0, 0, :, 0:win - LANES] = rolled[:, 0:win - LANES]
        out_refs[g][0, 0, :, win - LANES:win] = jnp.where(lane >= LANES - t_new, tail, rolled[:, win - LANES:win])

    ri = lax.broadcasted_iota(jnp.int32, (rows, D_A), 0)
    ci = lax.broadcasted_iota(jnp.int32, (rows, D_A), 1)
    own_head = jnp.right_shift(ri, int(math.log2(t_new))) == jnp.right_shift(ci, int(math.log2(HEAD_DIM)))
    pad = jnp.zeros((LANES - t_new, D_A), F32)
    glane = lax.broadcasted_iota(jnp.int32, (rows, LANES), 1)

    @pl.when(kv == 0)
    def _():
        q = q_ref[...]
        stats = []
        for g in range(N_DIL):
            win = cache_refs[g].shape[-1]
            qg = q[:, g * D_A:(g + 1) * D_A]
            qblk = jnp.where(own_head, jnp.concatenate([qg] * HEADS, axis=0), 0.0).astype(BF16)
            s_c = _dot(qblk, cache_refs[g][0, 0].astype(BF16)) + bias_refs[g][:, 0:win]
            s_c = jnp.where(allowed(g, 0, win), s_c, NEG_INF)
            k_new = jnp.concatenate([new_refs[g][0][:, 0:D_A], pad], axis=0).astype(BF16)
            s_n = _dot_nt(qblk, k_new) + bias_refs[g][:, win:]
            s_n = jnp.where(allowed(g, win, LANES), s_n, NEG_INF)
            m = jnp.maximum(jnp.max(s_c, axis=-1, keepdims=True), jnp.max(s_n, axis=-1, keepdims=True))
            p_c = jnp.exp(s_c - m)
            p_n = jnp.exp(s_n - m)
            l = jnp.sum(p_c, axis=-1, keepdims=True) + jnp.sum(p_n, axis=-1, keepdims=True)
            p_refs[g][:, 0:win] = p_c.astype(BF16)
            p_refs[g][:, win:] = p_n.astype(BF16)
            stats.append((l, m + jnp.log(l)))
        top = jnp.maximum(jnp.maximum(stats[0][1], stats[1][1]), stats[2][1])
        es = [jnp.exp(lse - top) for _, lse in stats]
        den = es[0] + es[1] + es[2]
        scale = jnp.zeros((rows, LANES), F32)
        for g in range(N_DIL):
            scale = jnp.where(glane == g, es[g] / den / stats[g][0], scale)
        scale_ref[...] = scale

    @pl.when(kv == 1)
    def _():
        acc = jnp.zeros((rows, D_A), F32)
        for g in range(N_DIL):
            win = cache_refs[g].shape[-1]
            part = _dot_nt(p_refs[g][:, 0:win], cache_refs[g][0, 0].astype(BF16))
            v_new = jnp.concatenate([new_refs[g][0][:, D_A:2 * D_A], pad], axis=0).astype(BF16)
            part = part + _dot(p_refs[g][:, win:], v_new)
            acc = acc + scale_ref[:, g:g + 1] * part
        acc = jnp.where(own_head, acc, 0.0)
        out = acc[0:t_new]
        for h in range(1, HEADS):
            out = out + acc[h * t_new:(h + 1) * t_new]
        o_ref[...] = out


def _sample_cache_specs(q_s, kv_new, caches_t, biases, t_new):
    n = q_s.shape[0]
    nb = n // t_new
    blk4 = lambda i: (i // 2, lax.rem(i, 2), 0, 0)
    in_specs = [pl.BlockSpec((t_new, D_QKV), lambda i: (i // 2, 0))]
    in_specs += [pl.BlockSpec((1, t_new, 2 * D_A), lambda i: (i // 2, 0, 0))] * N_DIL
    in_specs += [pl.BlockSpec((1, 1, D_A, c.shape[-1]), blk4) for c in caches_t]
    in_specs += [pl.BlockSpec(a.shape, lambda i: (0, 0), pipeline_mode=pl.Buffered(1)) for a in biases]
    out_shape = [jax.ShapeDtypeStruct(c.shape, c.dtype) for c in caches_t]
    out_specs = [pl.BlockSpec((1, 1, D_A, c.shape[-1]), blk4) for c in caches_t]
    out_shape.append(jax.ShapeDtypeStruct((n, D_A), F32))
    out_specs.append(pl.BlockSpec((t_new, D_A), lambda i: (i // 2, 0)))
    rows = HEADS * t_new
    scratch = [pltpu.VMEM((rows, c.shape[-1] + LANES), BF16) for c in caches_t]
    scratch.append(pltpu.VMEM((rows, LANES), F32))
    args = [q_s, *[a.reshape(nb, t_new, 2 * D_A) for a in kv_new], *caches_t, *biases]
    return args, in_specs, out_shape, out_specs, scratch, 2 * nb


def _stage2_body(*refs, merged):
    refs = list(refs)
    if merged:
        oa_ref = refs.pop(0)
    else:
        o_refs = [refs.pop(0) for _ in range(N_DIL)]
        l_refs = [refs.pop(0) for _ in range(N_DIL)]
    (ob_ref, ga_ref, gb_ref, x_ref, gt1_ref, sh2_ref, sc2_ref, wua_ref, wub_ref, wo_ref, g2_ref,
     wr_ref, br_ref, x1_ref, h2_ref, route_ref, code_ref, cnt_ref, run_ref) = refs[:19]
    step = pl.program_id(0)
    tm = x_ref.shape[0]

    @pl.when(step == 0)
    def _():
        run_ref[...] = jnp.zeros_like(run_ref)

    if merged:
        oa = oa_ref[...].astype(BF16)
    else:
        o_stage, l_stage = refs[19], refs[20]

        def by_position(ref, stage, dil):
            if dil == 1:
                return ref[0, 0].astype(F32)
            slabs = ref.shape[-1] // LANES
            for r in range(dil):
                rows = ref[0, r].astype(F32)
                for c in range(slabs):
                    stage[c, pl.ds(r, tm // dil, stride=dil), :] = rows[:, c * LANES:(c + 1) * LANES]
            return jnp.concatenate([stage[c] for c in range(slabs)], axis=1)

        ls = [by_position(l_refs[g], l_stage, dil) for g, (_, dil) in enumerate(DILATED_GROUPS)]
        top = jnp.maximum(jnp.maximum(ls[0], ls[1]), ls[2])
        es = [jnp.exp(l - top) for l in ls]
        den = es[0] + es[1] + es[2]
        oa = None
        for g, (_, dil) in enumerate(DILATED_GROUPS):
            term = es[g] * by_position(o_refs[g], o_stage, dil)
            oa = term if oa is None else oa + term
        oa = (oa / den).astype(BF16)

    ya = _dot(oa, wua_ref[...])
    yb = _dot(ob_ref[...], wub_ref[...])
    z = (ga_ref[...].astype(F32) * ya + gb_ref[...].astype(F32) * yb).astype(BF16)
    x1 = x_ref[...] + gt1_ref[0] * _dot(z, wo_ref[...])
    x1_ref[...] = x1
    h2 = x1 * lax.rsqrt(jnp.mean(x1 * x1, axis=-1, keepdims=True) + EPS) * g2_ref[...]
    h2 = h2 * (1.0 + sc2_ref[0]) + sh2_ref[0]
    h2_ref[...] = h2

    h_hi = h2.astype(BF16)
    h_lo = (h2 - h_hi.astype(F32)).astype(BF16)
    both = _dot(h_hi, wr_ref[...])
    logit = (both[:, 0:LANES] + both[:, LANES:] + _dot(h_lo, wr_ref[:, 0:LANES])) + br_ref[...]
    lane_i = lax.broadcasted_iota(jnp.int32, (tm, LANES), 1)
    lane = lane_i.astype(F32)
    big = float(LANES)
    is_g = (lane_i >= N_EXPERTS) & (lane_i < N_EXPERTS + MOE_GROUPS)
    gl = jnp.where(is_g, logit, -jnp.inf)
    gmax = jnp.max(gl, axis=-1, keepdims=True)
    g_idx = jnp.min(jnp.where(gl == gmax, lane, big), axis=-1, keepdims=True) - N_EXPERTS
    g_w = 1.0 / jnp.sum(jnp.where(is_g, jnp.exp(logit - gmax), 0.0), axis=-1, keepdims=True)
    lo = g_idx * EXPERTS_PER_GROUP
    el = jnp.where((lane >= lo) & (lane < lo + EXPERTS_PER_GROUP), logit, -jnp.inf)
    t1 = jnp.max(el, axis=-1, keepdims=True)
    i1 = jnp.min(jnp.where(el == t1, lane, big), axis=-1, keepdims=True)
    el2 = jnp.where(lane == i1, -jnp.inf, el)
    t2 = jnp.max(el2, axis=-1, keepdims=True)
    i2 = jnp.min(jnp.where(el2 == t2, lane, big), axis=-1, keepdims=True)
    d = jnp.exp(t2 - t1)
    cw1 = g_w * (1.0 / (1.0 + d))
    cw2 = g_w * (d / (1.0 + d))

    oh1 = lane == i1
    oh2 = lane == i2
    picked = jnp.where(oh1, 1, jnp.where(oh2, 2, 0))
    hit = jnp.minimum(picked, 1)
    row_i = lax.broadcasted_iota(jnp.int32, (tm, LANES), 0)
    seen = hit
    shift = 1
    while shift < tm:
        seen = seen + jnp.where(row_i >= shift, pltpu.roll(seen, shift, axis=0), 0)
        shift *= 2
    run = run_ref[0:1, :]
    code_ref[...] = jnp.where(hit > 0, picked + 4 * (seen - hit + run), 0)
    run_new = jnp.broadcast_to(run + seen[tm - 1:tm, :], run_ref.shape)
    run_ref[...] = run_new
    cnt_ref[...] = run_new

    route_ref[...] = jnp.where(lane_i == 0, cw1, jnp.where(lane_i == 1, cw2, 0.0))


def _stage2(attn_in, ob, ga, gb, x2d, gt1, sh2, sc2, wua, wub, wo, g2, wr_pair, br, *, tm, mod_index, stride_index,
            merged, name):
    n = x2d.shape[0]
    row = lambda i: (i, 0)
    const2 = lambda i: (0, 0)
    scratch = [pltpu.VMEM((8, LANES), jnp.int32)]
    if merged:
        attn_specs = [pl.BlockSpec((tm, D_A), row)]
    else:
        attn_specs = [pl.BlockSpec((1, dil, tm // dil, D_A), stride_index) for _, dil in DILATED_GROUPS]
        attn_specs += [pl.BlockSpec((1, dil, tm // dil, D_A), stride_index) for _, dil in DILATED_GROUPS]
        scratch += [pltpu.VMEM((D_A // LANES, tm, LANES), F32), pltpu.VMEM((D_A // LANES, tm, LANES), F32)]
    in_specs = attn_specs + [
        pl.BlockSpec((tm, D_B), row), pl.BlockSpec((tm, D_MODEL), row), pl.BlockSpec((tm, D_MODEL), row),
        pl.BlockSpec((tm, D_MODEL), row),
        _mod_spec(gt1, mod_index), _mod_spec(sh2, mod_index), _mod_spec(sc2, mod_index),
        pl.BlockSpec((D_A, D_MODEL), const2), pl.BlockSpec((D_B, D_MODEL), const2),
        pl.BlockSpec((D_MODEL, D_MODEL), const2), pl.BlockSpec((1, D_MODEL), const2),
        pl.BlockSpec((D_MODEL, 2 * LANES), const2),
        pl.BlockSpec((1, LANES), const2),
    ]
    out_shape = [jax.ShapeDtypeStruct((n, D_MODEL), F32), jax.ShapeDtypeStruct((n, D_MODEL), F32),
                 jax.ShapeDtypeStruct((n, LANES), F32), jax.ShapeDtypeStruct((n, LANES), jnp.int32),
                 jax.ShapeDtypeStruct((8, LANES), jnp.int32)]
    out_specs = [pl.BlockSpec((tm, D_MODEL), row), pl.BlockSpec((tm, D_MODEL), row),
                 pl.BlockSpec((tm, LANES), row), pl.BlockSpec((tm, LANES), row), pl.BlockSpec((8, LANES), const2)]
    return pl.pallas_call(
        functools.partial(_stage2_body, merged=merged),
        out_shape=out_shape, grid=(n // tm,), in_specs=in_specs, out_specs=out_specs,
        scratch_shapes=scratch,
        compiler_params=_params(1), name=name,
    )(*attn_in, ob, ga, gb, x2d, gt1[0], sh2[0], sc2[0], wua, wub, wo, g2, wr_pair, br)


def _dispatch_body(cnt_ref, off_ref, pad_ref, nv_ref, posp_ref, poss_ref, hp_ref, hs_ref, xs_ref,
                   zero_ref, buf_ref, load_sem, scat_sem, zero_sem, *, tiles_p, n_tiles):
    step = pl.program_id(0)
    tile = zero_ref.shape[0]
    groups = buf_ref.shape[1]
    tm = groups * ROW_TILE
    groups_s = hs_ref.shape[0]
    n_s = groups_s * ROW_TILE
    slot = lax.rem(step, 3)
    ahead = lax.rem(step + 1, 3)

    def load_prompt(j, s):
        return pltpu.make_async_copy(hp_ref.at[pl.ds(j * groups, groups)], buf_ref.at[s], load_sem.at[s])

    def load_sample(s):
        return pltpu.make_async_copy(hs_ref, buf_ref.at[s, pl.ds(0, groups_s)], load_sem.at[s])

    def row_copy(s, group, j, dst_row):
        return pltpu.make_async_copy(buf_ref.at[s, group, pl.ds(j, 1)], xs_ref.at[pl.ds(dst_row, 1)], scat_sem.at[s])

    def scatter(rows, pos_ref, s):
        def issue(group, c):
            for j in range(ROW_TILE):
                t = group * ROW_TILE + j
                row_copy(s, group, j, pos_ref[0, 0, 2 * t]).start()
                row_copy(s, group, j, pos_ref[0, 0, 2 * t + 1]).start(priority=1)
            return c

        lax.fori_loop(0, rows // ROW_TILE, issue, 0)

    def drain(rows, s):
        def one(t, c):
            row_copy(s, 0, 0, 0).wait()
            row_copy(s, 0, 0, 0).wait()
            return c

        lax.fori_loop(0, rows, one, 0, unroll=8)

    @pl.when(step == 0)
    def _():
        zero_ref[...] = jnp.zeros_like(zero_ref)
        load_prompt(0, 0).start()
        bits = int(math.log2(tile))

        def pad_copies(e, wait):
            n = pad_ref[e] - cnt_ref[e]
            base = off_ref[e] + cnt_ref[e]
            end = off_ref[e] + pad_ref[e]

            def one_row(i, c):
                copy = _row_copy(zero_ref, 0, xs_ref, base + i, zero_sem)
                if wait:
                    copy.wait()
                else:
                    copy.start()
                return c

            lax.fori_loop(0, jnp.bitwise_and(n, 7), one_row, 0)
            for b in range(3, bits):
                size = 1 << b
                first = pl.multiple_of(end - jnp.bitwise_and(n, ~(2 * size - 1)) - size, 8)
                copy = pltpu.make_async_copy(zero_ref.at[pl.ds(0, size)], xs_ref.at[pl.ds(first, size)], zero_sem)

                @pl.when(jnp.bitwise_and(n, size) != 0)
                def _():
                    if wait:
                        copy.wait()
                    else:
                        copy.start()

        def start_pads(e, c):
            pad_copies(e, False)
            return c

        def wait_pads(e, c):
            pad_copies(e, True)
            return c

        def tile_copy(j):
            return pltpu.make_async_copy(zero_ref, xs_ref.at[pl.ds(j * tile, tile)], zero_sem)

        def start_tile(j, c):
            tile_copy(j).start()
            return c

        def wait_tile(j, c):
            tile_copy(j).wait()
            return c

        lax.fori_loop(0, N_EXPERTS, start_pads, 0)
        lax.fori_loop(nv_ref[0], n_tiles, start_tile, 0)
        lax.fori_loop(0, N_EXPERTS, wait_pads, 0)
        lax.fori_loop(nv_ref[0], n_tiles, wait_tile, 0)

    @pl.when(step >= 2)
    def _():
        drain(tm, ahead)

    @pl.when(step + 1 < tiles_p)
    def _():
        load_prompt(step + 1, ahead).start()

    @pl.when(step + 1 == tiles_p)
    def _():
        load_sample(ahead).start()

    @pl.when(step < tiles_p)
    def _():
        load_prompt(step, slot).wait()
        scatter(tm, posp_ref, slot)

    @pl.when(step == tiles_p)
    def _():
        load_sample(slot).wait()
        scatter(n_s, poss_ref, slot)
        drain(tm, lax.rem(step + 2, 3))
        drain(n_s, slot)


def _dispatch(count, offset, padded, n_valid, pos_p, pos_s, h2_p, h2_s, *, cap, tm, tile):
    n_p, n_s = h2_p.shape[0], h2_s.shape[0]
    tiles_p = n_p // tm
    assert tiles_p >= 2 and n_s <= tm
    last_p = lambda i, *_: (jnp.minimum(i, tiles_p - 1), 0, 0)
    in_specs = [pl.BlockSpec((1, 1, 2 * tm), last_p, memory_space=pltpu.SMEM),
                pl.BlockSpec((1, 1, 2 * n_s), lambda i, *_: (0, 0, 0), memory_space=pltpu.SMEM),
                pl.BlockSpec(memory_space=pl.ANY),
                pl.BlockSpec(memory_space=pl.ANY)]
    grid_spec = pltpu.PrefetchScalarGridSpec(
        num_scalar_prefetch=4, grid=(tiles_p + 1,), in_specs=in_specs,
        out_specs=pl.BlockSpec(memory_space=pl.ANY),
        scratch_shapes=[pltpu.VMEM((tile, D_MODEL), F32), pltpu.VMEM((3, tm // ROW_TILE, ROW_TILE, D_MODEL), F32),
                        pltpu.SemaphoreType.DMA((3,)), pltpu.SemaphoreType.DMA((3,)), pltpu.SemaphoreType.DMA(())],
    )
    by_tile = lambda a: a.reshape(a.shape[0] // ROW_TILE, ROW_TILE, D_MODEL)
    return pl.pallas_call(
        functools.partial(_dispatch_body, tiles_p=tiles_p, n_tiles=cap // tile),
        out_shape=jax.ShapeDtypeStruct((cap, D_MODEL), F32), grid_spec=grid_spec,
        compiler_params=pltpu.CompilerParams(dimension_semantics=("arbitrary",), vmem_limit_bytes=VMEM_LIMIT,
                                             has_side_effects=True),
        name="dispatch",
    )(count, offset, padded, n_valid, pos_p.reshape(tiles_p, 1, 2 * tm), pos_s.reshape(1, 1, 2 * n_s),
      by_tile(h2_p), by_tile(h2_s))


def _moe_body(te_ref, nv_ref, run_ref, nxt_ref, x_ref, wg_hbm, wu_hbm, wd_hbm, y_ref,
              wg_f, wu_f, wd_f, wg_s, wu_s, wd_s, sem):
    i = pl.program_id(0)
    active = i < nv_ref[0]
    fresh = (i == 0) | (te_ref[i] != te_ref[jnp.maximum(i - 1, 0)])
    slot = lax.rem(run_ref[i], 2)

    def fetch(e, s):
        return (pltpu.make_async_copy(wg_hbm.at[e], wg_f.at[s], sem.at[s]),
                pltpu.make_async_copy(wu_hbm.at[e], wu_f.at[s], sem.at[s]),
                pltpu.make_async_copy(wd_hbm.at[e], wd_f.at[s], sem.at[s]))

    @pl.when(active & (i == 0))
    def _():
        for copy in fetch(te_ref[0], 0):
            copy.start()

    @pl.when(active & fresh)
    def _():
        for copy in fetch(te_ref[i], slot):
            copy.wait()

        @pl.when(nxt_ref[i] >= 0)
        def _():
            for copy in fetch(nxt_ref[i], 1 - slot):
                copy.start()

        wg_s[...] = wg_f[slot].astype(BF16)
        wu_s[...] = wu_f[slot].astype(BF16)
        wd_s[...] = wd_f[slot].astype(BF16)

    @pl.when(active)
    def _():
        x = x_ref[...].astype(BF16)
        a = _dot(x, wg_s[...])
        b = _dot(x, wu_s[...])
        mid = (a * _sigmoid(a) * b).astype(BF16)
        y_ref[...] = _dot(mid, wd_s[...])

    @pl.when(jnp.logical_not(active))
    def _():
        y_ref[...] = jnp.zeros_like(y_ref)


def _moe(tile_expert, n_valid, x_sorted, w_eg, w_eu, w_ed, tm):
    cap = x_sorted.shape[0]
    n_tiles = cap // tm
    tile = jnp.arange(n_tiles, dtype=jnp.int32)
    live = tile < n_valid[0]
    starts = jnp.concatenate([jnp.zeros((1,), jnp.bool_), tile_expert[1:] != tile_expert[:-1]]) & live
    run = jnp.cumsum(starts.astype(jnp.int32))
    later = (tile_expert[None, :] > tile_expert[:, None]) & live[None, :]
    nxt = jnp.min(jnp.where(later, tile_expert[None, :], N_EXPERTS), axis=1)
    nxt = jnp.where(nxt < N_EXPERTS, nxt, -1).astype(jnp.int32)

    xmap = lambda i, te, nv, rn, nx: (jnp.maximum(jnp.minimum(i, nv[0] - 1), 0), 0)
    grid_spec = pltpu.PrefetchScalarGridSpec(
        num_scalar_prefetch=4,
        grid=(n_tiles,),
        in_specs=[pl.BlockSpec((tm, D_MODEL), xmap),
                  pl.BlockSpec(memory_space=pl.ANY), pl.BlockSpec(memory_space=pl.ANY),
                  pl.BlockSpec(memory_space=pl.ANY)],
        out_specs=pl.BlockSpec((tm, D_MODEL), lambda i, te, nv, rn, nx: (i, 0)),
        scratch_shapes=[pltpu.VMEM((2, D_MODEL, D_EXPERT), F32), pltpu.VMEM((2, D_MODEL, D_EXPERT), F32),
                        pltpu.VMEM((2, D_EXPERT, D_MODEL), F32),
                        pltpu.VMEM((D_MODEL, D_EXPERT), BF16), pltpu.VMEM((D_MODEL, D_EXPERT), BF16),
                        pltpu.VMEM((D_EXPERT, D_MODEL), BF16), pltpu.SemaphoreType.DMA((2,))],
    )
    return pl.pallas_call(
        _moe_body, out_shape=jax.ShapeDtypeStruct((cap, D_MODEL), F32), grid_spec=grid_spec,
        compiler_params=_params(1), name="moe",
    )(tile_expert, n_valid, run, nxt, x_sorted, w_eg, w_eu, w_ed)


def _final_body(pos_ref, next_ref, x1_ref, route_ref, gt2_ref, gf_ref, ys_ref, o_ref, y_buf, sem):
    step = pl.program_id(0)
    tm = x1_ref.shape[0]
    slot = lax.rem(step, 2)

    def row_copy(s, k, group, j, src_row):
        return pltpu.make_async_copy(ys_ref.at[pl.ds(src_row, 1)], y_buf.at[s, k, group, pl.ds(j, 1)], sem.at[s])

    def gather(p_ref, s):
        def issue(group, c):
            for j in range(ROW_TILE):
                t = group * ROW_TILE + j
                row_copy(s, 0, group, j, p_ref[0, 0, 2 * t]).start()
                row_copy(s, 1, group, j, p_ref[0, 0, 2 * t + 1]).start(priority=1)
            return c

        lax.fori_loop(0, tm // ROW_TILE, issue, 0)

    @pl.when(step == 0)
    def _():
        gather(pos_ref, 0)

    @pl.when(step + 1 < pl.num_programs(0))
    def _():
        gather(next_ref, 1 - slot)

    def drain(t, c):
        row_copy(slot, 0, 0, 0, 0).wait()
        row_copy(slot, 1, 0, 0, 0).wait()
        return c

    lax.fori_loop(0, tm, drain, 0, unroll=8)

    route = route_ref[...]
    moe = (route[:, 0:1] * y_buf[slot, 0].reshape(tm, D_MODEL)
           + route[:, 1:2] * y_buf[slot, 1].reshape(tm, D_MODEL))
    x = x1_ref[...] + gt2_ref[0] * moe
    o_ref[...] = x * lax.rsqrt(jnp.mean(x * x, axis=-1, keepdims=True) + EPS) * gf_ref[...]


def _final(pos, x1, route, gt2, gf, y_sorted, *, tm, mod_index, name):
    n = x1.shape[0]
    steps = n // tm
    row = lambda i: (i, 0)
    pos3 = pos.reshape(steps, 1, 2 * tm)
    return pl.pallas_call(
        _final_body,
        out_shape=jax.ShapeDtypeStruct((n, D_MODEL), F32),
        grid=(steps,),
        in_specs=[pl.BlockSpec((1, 1, 2 * tm), lambda i: (i, 0, 0), memory_space=pltpu.SMEM),
                  pl.BlockSpec((1, 1, 2 * tm), lambda i: (jnp.minimum(i + 1, steps - 1), 0, 0),
                               memory_space=pltpu.SMEM),
                  pl.BlockSpec((tm, D_MODEL), row), pl.BlockSpec((tm, LANES), row),
                  _mod_spec(gt2, mod_index), pl.BlockSpec((1, D_MODEL), lambda i: (0, 0)),
                  pl.BlockSpec(memory_space=pl.ANY)],
        out_specs=pl.BlockSpec((tm, D_MODEL), row),
        scratch_shapes=[pltpu.VMEM((2, 2, tm // ROW_TILE, ROW_TILE, D_MODEL), F32), pltpu.SemaphoreType.DMA((2,))],
        compiler_params=_params(1), name=name,
    )(pos3, pos3, x1, route, gt2[0], gf, y_sorted)


def _t5_bucket(dist):
    max_exact = N_BUCKETS // 2
    dist = np.asarray(dist)
    d = np.maximum(dist, 1).astype(np.float32)
    large = max_exact + (np.log(d / np.float32(max_exact)) / np.float32(math.log(MAX_DISTANCE / max_exact))
                         * np.float32(N_BUCKETS - max_exact)).astype(np.int32)
    return np.where(dist < max_exact, dist, np.minimum(large, N_BUCKETS - 1)).astype(np.int32)


def _bias_lookup(tab, dist):
    onehot = (jnp.asarray(_t5_bucket(dist))[..., None] == jnp.arange(N_BUCKETS)).astype(F32)
    return jnp.einsum("...b,bh->...h", onehot, tab, precision=HIGHEST)


def _prompt_bias(tab, dil):
    delta = np.arange(BLOCK)[:, None] + BLOCK - np.arange(2 * BLOCK)[None, :]
    return jnp.transpose(_bias_lookup(tab, np.maximum(delta, 0) * dil), (2, 0, 1))


def _sample_bias(tab, win, dil, t_new):
    cols = win + LANES
    back = win + np.arange(t_new)[:, None] - np.arange(cols)[None, :]
    bias = _bias_lookup(tab, np.maximum(back, 0))
    return jnp.transpose(bias, (2, 0, 1)).reshape(HEADS * t_new, cols)


def kernel(x_prompt, x_sample, c_prompt, c_sample, cache_kv_w128, cache_kv_w512, cache_kv_w2048, rel_bias, w_ada,
           b_ada, g_norm1, w_in, ln_v_g, ln_v_b, w_spatial, b_spatial, w_up_a, w_up_b, w_out, g_norm2,
           w_route_group, b_route_group, w_route_expert, b_route_expert, w_e_gate, w_e_up, w_e_down, g_final):
    assert w_ada.shape[0] == 1, "single layer"
    bp, s_len, _ = x_prompt.shape
    bs, t_new, _ = x_sample.shape
    n_p, n_s = bp * s_len, bs * t_new
    caches = (cache_kv_w128[0], cache_kv_w512[0], cache_kv_w2048[0])
    assert all(c.shape[1] == win for c, (win, _) in zip(caches, DILATED_GROUPS)), "cache holds one full window"

    mod = _ada(jnp.concatenate([c_prompt, c_sample], axis=0), w_ada[0], b_ada[0])
    mod_p = [(mod[:bp].reshape(bp, 1, 6 * D_MODEL), j) for j in range(6)]
    mod_s = [(jnp.repeat(mod[bp:], t_new, axis=0).reshape(1, n_s, 6 * D_MODEL), j) for j in range(6)]

    w_in_bf = w_in[0].astype(BF16)
    g1 = g_norm1[0].reshape(1, D_MODEL)
    lng, lnb = ln_v_g[0].reshape(1, D_B), ln_v_b[0].reshape(1, D_B)
    ws_p = w_spatial[0]
    bs_p = jnp.repeat(jnp.transpose(b_spatial[0]), SGU_DIM, axis=1)
    reps = n_s // t_new
    corner = w_spatial[0][:, :t_new, :t_new]
    ws_s = (jnp.eye(reps, dtype=F32)[None, :, None, :, None] * corner[:, None, :, None, :]).reshape(
        SGU_GROUPS, n_s, n_s)
    bs_s = jnp.broadcast_to(bs_p[None, :t_new], (reps, t_new, D_B)).reshape(n_s, D_B)

    tpb = s_len // TM_STAGE1
    tail_p = [(bp, min(win, s_len)) for win, _ in DILATED_GROUPS]

    def tail_index_p(i, g, rows):
        first = tpb - tail_p[g][1] // rows
        return (i // tpb, jnp.maximum(i % tpb - first, 0), 0)

    q_s, kv0_s, kv1_s, kv2_s, ob_s, ga_s, gb_s, vn_s = _stage1(
        x_sample.reshape(n_s, D_MODEL), mod_s[0], mod_s[1], g1, w_in_bf, lng, lnb, ws_s, bs_s,
        tm=n_s, mod_index=lambda i: (0, 0, 0), stride_index=None, tail_shapes=[(1, n_s)] * N_DIL,
        tail_index=lambda i, g, rows: (0, 0, 0),
        period=t_new, by_stride=False, emit_vn=True, name="stage1_sample")

    tabs = [rel_bias[:, g * HEADS:(g + 1) * HEADS].astype(F32) for g in range(N_DIL)]
    sample_bias = [_sample_bias(tabs[g], win, dil, t_new) for g, (win, dil) in enumerate(DILATED_GROUPS)]
    caches_t = [jnp.transpose(c, (0, 2, 3, 4, 1)).reshape(bs, 2, D_A, c.shape[1]) for c in caches]
    outs = _stage1(
        x_prompt.reshape(n_p, D_MODEL), mod_p[0], mod_p[1], g1, w_in_bf, lng, lnb, ws_p, bs_p,
        tm=TM_STAGE1, mod_index=lambda i: (i // tpb, 0, 0),
        stride_index=(bp, s_len, lambda i: (i // tpb, 0, i % tpb, 0)), tail_shapes=tail_p, tail_index=tail_index_p,
        period=CHUNK, by_stride=True, emit_vn=False, name="stage1_prompt",
        cache=(q_s, (kv0_s, kv1_s, kv2_s), caches_t, sample_bias, t_new))
    q_g, k_g, v_g = outs[0:3], outs[3:6], outs[6:9]
    kv0_p, kv1_p, kv2_p, ob_p, ga_p, gb_p, cache0_t, cache1_t, cache2_t, oa_s = outs[9:]

    o_groups, lse_groups = [], []
    for g, (win, dil) in enumerate(DILATED_GROUPS):
        o, lse = _attn_prompt_group(q_g[g], k_g[g], v_g[g], _prompt_bias(tabs[g], dil) * LOG2E, dil,
                                    min(ATTN_BLOCKS_PER_STEP, s_len // dil // BLOCK))
        o_groups.append(o)
        lse_groups.append(lse)

    wua, wub, wo = w_up_a[0].astype(BF16), w_up_b[0].astype(BF16), w_out[0].astype(BF16)
    g2 = g_norm2[0].reshape(1, D_MODEL)
    wr = jnp.concatenate([w_route_expert[0].reshape(D_MODEL, N_EXPERTS), w_route_group[0]], axis=1)
    wr = jnp.pad(wr, ((0, 0), (0, LANES - wr.shape[1])))
    br = jnp.concatenate([b_route_expert[0].reshape(N_EXPERTS), b_route_group[0]])
    br = jnp.pad(br, (0, LANES - br.shape[0])).reshape(1, LANES)
    wr_hi = wr.astype(BF16)
    wr_pair = jnp.concatenate([wr_hi, (wr - wr_hi.astype(F32)).astype(BF16)], axis=1)

    tpb2 = s_len // TM_STAGE2
    x1_p, h2_p, route_p, code_p, cnt_p = _stage2(
        o_groups + lse_groups, ob_p, ga_p, gb_p, x_prompt.reshape(n_p, D_MODEL),
        mod_p[2], mod_p[3], mod_p[4], wua, wub, wo, g2, wr_pair, br,
        tm=TM_STAGE2, mod_index=lambda i: (i // tpb2, 0, 0), stride_index=lambda i: (i // tpb2, 0, i % tpb2, 0),
        merged=False, name="stage2_prompt")
    x1_s, h2_s, route_s, code_s, cnt_s = _stage2(
        [oa_s], ob_s, ga_s, gb_s, x_sample.reshape(n_s, D_MODEL),
        mod_s[2], mod_s[3], mod_s[4], wua, wub, wo, g2, wr_pair, br,
        tm=n_s, mod_index=lambda i: (0, 0, 0), stride_index=None, merged=True, name="stage2_sample")

    cap = 2 * (n_p + n_s) + N_EXPERTS * TM_MOE
    n_tiles = cap // TM_MOE
    count_p = cnt_p[0, :N_EXPERTS]
    count = count_p + cnt_s[0, :N_EXPERTS]
    padded = ((count + TM_MOE - 1) // TM_MOE) * TM_MOE
    ends = jnp.cumsum(padded)
    offset = ends - padded
    n_valid = (ends[-1] // TM_MOE).astype(jnp.int32).reshape(1)
    tile_start = jnp.arange(n_tiles, dtype=jnp.int32) * TM_MOE
    tile_expert = jnp.minimum(jnp.sum(ends[None, :] <= tile_start[:, None], axis=1), N_EXPERTS - 1).astype(jnp.int32)

    def sorted_rows(code, earlier):
        row = jnp.pad(offset + earlier, (0, LANES - N_EXPERTS))[None, :] + jnp.right_shift(code, 2)
        return jnp.stack([jnp.sum(jnp.where((code & 3) == k, row, 0), axis=1) for k in (1, 2)], axis=1)

    pos_p = sorted_rows(code_p, jnp.zeros_like(count_p))
    pos_s = sorted_rows(code_s, count_p)
    x_sorted = _dispatch(count, offset, padded, n_valid, pos_p, pos_s, h2_p, h2_s, cap=cap, tm=TM_FINAL, tile=TM_MOE)

    y_sorted = _moe(tile_expert, n_valid, x_sorted, w_e_gate[0], w_e_up[0], w_e_down[0], TM_MOE)

    gf = g_final.reshape(1, D_MODEL)
    tpb3 = s_len // TM_FINAL
    y_p = _final(pos_p, x1_p, route_p, mod_p[5], gf, y_sorted,
                 tm=TM_FINAL, mod_index=lambda i: (i // tpb3, 0, 0), name="final_prompt")
    y_s = _final(pos_s, x1_s, route_s, mod_s[5], gf, y_sorted,
                 tm=n_s, mod_index=lambda i: (0, 0, 0), name="final_sample")

    kv_p = [a.reshape(1, bp, a.shape[1], 2, HEADS, HEAD_DIM) for a in (kv0_p, kv1_p, kv2_p)]
    kv_s = [jnp.transpose(c.reshape(bs, 2, HEADS, HEAD_DIM, c.shape[-1]), (0, 4, 1, 2, 3))[None]
            for c in (cache0_t, cache1_t, cache2_t)]
    return (y_p.reshape(bp, s_len, D_MODEL), y_s.reshape(bs, t_new, D_MODEL),
            kv_p[0], kv_p[1], kv_p[2], kv_s[0], kv_s[1], kv_s[2],
            vn_s.reshape(1, bs, t_new, D_B))
```

```python
import functools
import math

import numpy as np
import jax
import jax.numpy as jnp
from jax import lax
from jax.experimental import pallas as pl
from jax.experimental.pallas import tpu as pltpu

F32 = jnp.float32
BF16 = jnp.bfloat16
HIGHEST = lax.Precision.HIGHEST

D_MODEL = 1024
HEAD_DIM = 64
HEADS = 8
DILATED_GROUPS = ((128, 1), (512, 4), (2048, 16))
N_DIL = 3
D_A = HEADS * HEAD_DIM
D_QKV = N_DIL * D_A
BLOCK = 128
STEPS = 128
SCALE = HEAD_DIM ** -0.5
CHUNK = 128
D_B = 512
SGU_GROUPS = 4
SGU_DIM = D_B // SGU_GROUPS
N_BUCKETS = 32
MAX_DISTANCE = 2048
MOE_GROUPS = 4
EXPERTS_PER_GROUP = 8
N_EXPERTS = 32
D_EXPERT = 512
EPS = 1e-6
NEG_INF = -1e30
LOG2E = math.log2(math.e)
LN2 = math.log(2.0)
LANES = 128
ROW_TILE = 8
C_U = 3 * D_QKV
C_V = C_U + D_B
C_GA = C_V + D_B
C_GB = C_GA + D_MODEL
IN_COLS = C_GB + D_MODEL

TM_STAGE1 = 256
TM_STAGE2 = 512
TM_MOE = 512
ATTN_BLOCKS_PER_STEP = 8
TM_FINAL = 512
VMEM_BYTES = 64 * 1024 * 1024
VMEM_LIMIT = VMEM_BYTES - 8 * 1024 * 1024


def _dot(a, b):
    return jnp.dot(a, b, preferred_element_type=F32)


def _dot_nt(a, b):
    return lax.dot_general(a, b, (((1,), (1,)), ((), ())), preferred_element_type=F32)


def _sigmoid(x):
    return 1.0 / (1.0 + jnp.exp(-x))


def _gelu(x):
    return 0.5 * x * (1.0 + lax.erf(x * (2.0 ** -0.5)))


def _params(n_grid):
    return pltpu.CompilerParams(dimension_semantics=("arbitrary",) * n_grid, vmem_limit_bytes=VMEM_LIMIT)


def _mod_spec(vectors_col, batch_index):
    vectors, col = vectors_col
    return pl.BlockSpec((1, vectors.shape[1], D_MODEL), lambda i: batch_index(i)[:2] + (col,))


def _row_copy(src, src_row, dst, dst_row, sem):
    return pltpu.make_async_copy(src.at[pl.ds(src_row, 1)], dst.at[pl.ds(dst_row, 1)], sem)


def _ada_body(c_ref, w_ref, b_ref, o_ref):
    c = c_ref[...]
    s = c * _sigmoid(c)
    o_ref[...] = jnp.dot(s, w_ref[...], precision=HIGHEST, preferred_element_type=F32) + b_ref[...]


def _ada(c_all, w_ada, b_ada):
    n, d = c_all.shape
    cols = w_ada.shape[1]
    tn = 1024
    return pl.pallas_call(
        _ada_body,
        out_shape=jax.ShapeDtypeStruct((n, cols), F32),
        grid=(cols // tn,),
        in_specs=[pl.BlockSpec((n, d), lambda j: (0, 0)),
                  pl.BlockSpec((d, tn), lambda j: (0, j)),
                  pl.BlockSpec((1, tn), lambda j: (0, j))],
        out_specs=pl.BlockSpec((n, tn), lambda j: (0, j)),
        compiler_params=_params(1),
        name="ada",
    )(c_all, w_ada, b_ada.reshape(1, cols))


def _stage1_body(x_ref, sh_ref, sc_ref, g1_ref, w_ref, lng_ref, lnb_ref, ws_ref, bs_ref, *rest,
                 period, by_stride, emit_vn, cache_t_new):
    rest = list(rest)
    cache_refs = [rest.pop(0) for _ in range(1 + 3 * N_DIL)] if cache_t_new else []
    if by_stride:
        qkv_refs = [[rest.pop(0) for _ in range(N_DIL)] for _ in range(3)]
    else:
        q_ref = rest.pop(0)
    kv_refs = [rest.pop(0) for _ in range(N_DIL)]
    ob_ref, ga_ref, gb_ref = rest.pop(0), rest.pop(0), rest.pop(0)
    vn_ref = rest.pop(0) if emit_vn else None
    if cache_t_new:
        cache_refs += [rest.pop(0) for _ in range(N_DIL + 1)]
    stage_ref = rest.pop(0) if by_stride else None
    if cache_t_new:
        _sample_cache_step(pl.program_id(0), cache_refs + rest, cache_t_new)

    x = x_ref[...]
    tm = x.shape[0]
    h = x * lax.rsqrt(jnp.mean(x * x, axis=-1, keepdims=True) + EPS) * g1_ref[...]
    h = h * (1.0 + sc_ref[0]) + sh_ref[0]
    hb = h.astype(BF16)

    q = _dot(hb, w_ref[:, 0:D_QKV]) * (SCALE * LOG2E if by_stride else SCALE)
    k = _dot(hb, w_ref[:, D_QKV:2 * D_QKV])
    v = _dot(hb, w_ref[:, 2 * D_QKV:3 * D_QKV])
    if by_stride:
        per_group = D_A // LANES
        for val, refs in zip((q, k, v), qkv_refs):
            for c in range(D_QKV // LANES):
                stage_ref[c] = val[:, c * LANES:(c + 1) * LANES]
            for g, (_, dil) in enumerate(DILATED_GROUPS):
                for r in range(dil):
                    for c in range(per_group):
                        piece = stage_ref[g * per_group + c, pl.ds(r, tm // dil, stride=dil), :]
                        refs[g][0, r, :, c * LANES:(c + 1) * LANES] = piece.astype(BF16)
    else:
        q_ref[...] = q
    for g in range(N_DIL):
        rows = kv_refs[g].shape[1]
        kv_refs[g][0, :, 0:D_A] = k[tm - rows:, g * D_A:(g + 1) * D_A]
        kv_refs[g][0, :, D_A:2 * D_A] = v[tm - rows:, g * D_A:(g + 1) * D_A]

    ga_ref[...] = _sigmoid(_dot(hb, w_ref[:, C_GA:C_GB])).astype(BF16)
    gb_ref[...] = _sigmoid(_dot(hb, w_ref[:, C_GB:IN_COLS])).astype(BF16)

    u = _gelu(_dot(hb, w_ref[:, C_U:C_V]))
    vb = _gelu(_dot(hb, w_ref[:, C_V:C_GA]))
    xc = vb - jnp.mean(vb, axis=-1, keepdims=True)
    vn = xc * lax.rsqrt(jnp.mean(xc * xc, axis=-1, keepdims=True) + EPS) * lng_ref[...] + lnb_ref[...]
    if emit_vn:
        vn_ref[...] = vn
    vnb = vn.astype(BF16)

    r = ws_ref.shape[1]
    ri = lax.broadcasted_iota(jnp.int32, (r, r), 0)
    ci = lax.broadcasted_iota(jnp.int32, (r, r), 1)
    keep = ci <= ri
    if period != r:
        sh = int(math.log2(period))
        keep = keep & (jnp.right_shift(ri, sh) == jnp.right_shift(ci, sh))
    for g in range(SGU_GROUPS):
        wg = jnp.where(keep, ws_ref[g], 0.0).astype(BF16)
        cs = slice(g * SGU_DIM, (g + 1) * SGU_DIM)
        for c in range(tm // r):
            rs = slice(c * r, (c + 1) * r)
            mixed = _dot(wg, vnb[rs, cs]) + bs_ref[:, cs]
            ob_ref[rs, cs] = (u[rs, cs] * mixed).astype(BF16)


def _stage1(x2d, sh, sc, g1, w_in_bf, lng, lnb, ws, bs, *, tm, mod_index, stride_index, tail_shapes, tail_index,
            period, by_stride, emit_vn, name, cache=None):
    n = x2d.shape[0]
    grid = (n // tm,)
    r = ws.shape[1]
    const2 = lambda i: (0, 0)
    row = lambda i: (i, 0)
    in_specs = [
        pl.BlockSpec((tm, D_MODEL), row),
        _mod_spec(sh, mod_index),
        _mod_spec(sc, mod_index),
        pl.BlockSpec((1, D_MODEL), const2),
        pl.BlockSpec((D_MODEL, IN_COLS), const2, pipeline_mode=pl.Buffered(1)),
        pl.BlockSpec((1, D_B), const2),
        pl.BlockSpec((1, D_B), const2),
        pl.BlockSpec((SGU_GROUPS, r, r), lambda i: (0, 0, 0)),
        pl.BlockSpec((r, D_B), const2),
    ]
    out_shape, out_specs, scratch = [], [], []
    if by_stride:
        n_batch, s_len = stride_index[0], stride_index[1]
        for _ in range(3):
            for _, dil in DILATED_GROUPS:
                out_shape.append(jax.ShapeDtypeStruct((n_batch, dil, s_len // dil, D_A), BF16))
                out_specs.append(pl.BlockSpec((1, dil, tm // dil, D_A), stride_index[2]))
        scratch.append(pltpu.VMEM((D_QKV // LANES, tm, LANES), F32))
    else:
        out_shape.append(jax.ShapeDtypeStruct((n, D_QKV), F32))
        out_specs.append(pl.BlockSpec((tm, D_QKV), row))
    for g in range(N_DIL):
        nb, keep = tail_shapes[g]
        rows = min(tm, keep)
        out_shape.append(jax.ShapeDtypeStruct((nb, keep, 2 * D_A), F32))
        out_specs.append(pl.BlockSpec((1, rows, 2 * D_A), functools.partial(tail_index, g=g, rows=rows)))
    out_shape += [jax.ShapeDtypeStruct((n, D_B), BF16),
                  jax.ShapeDtypeStruct((n, D_MODEL), BF16),
                  jax.ShapeDtypeStruct((n, D_MODEL), BF16)]
    out_specs += [pl.BlockSpec((tm, D_B), row), pl.BlockSpec((tm, D_MODEL), row), pl.BlockSpec((tm, D_MODEL), row)]
    if emit_vn:
        out_shape.append(jax.ShapeDtypeStruct((n, D_B), F32))
        out_specs.append(pl.BlockSpec((tm, D_B), row))
    args = [x2d, sh[0], sc[0], g1, w_in_bf, lng, lnb, ws, bs]
    cache_t_new = 0
    if cache is not None:
        c_args, c_in, c_shape, c_out, c_scratch, units = _sample_cache_specs(*cache)
        assert units == grid[0], "one sample-cache unit per projection tile"
        args += c_args
        in_specs += c_in
        out_shape += c_shape
        out_specs += c_out
        scratch += c_scratch
        cache_t_new = cache[-1]
    body = functools.partial(_stage1_body, period=period, by_stride=by_stride, emit_vn=emit_vn,
                             cache_t_new=cache_t_new)
    return pl.pallas_call(
        body, out_shape=out_shape, grid=grid, in_specs=in_specs, out_specs=out_specs, scratch_shapes=scratch,
        compiler_params=_params(1), name=name,
    )(*args)


def _attn_prompt_body(q_ref, kp_ref, kc_ref, vp_ref, vc_ref, bias_ref, o_ref, lse_ref):
    n = pl.program_id(2)
    n_sub = q_ref.shape[2] // BLOCK
    ri = lax.broadcasted_iota(jnp.int32, (BLOCK, 2 * BLOCK), 0)
    ci = lax.broadcasted_iota(jnp.int32, (BLOCK, 2 * BLOCK), 1)
    delta = ri + BLOCK - ci
    band = (delta >= 0) & (delta <= STEPS)
    first_key = jnp.where(n > 0, 0, BLOCK)
    low_q = lax.broadcasted_iota(jnp.int32, (BLOCK, LANES), 1) < HEAD_DIM
    low_v = lax.broadcasted_iota(jnp.int32, (2 * BLOCK, LANES), 1) < HEAD_DIM
    ones_lo = jnp.where(low_v, 1.0, 0.0).astype(BF16)
    ones_hi = jnp.where(low_v, 0.0, 1.0).astype(BF16)
    for sub in range(n_sub):
        rows = slice(sub * BLOCK, (sub + 1) * BLOCK)
        valid = band & (ci >= first_key) if sub == 0 else band
        for pair in range(HEADS // 2):
            cs = slice(pair * LANES, (pair + 1) * LANES)
            qp = q_ref[0, 0, rows, cs]
            if sub == 0:
                kp = jnp.concatenate([kp_ref[0, 0, :, cs], kc_ref[0, 0, 0:BLOCK, cs]], axis=0)
                vp = jnp.concatenate([vp_ref[0, 0, :, cs], vc_ref[0, 0, 0:BLOCK, cs]], axis=0)
            else:
                kp = kc_ref[0, 0, (sub - 1) * BLOCK:(sub + 1) * BLOCK, cs]
                vp = vc_ref[0, 0, (sub - 1) * BLOCK:(sub + 1) * BLOCK, cs]
            acc, tops = None, []
            for half in range(2):
                own_q = low_q if half == 0 else jnp.logical_not(low_q)
                own_v = low_v if half == 0 else jnp.logical_not(low_v)
                s = _dot_nt(jnp.where(own_q, qp, jnp.zeros_like(qp)), kp) + bias_ref[2 * pair + half]
                s = jnp.where(valid, s, NEG_INF)
                m = jnp.max(s, axis=-1, keepdims=True)
                p = jnp.exp2(s - m).astype(BF16)
                w = jnp.concatenate([jnp.where(own_v, vp, jnp.zeros_like(vp)), ones_lo if half == 0 else ones_hi],
                                    axis=1)
                r = _dot(p, w)
                acc = r if acc is None else acc + r
                tops.append(m)
            den = acc[:, LANES:]
            o_ref[0, 0, rows, cs] = (acc[:, :LANES] / den).astype(BF16)
            lse_ref[0, 0, rows, cs] = jnp.where(low_q, tops[0], tops[1]) * LN2 + jnp.log(den)


def _attn_prompt_group(q4, k4, v4, bias, dil, n_sub):
    b_sz, _, m_len, _ = q4.shape
    span = n_sub * BLOCK
    cur = pl.BlockSpec((1, 1, span, D_A), lambda b, r, n: (b, r, n, 0))
    prev = pl.BlockSpec((1, 1, BLOCK, D_A), lambda b, r, n: (b, r, jnp.maximum(n * n_sub - 1, 0), 0))
    return pl.pallas_call(
        _attn_prompt_body,
        out_shape=[jax.ShapeDtypeStruct((b_sz, dil, m_len, D_A), BF16),
                   jax.ShapeDtypeStruct((b_sz, dil, m_len, D_A), F32)],
        grid=(b_sz, dil, m_len // span),
        in_specs=[cur, prev, cur, prev, cur,
                  pl.BlockSpec((HEADS, BLOCK, 2 * BLOCK), lambda b, r, n: (0, 0, 0))],
        out_specs=[cur, cur],
        compiler_params=_params(3),
        name=f"attn_prompt_d{dil}",
    )(q4, k4, k4, v4, v4, bias)


def _sample_cache_step(unit, refs, t_new):
    refs = list(refs)
    q_ref = refs.pop(0)
    new_refs = [refs.pop(0) for _ in range(N_DIL)]
    cache_refs = [refs.pop(0) for _ in range(N_DIL)]
    bias_refs = [refs.pop(0) for _ in range(N_DIL)]
    out_refs = [refs.pop(0) for _ in range(N_DIL)]
    o_ref = refs.pop(0)
    p_refs = [refs.pop(0) for _ in range(N_DIL)]
    scale_ref = refs.pop(0)
    kv = lax.rem(unit, 2)
    rows = HEADS * t_new

    def allowed(g, col0, cols):
        win, dil = DILATED_GROUPS[g]
        tok = jnp.bitwise_and(lax.broadcasted_iota(jnp.int32, (rows, cols), 0), t_new - 1)
        col = lax.broadcasted_iota(jnp.int32, (rows, cols), 1) + col0
        back = win + tok - col
        return ((col < win + t_new) & (back >= 0) & (jnp.bitwise_and(back, dil - 1) == 0)
                & (back <= STEPS * dil))

    lane = lax.broadcasted_iota(jnp.int32, (D_A, LANES), 1)
    front = jnp.zeros((LANES - t_new, D_A), F32)
    for g in range(N_DIL):
        win = cache_refs[g].shape[-1]
        new = new_refs[g][0]
        half = jnp.where(kv == 0, new[:, 0:D_A], new[:, D_A:2 * D_A])
        tail = jnp.transpose(jnp.concatenate([front, half], axis=0))
        rolled = pltpu.roll(cache_refs[g][0, 0], win - t_new, axis=1)
        if win > LANES:
            out_refs[g][0, 0, :, 0:win - LANES] = rolled[:, 0:win - LANES]
        out_refs[g][0, 0, :, win - LANES:win] = jnp.where(lane >= LANES - t_new, tail, rolled[:, win - LANES:win])

    ri = lax.broadcasted_iota(jnp.int32, (rows, D_A), 0)
    ci = lax.broadcasted_iota(jnp.int32, (rows, D_A), 1)
    own_head = jnp.right_shift(ri, int(math.log2(t_new))) == jnp.right_shift(ci, int(math.log2(HEAD_DIM)))
    pad = jnp.zeros((LANES - t_new, D_A), F32)
    glane = lax.broadcasted_iota(jnp.int32, (rows, LANES), 1)

    @pl.when(kv == 0)
    def _():
        q = q_ref[...]
        stats = []
        for g in range(N_DIL):
            win = cache_refs[g].shape[-1]
            qg = q[:, g * D_A:(g + 1) * D_A]
            qblk = jnp.where(own_head, jnp.concatenate([qg] * HEADS, axis=0), 0.0).astype(BF16)
            s_c = _dot(qblk, cache_refs[g][0, 0].astype(BF16)) + bias_refs[g][:, 0:win]
            s_c = jnp.where(allowed(g, 0, win), s_c, NEG_INF)
            k_new = jnp.concatenate([new_refs[g][0][:, 0:D_A], pad], axis=0).astype(BF16)
            s_n = _dot_nt(qblk, k_new) + bias_refs[g][:, win:]
            s_n = jnp.where(allowed(g, win, LANES), s_n, NEG_INF)
            m = jnp.maximum(jnp.max(s_c, axis=-1, keepdims=True), jnp.max(s_n, axis=-1, keepdims=True))
            p_c = jnp.exp(s_c - m)
            p_n = jnp.exp(s_n - m)
            l = jnp.sum(p_c, axis=-1, keepdims=True) + jnp.sum(p_n, axis=-1, keepdims=True)
            p_refs[g][:, 0:win] = p_c.astype(BF16)
            p_refs[g][:, win:] = p_n.astype(BF16)
            stats.append((l, m + jnp.log(l)))
        top = jnp.maximum(jnp.maximum(stats[0][1], stats[1][1]), stats[2][1])
        es = [jnp.exp(lse - top) for _, lse in stats]
        den = es[0] + es[1] + es[2]
        scale = jnp.zeros((rows, LANES), F32)
        for g in range(N_DIL):
            scale = jnp.where(glane == g, es[g] / den / stats[g][0], scale)
        scale_ref[...] = scale

    @pl.when(kv == 1)
    def _():
        acc = jnp.zeros((rows, D_A), F32)
        for g in range(N_DIL):
            win = cache_refs[g].shape[-1]
            part = _dot_nt(p_refs[g][:, 0:win], cache_refs[g][0, 0].astype(BF16))
            v_new = jnp.concatenate([new_refs[g][0][:, D_A:2 * D_A], pad], axis=0).astype(BF16)
            part = part + _dot(p_refs[g][:, win:], v_new)
            acc = acc + scale_ref[:, g:g + 1] * part
        acc = jnp.where(own_head, acc, 0.0)
        out = acc[0:t_new]
        for h in range(1, HEADS):
            out = out + acc[h * t_new:(h + 1) * t_new]
        o_ref[...] = out


def _sample_cache_specs(q_s, kv_new, caches_t, biases, t_new):
    n = q_s.shape[0]
    nb = n // t_new
    blk4 = lambda i: (i // 2, lax.rem(i, 2), 0, 0)
    in_specs = [pl.BlockSpec((t_new, D_QKV), lambda i: (i // 2, 0))]
    in_specs += [pl.BlockSpec((1, t_new, 2 * D_A), lambda i: (i // 2, 0, 0))] * N_DIL
    in_specs += [pl.BlockSpec((1, 1, D_A, c.shape[-1]), blk4) for c in caches_t]
    in_specs += [pl.BlockSpec(a.shape, lambda i: (0, 0), pipeline_mode=pl.Buffered(1)) for a in biases]
    out_shape = [jax.ShapeDtypeStruct(c.shape, c.dtype) for c in caches_t]
    out_specs = [pl.BlockSpec((1, 1, D_A, c.shape[-1]), blk4) for c in caches_t]
    out_shape.append(jax.ShapeDtypeStruct((n, D_A), F32))
    out_specs.append(pl.BlockSpec((t_new, D_A), lambda i: (i // 2, 0)))
    rows = HEADS * t_new
    scratch = [pltpu.VMEM((rows, c.shape[-1] + LANES), BF16) for c in caches_t]
    scratch.append(pltpu.VMEM((rows, LANES), F32))
    args = [q_s, *[a.reshape(nb, t_new, 2 * D_A) for a in kv_new], *caches_t, *biases]
    return args, in_specs, out_shape, out_specs, scratch, 2 * nb


def _stage2_body(*refs, merged):
    refs = list(refs)
    if merged:
        oa_ref = refs.pop(0)
    else:
        o_refs = [refs.pop(0) for _ in range(N_DIL)]
        l_refs = [refs.pop(0) for _ in range(N_DIL)]
    (ob_ref, ga_ref, gb_ref, x_ref, gt1_ref, sh2_ref, sc2_ref, wua_ref, wub_ref, wo_ref, g2_ref,
     wr_ref, br_ref, x1_ref, h2_ref, route_ref, code_ref, cnt_ref, run_ref) = refs[:19]
    step = pl.program_id(0)
    tm = x_ref.shape[0]

    @pl.when(step == 0)
    def _():
        run_ref[...] = jnp.zeros_like(run_ref)

    if merged:
        oa = oa_ref[...].astype(BF16)
    else:
        o_stage, l_stage = refs[19], refs[20]

        def by_position(ref, stage, dil):
            if dil == 1:
                return ref[0, 0].astype(F32)
            slabs = ref.shape[-1] // LANES
            for r in range(dil):
                rows = ref[0, r].astype(F32)
                for c in range(slabs):
                    stage[c, pl.ds(r, tm // dil, stride=dil), :] = rows[:, c * LANES:(c + 1) * LANES]
            return jnp.concatenate([stage[c] for c in range(slabs)], axis=1)

        ls = [by_position(l_refs[g], l_stage, dil) for g, (_, dil) in enumerate(DILATED_GROUPS)]
        top = jnp.maximum(jnp.maximum(ls[0], ls[1]), ls[2])
        es = [jnp.exp(l - top) for l in ls]
        den = es[0] + es[1] + es[2]
        oa = None
        for g, (_, dil) in enumerate(DILATED_GROUPS):
            term = es[g] * by_position(o_refs[g], o_stage, dil)
            oa = term if oa is None else oa + term
        oa = (oa / den).astype(BF16)

    ya = _dot(oa, wua_ref[...])
    yb = _dot(ob_ref[...], wub_ref[...])
    z = (ga_ref[...].astype(F32) * ya + gb_ref[...].astype(F32) * yb).astype(BF16)
    x1 = x_ref[...] + gt1_ref[0] * _dot(z, wo_ref[...])
    x1_ref[...] = x1
    h2 = x1 * lax.rsqrt(jnp.mean(x1 * x1, axis=-1, keepdims=True) + EPS) * g2_ref[...]
    h2 = h2 * (1.0 + sc2_ref[0]) + sh2_ref[0]
    h2_ref[...] = h2

    h_hi = h2.astype(BF16)
    h_lo = (h2 - h_hi.astype(F32)).astype(BF16)
    both = _dot(h_hi, wr_ref[...])
    logit = (both[:, 0:LANES] + both[:, LANES:] + _dot(h_lo, wr_ref[:, 0:LANES])) + br_ref[...]
    lane_i = lax.broadcasted_iota(jnp.int32, (tm, LANES), 1)
    lane = lane_i.astype(F32)
    big = float(LANES)
    is_g = (lane_i >= N_EXPERTS) & (lane_i < N_EXPERTS + MOE_GROUPS)
    gl = jnp.where(is_g, logit, -jnp.inf)
    gmax = jnp.max(gl, axis=-1, keepdims=True)
    g_idx = jnp.min(jnp.where(gl == gmax, lane, big), axis=-1, keepdims=True) - N_EXPERTS
    g_w = 1.0 / jnp.sum(jnp.where(is_g, jnp.exp(logit - gmax), 0.0), axis=-1, keepdims=True)
    lo = g_idx * EXPERTS_PER_GROUP
    el = jnp.where((lane >= lo) & (lane < lo + EXPERTS_PER_GROUP), logit, -jnp.inf)
    t1 = jnp.max(el, axis=-1, keepdims=True)
    i1 = jnp.min(jnp.where(el == t1, lane, big), axis=-1, keepdims=True)
    el2 = jnp.where(lane == i1, -jnp.inf, el)
    t2 = jnp.max(el2, axis=-1, keepdims=True)
    i2 = jnp.min(jnp.where(el2 == t2, lane, big), axis=-1, keepdims=True)
    d = jnp.exp(t2 - t1)
    cw1 = g_w * (1.0 / (1.0 + d))
    cw2 = g_w * (d / (1.0 + d))

    oh1 = lane == i1
    oh2 = lane == i2
    picked = jnp.where(oh1, 1, jnp.where(oh2, 2, 0))
    hit = jnp.minimum(picked, 1)
    row_i = lax.broadcasted_iota(jnp.int32, (tm, LANES), 0)
    seen = hit
    shift = 1
    while shift < tm:
        seen = seen + jnp.where(row_i >= shift, pltpu.roll(seen, shift, axis=0), 0)
        shift *= 2
    run = run_ref[0:1, :]
    code_ref[...] = jnp.where(hit > 0, picked + 4 * (seen - hit + run), 0)
    run_new = jnp.broadcast_to(run + seen[tm - 1:tm, :], run_ref.shape)
    run_ref[...] = run_new
    cnt_ref[...] = run_new

    route_ref[...] = jnp.where(lane_i == 0, cw1, jnp.where(lane_i == 1, cw2, 0.0))


def _stage2(attn_in, ob, ga, gb, x2d, gt1, sh2, sc2, wua, wub, wo, g2, wr_pair, br, *, tm, mod_index, stride_index,
            merged, name):
    n = x2d.shape[0]
    row = lambda i: (i, 0)
    const2 = lambda i: (0, 0)
    scratch = [pltpu.VMEM((8, LANES), jnp.int32)]
    if merged:
        attn_specs = [pl.BlockSpec((tm, D_A), row)]
    else:
        attn_specs = [pl.BlockSpec((1, dil, tm // dil, D_A), stride_index) for _, dil in DILATED_GROUPS]
        attn_specs += [pl.BlockSpec((1, dil, tm // dil, D_A), stride_index) for _, dil in DILATED_GROUPS]
        scratch += [pltpu.VMEM((D_A // LANES, tm, LANES), F32), pltpu.VMEM((D_A // LANES, tm, LANES), F32)]
    in_specs = attn_specs + [
        pl.BlockSpec((tm, D_B), row), pl.BlockSpec((tm, D_MODEL), row), pl.BlockSpec((tm, D_MODEL), row),
        pl.BlockSpec((tm, D_MODEL), row),
        _mod_spec(gt1, mod_index), _mod_spec(sh2, mod_index), _mod_spec(sc2, mod_index),
        pl.BlockSpec((D_A, D_MODEL), const2), pl.BlockSpec((D_B, D_MODEL), const2),
        pl.BlockSpec((D_MODEL, D_MODEL), const2), pl.BlockSpec((1, D_MODEL), const2),
        pl.BlockSpec((D_MODEL, 2 * LANES), const2),
        pl.BlockSpec((1, LANES), const2),
    ]
    out_shape = [jax.ShapeDtypeStruct((n, D_MODEL), F32), jax.ShapeDtypeStruct((n, D_MODEL), F32),
                 jax.ShapeDtypeStruct((n, LANES), F32), jax.ShapeDtypeStruct((n, LANES), jnp.int32),
                 jax.ShapeDtypeStruct((8, LANES), jnp.int32)]
    out_specs = [pl.BlockSpec((tm, D_MODEL), row), pl.BlockSpec((tm, D_MODEL), row),
                 pl.BlockSpec((tm, LANES), row), pl.BlockSpec((tm, LANES), row), pl.BlockSpec((8, LANES), const2)]
    return pl.pallas_call(
        functools.partial(_stage2_body, merged=merged),
        out_shape=out_shape, grid=(n // tm,), in_specs=in_specs, out_specs=out_specs,
        scratch_shapes=scratch,
        compiler_params=_params(1), name=name,
    )(*attn_in, ob, ga, gb, x2d, gt1[0], sh2[0], sc2[0], wua, wub, wo, g2, wr_pair, br)


def _dispatch_body(cnt_ref, off_ref, pad_ref, nv_ref, posp_ref, poss_ref, hp_ref, hs_ref, xs_ref,
                   zero_ref, buf_ref, load_sem, scat_sem, zero_sem, *, tiles_p, n_tiles):
    step = pl.program_id(0)
    tile = zero_ref.shape[0]
    groups = buf_ref.shape[1]
    tm = groups * ROW_TILE
    groups_s = hs_ref.shape[0]
    n_s = groups_s * ROW_TILE
    slot = lax.rem(step, 3)
    ahead = lax.rem(step + 1, 3)

    def load_prompt(j, s):
        return pltpu.make_async_copy(hp_ref.at[pl.ds(j * groups, groups)], buf_ref.at[s], load_sem.at[s])

    def load_sample(s):
        return pltpu.make_async_copy(hs_ref, buf_ref.at[s, pl.ds(0, groups_s)], load_sem.at[s])

    def row_copy(s, group, j, dst_row):
        return pltpu.make_async_copy(buf_ref.at[s, group, pl.ds(j, 1)], xs_ref.at[pl.ds(dst_row, 1)], scat_sem.at[s])

    def scatter(rows, pos_ref, s):
        def issue(group, c):
            for j in range(ROW_TILE):
                t = group * ROW_TILE + j
                row_copy(s, group, j, pos_ref[0, 0, 2 * t]).start()
                row_copy(s, group, j, pos_ref[0, 0, 2 * t + 1]).start(priority=1)
            return c

        lax.fori_loop(0, rows // ROW_TILE, issue, 0)

    def drain(rows, s):
        def one(t, c):
            row_copy(s, 0, 0, 0).wait()
            row_copy(s, 0, 0, 0).wait()
            return c

        lax.fori_loop(0, rows, one, 0, unroll=8)

    @pl.when(step == 0)
    def _():
        zero_ref[...] = jnp.zeros_like(zero_ref)
        load_prompt(0, 0).start()
        bits = int(math.log2(tile))

        def pad_copies(e, wait):
            n = pad_ref[e] - cnt_ref[e]
            base = off_ref[e] + cnt_ref[e]
            end = off_ref[e] + pad_ref[e]

            def one_row(i, c):
                copy = _row_copy(zero_ref, 0, xs_ref, base + i, zero_sem)
                if wait:
                    copy.wait()
                else:
                    copy.start()
                return c

            lax.fori_loop(0, jnp.bitwise_and(n, 7), one_row, 0)
            for b in range(3, bits):
                size = 1 << b
                first = pl.multiple_of(end - jnp.bitwise_and(n, ~(2 * size - 1)) - size, 8)
                copy = pltpu.make_async_copy(zero_ref.at[pl.ds(0, size)], xs_ref.at[pl.ds(first, size)], zero_sem)

                @pl.when(jnp.bitwise_and(n, size) != 0)
                def _():
                    if wait:
                        copy.wait()
                    else:
                        copy.start()

        def start_pads(e, c):
            pad_copies(e, False)
            return c

        def wait_pads(e, c):
            pad_copies(e, True)
            return c

        def tile_copy(j):
            return pltpu.make_async_copy(zero_ref, xs_ref.at[pl.ds(j * tile, tile)], zero_sem)

        def start_tile(j, c):
            tile_copy(j).start()
            return c

        def wait_tile(j, c):
            tile_copy(j).wait()
            return c

        lax.fori_loop(0, N_EXPERTS, start_pads, 0)
        lax.fori_loop(nv_ref[0], n_tiles, start_tile, 0)
        lax.fori_loop(0, N_EXPERTS, wait_pads, 0)
        lax.fori_loop(nv_ref[0], n_tiles, wait_tile, 0)

    @pl.when(step >= 2)
    def _():
        drain(tm, ahead)

    @pl.when(step + 1 < tiles_p)
    def _():
        load_prompt(step + 1, ahead).start()

    @pl.when(step + 1 == tiles_p)
    def _():
        load_sample(ahead).start()

    @pl.when(step < tiles_p)
    def _():
        load_prompt(step, slot).wait()
        scatter(tm, posp_ref, slot)

    @pl.when(step == tiles_p)
    def _():
        load_sample(slot).wait()
        scatter(n_s, poss_ref, slot)
        drain(tm, lax.rem(step + 2, 3))
        drain(n_s, slot)


def _dispatch(count, offset, padded, n_valid, pos_p, pos_s, h2_p, h2_s, *, cap, tm, tile):
    n_p, n_s = h2_p.shape[0], h2_s.shape[0]
    tiles_p = n_p // tm
    assert tiles_p >= 2 and n_s <= tm
    last_p = lambda i, *_: (jnp.minimum(i, tiles_p - 1), 0, 0)
    in_specs = [pl.BlockSpec((1, 1, 2 * tm), last_p, memory_space=pltpu.SMEM),
                pl.BlockSpec((1, 1, 2 * n_s), lambda i, *_: (0, 0, 0), memory_space=pltpu.SMEM),
                pl.BlockSpec(memory_space=pl.ANY),
                pl.BlockSpec(memory_space=pl.ANY)]
    grid_spec = pltpu.PrefetchScalarGridSpec(
        num_scalar_prefetch=4, grid=(tiles_p + 1,), in_specs=in_specs,
        out_specs=pl.BlockSpec(memory_space=pl.ANY),
        scratch_shapes=[pltpu.VMEM((tile, D_MODEL), F32), pltpu.VMEM((3, tm // ROW_TILE, ROW_TILE, D_MODEL), F32),
                        pltpu.SemaphoreType.DMA((3,)), pltpu.SemaphoreType.DMA((3,)), pltpu.SemaphoreType.DMA(())],
    )
    by_tile = lambda a: a.reshape(a.shape[0] // ROW_TILE, ROW_TILE, D_MODEL)
    return pl.pallas_call(
        functools.partial(_dispatch_body, tiles_p=tiles_p, n_tiles=cap // tile),
        out_shape=jax.ShapeDtypeStruct((cap, D_MODEL), F32), grid_spec=grid_spec,
        compiler_params=pltpu.CompilerParams(dimension_semantics=("arbitrary",), vmem_limit_bytes=VMEM_LIMIT,
                                             has_side_effects=True),
        name="dispatch",
    )(count, offset, padded, n_valid, pos_p.reshape(tiles_p, 1, 2 * tm), pos_s.reshape(1, 1, 2 * n_s),
      by_tile(h2_p), by_tile(h2_s))


def _moe_body(te_ref, nv_ref, run_ref, nxt_ref, x_ref, wg_hbm, wu_hbm, wd_hbm, y_ref,
              wg_f, wu_f, wd_f, wg_s, wu_s, wd_s, sem):
    i = pl.program_id(0)
    active = i < nv_ref[0]
    fresh = (i == 0) | (te_ref[i] != te_ref[jnp.maximum(i - 1, 0)])
    slot = lax.rem(run_ref[i], 2)

    def fetch(e, s):
        return (pltpu.make_async_copy(wg_hbm.at[e], wg_f.at[s], sem.at[s]),
                pltpu.make_async_copy(wu_hbm.at[e], wu_f.at[s], sem.at[s]),
                pltpu.make_async_copy(wd_hbm.at[e], wd_f.at[s], sem.at[s]))

    @pl.when(active & (i == 0))
    def _():
        for copy in fetch(te_ref[0], 0):
            copy.start()

    @pl.when(active & fresh)
    def _():
        for copy in fetch(te_ref[i], slot):
            copy.wait()

        @pl.when(nxt_ref[i] >= 0)
        def _():
            for copy in fetch(nxt_ref[i], 1 - slot):
                copy.start()

        wg_s[...] = wg_f[slot].astype(BF16)
        wu_s[...] = wu_f[slot].astype(BF16)
        wd_s[...] = wd_f[slot].astype(BF16)

    @pl.when(active)
    def _():
        x = x_ref[...].astype(BF16)
        a = _dot(x, wg_s[...])
        b = _dot(x, wu_s[...])
        mid = (a * _sigmoid(a) * b).astype(BF16)
        y_ref[...] = _dot(mid, wd_s[...])

    @pl.when(jnp.logical_not(active))
    def _():
        y_ref[...] = jnp.zeros_like(y_ref)


def _moe(tile_expert, n_valid, x_sorted, w_eg, w_eu, w_ed, tm):
    cap = x_sorted.shape[0]
    n_tiles = cap // tm
    tile = jnp.arange(n_tiles, dtype=jnp.int32)
    live = tile < n_valid[0]
    starts = jnp.concatenate([jnp.zeros((1,), jnp.bool_), tile_expert[1:] != tile_expert[:-1]]) & live
    run = jnp.cumsum(starts.astype(jnp.int32))
    later = (tile_expert[None, :] > tile_expert[:, None]) & live[None, :]
    nxt = jnp.min(jnp.where(later, tile_expert[None, :], N_EXPERTS), axis=1)
    nxt = jnp.where(nxt < N_EXPERTS, nxt, -1).astype(jnp.int32)

    xmap = lambda i, te, nv, rn, nx: (jnp.maximum(jnp.minimum(i, nv[0] - 1), 0), 0)
    grid_spec = pltpu.PrefetchScalarGridSpec(
        num_scalar_prefetch=4,
        grid=(n_tiles,),
        in_specs=[pl.BlockSpec((tm, D_MODEL), xmap),
                  pl.BlockSpec(memory_space=pl.ANY), pl.BlockSpec(memory_space=pl.ANY),
                  pl.BlockSpec(memory_space=pl.ANY)],
        out_specs=pl.BlockSpec((tm, D_MODEL), lambda i, te, nv, rn, nx: (i, 0)),
        scratch_shapes=[pltpu.VMEM((2, D_MODEL, D_EXPERT), F32), pltpu.VMEM((2, D_MODEL, D_EXPERT), F32),
                        pltpu.VMEM((2, D_EXPERT, D_MODEL), F32),
                        pltpu.VMEM((D_MODEL, D_EXPERT), BF16), pltpu.VMEM((D_MODEL, D_EXPERT), BF16),
                        pltpu.VMEM((D_EXPERT, D_MODEL), BF16), pltpu.SemaphoreType.DMA((2,))],
    )
    return pl.pallas_call(
        _moe_body, out_shape=jax.ShapeDtypeStruct((cap, D_MODEL), F32), grid_spec=grid_spec,
        compiler_params=_params(1), name="moe",
    )(tile_expert, n_valid, run, nxt, x_sorted, w_eg, w_eu, w_ed)


def _final_body(pos_ref, next_ref, x1_ref, route_ref, gt2_ref, gf_ref, ys_ref, o_ref, y_buf, sem):
    step = pl.program_id(0)
    tm = x1_ref.shape[0]
    slot = lax.rem(step, 2)

    def row_copy(s, k, group, j, src_row):
        return pltpu.make_async_copy(ys_ref.at[pl.ds(src_row, 1)], y_buf.at[s, k, group, pl.ds(j, 1)], sem.at[s])

    def gather(p_ref, s):
        def issue(group, c):
            for j in range(ROW_TILE):
                t = group * ROW_TILE + j
                row_copy(s, 0, group, j, p_ref[0, 0, 2 * t]).start(priority=1)
                row_copy(s, 1, group, j, p_ref[0, 0, 2 * t + 1]).start(priority=1)
            return c

        lax.fori_loop(0, tm // ROW_TILE, issue, 0)

    @pl.when(step == 0)
    def _():
        gather(pos_ref, 0)

    @pl.when(step + 1 < pl.num_programs(0))
    def _():
        gather(next_ref, 1 - slot)

    def drain(t, c):
        row_copy(slot, 0, 0, 0, 0).wait()
        row_copy(slot, 1, 0, 0, 0).wait()
        return c

    lax.fori_loop(0, tm, drain, 0, unroll=8)

    route = route_ref[...]
    moe = (route[:, 0:1] * y_buf[slot, 0].reshape(tm, D_MODEL)
           + route[:, 1:2] * y_buf[slot, 1].reshape(tm, D_MODEL))
    x = x1_ref[...] + gt2_ref[0] * moe
    o_ref[...] = x * lax.rsqrt(jnp.mean(x * x, axis=-1, keepdims=True) + EPS) * gf_ref[...]


def _final(pos, x1, route, gt2, gf, y_sorted, *, tm, mod_index, name):
    n = x1.shape[0]
    steps = n // tm
    row = lambda i: (i, 0)
    pos3 = pos.reshape(steps, 1, 2 * tm)
    return pl.pallas_call(
        _final_body,
        out_shape=jax.ShapeDtypeStruct((n, D_MODEL), F32),
        grid=(steps,),
        in_specs=[pl.BlockSpec((1, 1, 2 * tm), lambda i: (i, 0, 0), memory_space=pltpu.SMEM),
                  pl.BlockSpec((1, 1, 2 * tm), lambda i: (jnp.minimum(i + 1, steps - 1), 0, 0),
                               memory_space=pltpu.SMEM),
                  pl.BlockSpec((tm, D_MODEL), row), pl.BlockSpec((tm, LANES), row),
                  _mod_spec(gt2, mod_index), pl.BlockSpec((1, D_MODEL), lambda i: (0, 0)),
                  pl.BlockSpec(memory_space=pl.ANY)],
        out_specs=pl.BlockSpec((tm, D_MODEL), row),
        scratch_shapes=[pltpu.VMEM((2, 2, tm // ROW_TILE, ROW_TILE, D_MODEL), F32), pltpu.SemaphoreType.DMA((2,))],
        compiler_params=_params(1), name=name,
    )(pos3, pos3, x1, route, gt2[0], gf, y_sorted)


def _t5_bucket(dist):
    max_exact = N_BUCKETS // 2
    dist = np.asarray(dist)
    d = np.maximum(dist, 1).astype(np.float32)
    large = max_exact + (np.log(d / np.float32(max_exact)) / np.float32(math.log(MAX_DISTANCE / max_exact))
                         * np.float32(N_BUCKETS - max_exact)).astype(np.int32)
    return np.where(dist < max_exact, dist, np.minimum(large, N_BUCKETS - 1)).astype(np.int32)


def _bias_lookup(tab, dist):
    onehot = (jnp.asarray(_t5_bucket(dist))[..., None] == jnp.arange(N_BUCKETS)).astype(F32)
    return jnp.einsum("...b,bh->...h", onehot, tab, precision=HIGHEST)


def _prompt_bias(tab, dil):
    delta = np.arange(BLOCK)[:, None] + BLOCK - np.arange(2 * BLOCK)[None, :]
    return jnp.transpose(_bias_lookup(tab, np.maximum(delta, 0) * dil), (2, 0, 1))


def _sample_bias(tab, win, dil, t_new):
    cols = win + LANES
    back = win + np.arange(t_new)[:, None] - np.arange(cols)[None, :]
    bias = _bias_lookup(tab, np.maximum(back, 0))
    return jnp.transpose(bias, (2, 0, 1)).reshape(HEADS * t_new, cols)


def kernel(x_prompt, x_sample, c_prompt, c_sample, cache_kv_w128, cache_kv_w512, cache_kv_w2048, rel_bias, w_ada,
           b_ada, g_norm1, w_in, ln_v_g, ln_v_b, w_spatial, b_spatial, w_up_a, w_up_b, w_out, g_norm2,
           w_route_group, b_route_group, w_route_expert, b_route_expert, w_e_gate, w_e_up, w_e_down, g_final):
    assert w_ada.shape[0] == 1, "single layer"
    bp, s_len, _ = x_prompt.shape
    bs, t_new, _ = x_sample.shape
    n_p, n_s = bp * s_len, bs * t_new
    caches = (cache_kv_w128[0], cache_kv_w512[0], cache_kv_w2048[0])
    assert all(c.shape[1] == win for c, (win, _) in zip(caches, DILATED_GROUPS)), "cache holds one full window"

    mod = _ada(jnp.concatenate([c_prompt, c_sample], axis=0), w_ada[0], b_ada[0])
    mod_p = [(mod[:bp].reshape(bp, 1, 6 * D_MODEL), j) for j in range(6)]
    mod_s = [(jnp.repeat(mod[bp:], t_new, axis=0).reshape(1, n_s, 6 * D_MODEL), j) for j in range(6)]

    w_in_bf = w_in[0].astype(BF16)
    g1 = g_norm1[0].reshape(1, D_MODEL)
    lng, lnb = ln_v_g[0].reshape(1, D_B), ln_v_b[0].reshape(1, D_B)
    ws_p = w_spatial[0]
    bs_p = jnp.repeat(jnp.transpose(b_spatial[0]), SGU_DIM, axis=1)
    reps = n_s // t_new
    corner = w_spatial[0][:, :t_new, :t_new]
    ws_s = (jnp.eye(reps, dtype=F32)[None, :, None, :, None] * corner[:, None, :, None, :]).reshape(
        SGU_GROUPS, n_s, n_s)
    bs_s = jnp.broadcast_to(bs_p[None, :t_new], (reps, t_new, D_B)).reshape(n_s, D_B)

    tpb = s_len // TM_STAGE1
    tail_p = [(bp, min(win, s_len)) for win, _ in DILATED_GROUPS]

    def tail_index_p(i, g, rows):
        first = tpb - tail_p[g][1] // rows
        return (i // tpb, jnp.maximum(i % tpb - first, 0), 0)

    q_s, kv0_s, kv1_s, kv2_s, ob_s, ga_s, gb_s, vn_s = _stage1(
        x_sample.reshape(n_s, D_MODEL), mod_s[0], mod_s[1], g1, w_in_bf, lng, lnb, ws_s, bs_s,
        tm=n_s, mod_index=lambda i: (0, 0, 0), stride_index=None, tail_shapes=[(1, n_s)] * N_DIL,
        tail_index=lambda i, g, rows: (0, 0, 0),
        period=t_new, by_stride=False, emit_vn=True, name="stage1_sample")

    tabs = [rel_bias[:, g * HEADS:(g + 1) * HEADS].astype(F32) for g in range(N_DIL)]
    sample_bias = [_sample_bias(tabs[g], win, dil, t_new) for g, (win, dil) in enumerate(DILATED_GROUPS)]
    caches_t = [jnp.transpose(c, (0, 2, 3, 4, 1)).reshape(bs, 2, D_A, c.shape[1]) for c in caches]
    outs = _stage1(
        x_prompt.reshape(n_p, D_MODEL), mod_p[0], mod_p[1], g1, w_in_bf, lng, lnb, ws_p, bs_p,
        tm=TM_STAGE1, mod_index=lambda i: (i // tpb, 0, 0),
        stride_index=(bp, s_len, lambda i: (i // tpb, 0, i % tpb, 0)), tail_shapes=tail_p, tail_index=tail_index_p,
        period=CHUNK, by_stride=True, emit_vn=False, name="stage1_prompt",
        cache=(q_s, (kv0_s, kv1_s, kv2_s), caches_t, sample_bias, t_new))
    q_g, k_g, v_g = outs[0:3], outs[3:6], outs[6:9]
    kv0_p, kv1_p, kv2_p, ob_p, ga_p, gb_p, cache0_t, cache1_t, cache2_t, oa_s = outs[9:]

    o_groups, lse_groups = [], []
    for g, (win, dil) in enumerate(DILATED_GROUPS):
        o, lse = _attn_prompt_group(q_g[g], k_g[g], v_g[g], _prompt_bias(tabs[g], dil) * LOG2E, dil,
                                    min(ATTN_BLOCKS_PER_STEP, s_len // dil // BLOCK))
        o_groups.append(o)
        lse_groups.append(lse)

    wua, wub, wo = w_up_a[0].astype(BF16), w_up_b[0].astype(BF16), w_out[0].astype(BF16)
    g2 = g_norm2[0].reshape(1, D_MODEL)
    wr = jnp.concatenate([w_route_expert[0].reshape(D_MODEL, N_EXPERTS), w_route_group[0]], axis=1)
    wr = jnp.pad(wr, ((0, 0), (0, LANES - wr.shape[1])))
    br = jnp.concatenate([b_route_expert[0].reshape(N_EXPERTS), b_route_group[0]])
    br = jnp.pad(br, (0, LANES - br.shape[0])).reshape(1, LANES)
    wr_hi = wr.astype(BF16)
    wr_pair = jnp.concatenate([wr_hi, (wr - wr_hi.astype(F32)).astype(BF16)], axis=1)

    tpb2 = s_len // TM_STAGE2
    x1_p, h2_p, route_p, code_p, cnt_p = _stage2(
        o_groups + lse_groups, ob_p, ga_p, gb_p, x_prompt.reshape(n_p, D_MODEL),
        mod_p[2], mod_p[3], mod_p[4], wua, wub, wo, g2, wr_pair, br,
        tm=TM_STAGE2, mod_index=lambda i: (i // tpb2, 0, 0), stride_index=lambda i: (i // tpb2, 0, i % tpb2, 0),
        merged=False, name="stage2_prompt")
    x1_s, h2_s, route_s, code_s, cnt_s = _stage2(
        [oa_s], ob_s, ga_s, gb_s, x_sample.reshape(n_s, D_MODEL),
        mod_s[2], mod_s[3], mod_s[4], wua, wub, wo, g2, wr_pair, br,
        tm=n_s, mod_index=lambda i: (0, 0, 0), stride_index=None, merged=True, name="stage2_sample")

    cap = 2 * (n_p + n_s) + N_EXPERTS * TM_MOE
    n_tiles = cap // TM_MOE
    count_p = cnt_p[0, :N_EXPERTS]
    count = count_p + cnt_s[0, :N_EXPERTS]
    padded = ((count + TM_MOE - 1) // TM_MOE) * TM_MOE
    ends = jnp.cumsum(padded)
    offset = ends - padded
    n_valid = (ends[-1] // TM_MOE).astype(jnp.int32).reshape(1)
    tile_start = jnp.arange(n_tiles, dtype=jnp.int32) * TM_MOE
    tile_expert = jnp.minimum(jnp.sum(ends[None, :] <= tile_start[:, None], axis=1), N_EXPERTS - 1).astype(jnp.int32)

    def sorted_rows(code, earlier):
        row = jnp.pad(offset + earlier, (0, LANES - N_EXPERTS))[None, :] + jnp.right_shift(code, 2)
        return jnp.stack([jnp.sum(jnp.where((code & 3) == k, row, 0), axis=1) for k in (1, 2)], axis=1)

    pos_p = sorted_rows(code_p, jnp.zeros_like(count_p))
    pos_s = sorted_rows(code_s, count_p)
    x_sorted = _dispatch(count, offset, padded, n_valid, pos_p, pos_s, h2_p, h2_s, cap=cap, tm=TM_FINAL, tile=TM_MOE)

    y_sorted = _moe(tile_expert, n_valid, x_sorted, w_e_gate[0], w_e_up[0], w_e_down[0], TM_MOE)

    gf = g_final.reshape(1, D_MODEL)
    tpb3 = s_len // TM_FINAL
    y_p = _final(pos_p, x1_p, route_p, mod_p[5], gf, y_sorted,
                 tm=TM_FINAL, mod_index=lambda i: (i // tpb3, 0, 0), name="final_prompt")
    y_s = _final(pos_s, x1_s, route_s, mod_s[5], gf, y_sorted,
                 tm=n_s, mod_index=lambda i: (0, 0, 0), name="final_sample")

    kv_p = [a.reshape(1, bp, a.shape[1], 2, HEADS, HEAD_DIM) for a in (kv0_p, kv1_p, kv2_p)]
    kv_s = [jnp.transpose(c.reshape(bs, 2, HEADS, HEAD_DIM, c.shape[-1]), (0, 4, 1, 2, 3))[None]
            for c in (cache0_t, cache1_t, cache2_t)]
    return (y_p.reshape(bp, s_len, D_MODEL), y_s.reshape(bs, t_new, D_MODEL),
            kv_p[0], kv_p[1], kv_p[2], kv_s[0], kv_s[1], kv_s[2],
            vn_s.reshape(1, bs, t_new, D_B))
```
